```python
import math
import jax, jax.numpy as jnp
from jax import lax
import numpy as np

D_MODEL = 1024
BATCH = 2
SEQ = 8192
DEPTH = 1

DA_HEADS = 4
DA_QK_DIM = 64
DA_V_DIM = 2 * DA_QK_DIM
DA_QK_WIDTH = DA_HEADS * 2 * DA_QK_DIM
DA_WIDTH = DA_HEADS * DA_V_DIM
DN_HEADS = 4
DN_K_DIM = 128
DN_V_DIM = 128
DN_WIDTH = DN_HEADS * DN_V_DIM
DN_QKV_WIDTH = DN_HEADS * (2 * DN_K_DIM + DN_V_DIM)
CONV_WIDTH = 4
CHUNK = 64
MIX_WIDTH = DA_WIDTH + DN_WIDTH
D_FF = 4 * D_MODEL
Q_BLOCK = 128
EPS = 1e-6
IN_WIDTH = 2 * DA_QK_WIDTH + DA_WIDTH + DN_QKV_WIDTH + DN_WIDTH + 2 * DN_HEADS
SPLIT_1 = DA_QK_WIDTH
SPLIT_2 = SPLIT_1 + DA_QK_WIDTH
SPLIT_3 = SPLIT_2 + DA_WIDTH
SPLIT_4 = SPLIT_3 + DN_QKV_WIDTH
SPLIT_5 = SPLIT_4 + DN_WIDTH
SPLIT_6 = SPLIT_5 + DN_HEADS

kernel_name = "hymba_diffattn_gdn_sqrelu"


def rms_norm(x, w):
    xf = x.astype(jnp.float32)
    y = xf * lax.rsqrt(jnp.mean(xf * xf, axis=-1, keepdims=True) + EPS)
    return (y * w.astype(jnp.float32)).astype(x.dtype)


def l2_norm(x):
    xf = x.astype(jnp.float32)
    return xf * lax.rsqrt(jnp.sum(xf * xf, axis=-1, keepdims=True) + EPS)


def diff_attention(q, k, v, lam):
    B, S = q.shape[0], q.shape[1]
    nb = S // Q_BLOCK
    scale = DA_QK_DIM ** -0.5
    k_pos = jnp.arange(S)
    q_blocks = jnp.moveaxis(q.reshape(B, nb, Q_BLOCK, DA_HEADS, 2, DA_QK_DIM), 1, 0)
    starts = jnp.arange(nb) * Q_BLOCK

    def block(args):
        qb, start = args
        s = jnp.einsum('bqhcd,bkhcd->bhcqk', qb, k,
                       preferred_element_type=jnp.float32) * scale
        causal = (start + jnp.arange(Q_BLOCK))[:, None] >= k_pos[None, :]
        s = jnp.where(causal, s, jnp.finfo(jnp.float32).min)
        p = jax.nn.softmax(s, axis=-1)
        a = p[:, :, 0] - lam * p[:, :, 1]
        return jnp.einsum('bhqk,bkhe->bqhe', a.astype(v.dtype), v)

    out = lax.map(block, (q_blocks, starts))
    return jnp.moveaxis(out, 0, 1).reshape(B, S, DA_HEADS, DA_V_DIM)


def causal_conv_silu(x, w):
    C = x.shape[-1]
    y = lax.conv_general_dilated(
        x, w[:, None, :].astype(x.dtype), window_strides=(1,),
        padding=[(CONV_WIDTH - 1, 0)], dimension_numbers=('NWC', 'WIO', 'NWC'),
        feature_group_count=C)
    return jax.nn.silu(y)


def gated_delta_rule(q, k, v, g, beta):
    B, S, H, Dk = k.shape
    Dv = v.shape[-1]
    n = S // CHUNK

    def chunks(t):
        t = jnp.moveaxis(t.astype(jnp.float32), 2, 1)
        return t.reshape((B, H, n, CHUNK) + t.shape[3:])

    q, k, v, g, beta = (chunks(q * Dk ** -0.5), chunks(k), chunks(v), chunks(g), chunks(beta))
    g = jnp.cumsum(g, axis=-1)
    tri = jnp.tril(jnp.ones((CHUNK, CHUNK), dtype=bool))
    strict = jnp.tril(jnp.ones((CHUNK, CHUNK), dtype=bool), -1)
    gdiff = g[..., :, None] - g[..., None, :]
    decay = jnp.where(tri, jnp.exp(jnp.where(tri, gdiff, 0.0)), 0.0)
    k_beta = k * beta[..., None]
    v_beta = v * beta[..., None]
    Lmat = jnp.where(strict, jnp.einsum('bhncd,bhnjd->bhncj', k_beta, k) * decay, 0.0)
    rhs = jnp.concatenate([v_beta, k_beta * jnp.exp(g)[..., None]], axis=-1)
    sol = lax.linalg.triangular_solve(Lmat, rhs, left_side=True, lower=True, unit_diagonal=True)
    u, w = sol[..., :Dv], sol[..., Dv:]
    qk = jnp.where(tri, jnp.einsum('bhncd,bhnjd->bhncj', q, k) * decay, 0.0)
    xs = tuple(jnp.moveaxis(t, 2, 0) for t in (q, k, u, w, qk, g))

    def step(state, inp):
        q_c, k_c, u_c, w_c, qk_c, g_c = inp
        v_new = u_c - jnp.einsum('bhcd,bhde->bhce', w_c, state)
        o = (jnp.einsum('bhcd,bhde->bhce', q_c * jnp.exp(g_c)[..., None], state)
             + jnp.einsum('bhcj,bhje->bhce', qk_c, v_new))
        g_last = g_c[..., -1]
        k_dec = k_c * jnp.exp(g_last[..., None] - g_c)[..., None]
        state = state * jnp.exp(g_last)[..., None, None] + jnp.einsum('bhcd,bhce->bhde', k_dec, v_new)
        return state, o

    state0 = jnp.zeros((B, H, Dk, Dv), jnp.float32)
    _, o = lax.scan(step, state0, xs)
    o = jnp.moveaxis(o, 0, 2).reshape(B, H, S, Dv)
    return jnp.moveaxis(o, 1, 2)


def setup_inputs(seed: int = 0) -> dict:
    key = jax.random.key(seed)
    ks = jax.random.split(key, 20)
    f32 = jnp.float32
    nrm = lambda k, shape, s: jax.random.normal(k, shape, f32) * s
    dt = jnp.exp(jax.random.uniform(ks[12], (DEPTH, DN_HEADS), f32, math.log(1e-3), math.log(1e-1)))
    return {
        "x": jax.random.normal(ks[0], (BATCH, SEQ, D_MODEL), f32),
        "norm1_w": 1.0 + nrm(ks[1], (DEPTH, D_MODEL), 0.02),
        "w_in": nrm(ks[2], (DEPTH, D_MODEL, IN_WIDTH), D_MODEL ** -0.5),
        "lambda_q1": nrm(ks[3], (DEPTH, DA_QK_DIM), 0.1),
        "lambda_k1": nrm(ks[4], (DEPTH, DA_QK_DIM), 0.1),
        "lambda_q2": nrm(ks[5], (DEPTH, DA_QK_DIM), 0.1),
        "lambda_k2": nrm(ks[6], (DEPTH, DA_QK_DIM), 0.1),
        "q_norm_w": 1.0 + nrm(ks[7], (DEPTH, DA_QK_DIM), 0.02),
        "k_norm_w": 1.0 + nrm(ks[8], (DEPTH, DA_QK_DIM), 0.02),
        "da_out_norm_w": 1.0 + nrm(ks[9], (DEPTH, DA_V_DIM), 0.02),
        "conv_w": nrm(ks[10], (DEPTH, CONV_WIDTH, DN_QKV_WIDTH), CONV_WIDTH ** -0.5),
        "A_log": jnp.log(jax.random.uniform(ks[11], (DEPTH, DN_HEADS), f32, 1.0, 16.0)),
        "dt_bias": dt + jnp.log(-jnp.expm1(-dt)),
        "dn_out_norm_w": 1.0 + nrm(ks[13], (DEPTH, DN_V_DIM), 0.02),
        "w_out": nrm(ks[14], (DEPTH, MIX_WIDTH, D_MODEL), MIX_WIDTH ** -0.5),
        "norm2_w": 1.0 + nrm(ks[15], (DEPTH, D_MODEL), 0.02),
        "w_up": nrm(ks[16], (DEPTH, D_MODEL, D_FF), D_MODEL ** -0.5),
        "w_down": nrm(ks[17], (DEPTH, D_FF, D_MODEL), D_FF ** -0.5),
    }


def reference(x, norm1_w, w_in, lambda_q1, lambda_k1, lambda_q2, lambda_k2, q_norm_w, k_norm_w,
              da_out_norm_w, conv_w, A_log, dt_bias, dn_out_norm_w, w_out, norm2_w, w_up, w_down):
    B, S = x.shape[0], x.shape[1]
    for l in range(DEPTH):
        h = rms_norm(x, norm1_w[l])
        proj = h @ w_in[l]
        da_q, da_k, da_v, dn_qkv, dn_z, dn_a, dn_b = jnp.split(
            proj, [SPLIT_1, SPLIT_2, SPLIT_3, SPLIT_4, SPLIT_5, SPLIT_6], axis=-1)

        da_q = rms_norm(da_q.reshape(B, S, DA_HEADS, 2, DA_QK_DIM), q_norm_w[l])
        da_k = rms_norm(da_k.reshape(B, S, DA_HEADS, 2, DA_QK_DIM), k_norm_w[l])
        da_v = da_v.reshape(B, S, DA_HEADS, DA_V_DIM)
        lam_init = 0.8 - 0.6 * math.exp(-0.3 * l)
        lam = (jnp.exp(jnp.sum(lambda_q1[l].astype(jnp.float32) * lambda_k1[l].astype(jnp.float32)))
               - jnp.exp(jnp.sum(lambda_q2[l].astype(jnp.float32) * lambda_k2[l].astype(jnp.float32)))
               + lam_init)
        da_o = diff_attention(da_q, da_k, da_v, lam)
        da_o = rms_norm(da_o, da_out_norm_w[l]) * (1.0 - lam_init)

        dn_qkv = causal_conv_silu(dn_qkv, conv_w[l])
        dn_q, dn_k, dn_v = jnp.split(dn_qkv, [DN_HEADS * DN_K_DIM, 2 * DN_HEADS * DN_K_DIM], axis=-1)
        dn_q = l2_norm(dn_q.reshape(B, S, DN_HEADS, DN_K_DIM))
        dn_k = l2_norm(dn_k.reshape(B, S, DN_HEADS, DN_K_DIM))
        dn_v = dn_v.reshape(B, S, DN_HEADS, DN_V_DIM)
        beta = jax.nn.sigmoid(dn_b.astype(jnp.float32))
        g = -jnp.exp(A_log[l].astype(jnp.float32)) * jax.nn.softplus(
            dn_a.astype(jnp.float32) + dt_bias[l].astype(jnp.float32))
        dn_o = gated_delta_rule(dn_q, dn_k, dn_v, g, beta).astype(x.dtype)
        dn_o = rms_norm(dn_o, dn_out_norm_w[l]) * jax.nn.silu(dn_z.reshape(B, S, DN_HEADS, DN_V_DIM))

        mix = jnp.concatenate([da_o.reshape(B, S, DA_WIDTH), dn_o.reshape(B, S, DN_WIDTH)], axis=-1)
        x = x + mix @ w_out[l]

        h = rms_norm(x, norm2_w[l])
        x = x + jnp.square(jax.nn.relu(h @ w_up[l])) @ w_down[l]
    return x
```

```python
import functools
import math

import jax
import jax.numpy as jnp
from jax import lax
from jax.experimental import pallas as pl
from jax.experimental.pallas import tpu as pltpu

F32 = jnp.float32
BF16 = jnp.bfloat16
EPS = 1e-6
LANES = 128
NEG_BIG = -1e30

DA_HEADS = 4
DA_QK_DIM = 64
DN_HEADS = 4
DN_DIM = 128
CONV_WIDTH = 4
HEAD_COLS = 512
GDN_CHUNK = 128
VMEM_LIMIT = 56 * 1024 * 1024


def _dot(a, b):
    return jnp.dot(a, b, preferred_element_type=F32)


def _dot_nt(a, b):
    return lax.dot_general(a, b, (((1,), (1,)), ((), ())), preferred_element_type=F32)


def _dot_tn(a, b):
    return lax.dot_general(a, b, (((0,), (0,)), ((), ())), preferred_element_type=F32)


def _split3(x):
    x1 = x.astype(BF16)
    r1 = x - x1.astype(F32)
    x2 = r1.astype(BF16)
    x3 = (r1 - x2.astype(F32)).astype(BF16)
    return x1, x2, x3


def _inproj_body(x_ref, n1_ref, wa_ref, wdn_ref, wz_ref, wab_ref, qnw_ref, knw_ref,
                 q_out, k_out, v_out, dn_out, z_out, ab_out):
    x = x_ref[...]
    ms = jnp.mean(x * x, axis=-1, keepdims=True)
    h = (x * lax.rsqrt(ms + EPS) * n1_ref[...]).astype(BF16)

    qkv = _dot(h, wa_ref[...])
    tm = x.shape[0]
    lo = lax.broadcasted_iota(jnp.int32, (tm, LANES), 1) < DA_QK_DIM

    def qk_norm(t, w):
        sq = t * t
        s_lo = jnp.sum(jnp.where(lo, sq, 0.0), axis=-1, keepdims=True)
        s_hi = jnp.sum(jnp.where(lo, 0.0, sq), axis=-1, keepdims=True)
        ms_ = jnp.where(lo, s_lo, s_hi) * (1.0 / DA_QK_DIM)
        return t * lax.rsqrt(ms_ + EPS) * w

    for hd in range(DA_HEADS):
        c0 = hd * LANES
        qh = qk_norm(qkv[:, c0:c0 + LANES], qnw_ref[...]) * (DA_QK_DIM ** -0.5)
        kh = qk_norm(qkv[:, HEAD_COLS + c0:HEAD_COLS + c0 + LANES], knw_ref[...])
        q_out[:, c0:c0 + LANES] = qh.astype(BF16)
        k_out[:, c0:c0 + LANES] = kh.astype(BF16)
    v_out[...] = qkv[:, 2 * HEAD_COLS:].astype(BF16)

    dn_out[...] = _dot(h, wdn_ref[...])
    z_out[...] = _dot(h, wz_ref[...])
    ab_out[...] = _dot(h, wab_ref[...])


def _inproj(x2, n1, wa, wdn, wz, wab, qnw, knw, tm):
    T, D = x2.shape
    const = lambda i: (0, 0)
    row = lambda i: (i, 0)
    return pl.pallas_call(
        _inproj_body,
        grid=(T // tm,),
        in_specs=[
            pl.BlockSpec((tm, D), row),
            pl.BlockSpec((1, D), const),
            pl.BlockSpec(wa.shape, const),
            pl.BlockSpec(wdn.shape, const),
            pl.BlockSpec(wz.shape, const),
            pl.BlockSpec(wab.shape, const),
            pl.BlockSpec((1, LANES), const),
            pl.BlockSpec((1, LANES), const),
        ],
        out_specs=[
            pl.BlockSpec((tm, HEAD_COLS), row),
            pl.BlockSpec((tm, HEAD_COLS), row),
            pl.BlockSpec((tm, HEAD_COLS), row),
            pl.BlockSpec((tm, 3 * HEAD_COLS), row),
            pl.BlockSpec((tm, HEAD_COLS), row),
            pl.BlockSpec((tm, LANES), row),
        ],
        out_shape=[
            jax.ShapeDtypeStruct((T, HEAD_COLS), BF16),
            jax.ShapeDtypeStruct((T, HEAD_COLS), BF16),
            jax.ShapeDtypeStruct((T, HEAD_COLS), BF16),
            jax.ShapeDtypeStruct((T, 3 * HEAD_COLS), F32),
            jax.ShapeDtypeStruct((T, HEAD_COLS), F32),
            jax.ShapeDtypeStruct((T, LANES), F32),
        ],
        compiler_params=pltpu.CompilerParams(
            dimension_semantics=("arbitrary",), vmem_limit_bytes=VMEM_LIMIT),
        name="inproj",
    )(x2, n1, wa, wdn, wz, wab, qnw, knw)


def _attn_body(q_ref, k_ref, v_ref, lq1_ref, lk1_ref, lq2_ref, lk2_ref, onw_ref, o_ref, *, tq, lam_init):
    i = pl.program_id(2)
    q = q_ref[...]
    lane = lax.broadcasted_iota(jnp.int32, (tq, LANES), 1)
    zero = jnp.zeros_like(q)
    qz = jnp.concatenate([jnp.where(lane < DA_QK_DIM, q, zero),
                          jnp.where(lane < DA_QK_DIM, zero, q)], axis=0)

    def step(j, carry, diagonal):
        m, l, acc = carry
        r0 = pl.multiple_of(j * tq, tq)
        kj = k_ref[pl.ds(r0, tq), :]
        vj = v_ref[pl.ds(r0, tq), :]
        s = _dot_nt(qz, kj)
        if diagonal:
            rows = lax.broadcasted_iota(jnp.int32, (2 * tq, tq), 0)
            cols = lax.broadcasted_iota(jnp.int32, (2 * tq, tq), 1)
            rows = jnp.where(rows >= tq, rows - tq, rows)
            s = jnp.where(rows >= cols, s, NEG_BIG)
        m_new = jnp.maximum(m, jnp.max(s, axis=-1, keepdims=True))
        p = jnp.exp(s - m_new)
        alpha = jnp.exp(m - m_new)
        l = alpha * l + jnp.sum(p, axis=-1, keepdims=True)
        acc = alpha * acc + _dot(p.astype(BF16), vj)
        return m_new, l, acc

    init = (jnp.full((2 * tq, 1), NEG_BIG, F32), jnp.zeros((2 * tq, 1), F32),
            jnp.zeros((2 * tq, LANES), F32))
    carry = lax.fori_loop(0, i, lambda j, c: step(j, c, False), init)
    m, l, acc = step(i, carry, True)

    lam = (jnp.exp(jnp.sum(lq1_ref[...] * lk1_ref[...], axis=-1, keepdims=True))
           - jnp.exp(jnp.sum(lq2_ref[...] * lk2_ref[...], axis=-1, keepdims=True)) + lam_init)
    o = acc[:tq] / l[:tq] - lam * (acc[tq:] / l[tq:])
    ms = jnp.mean(o * o, axis=-1, keepdims=True)
    o = o * lax.rsqrt(ms + EPS) * onw_ref[...] * (1.0 - lam_init)
    o_ref[...] = o.astype(BF16)


def _attention(q, k, v, lq1, lk1, lq2, lk2, onw, B, S, tq, lam_init):
    nq = S // tq
    vec = pl.BlockSpec((1, DA_QK_DIM), lambda b, h, i: (0, 0))
    return pl.pallas_call(
        functools.partial(_attn_body, tq=tq, lam_init=lam_init),
        grid=(B, DA_HEADS, nq),
        in_specs=[
            pl.BlockSpec((tq, LANES), lambda b, h, i: (b * nq + i, h)),
            pl.BlockSpec((S, LANES), lambda b, h, i: (b, h)),
            pl.BlockSpec((S, LANES), lambda b, h, i: (b, h)),
            vec, vec, vec, vec,
            pl.BlockSpec((1, LANES), lambda b, h, i: (0, 0)),
        ],
        out_specs=pl.BlockSpec((tq, LANES), lambda b, h, i: (b * nq + i, h)),
        out_shape=jax.ShapeDtypeStruct((B * S, HEAD_COLS), BF16),
        compiler_params=pltpu.CompilerParams(
            dimension_semantics=("arbitrary", "arbitrary", "arbitrary"), vmem_limit_bytes=VMEM_LIMIT),
        name="diff_attention",
    )(q, k, v, lq1, lk1, lq2, lk2, onw)


def _unit_lower_inverse(L, row, col):
    n = L.shape[0]
    eye = (row == col).astype(F32)
    T = eye - jnp.where((row == col + 1) & (row % 2 == 1), L, 0.0)
    b = 2
    while b < n:
        off = ((row // b) == (col // b) + 1) & ((row // (2 * b)) == (col // (2 * b)))
        Lo = jnp.where(off, L, 0.0).astype(BF16)
        Tb = T.astype(BF16)
        T = T - _dot(Tb, _dot(Lo, Tb).astype(BF16))
        b *= 2
    return T


def _gdn_body(x_ref, z_ref, ab_ref, cw_ref, alog_ref, dtb_ref, onw_ref, o_ref,
              state_ref, tail_ref, ext_ref, y_ref, gc_ref, gct_ref, beta_ref, *, tb):
    C = GDN_CHUNK
    blk = pl.program_id(1)

    @pl.when(blk == 0)
    def _():
        state_ref[...] = jnp.zeros_like(state_ref)
        tail_ref[...] = jnp.zeros_like(tail_ref)

    ext_ref[0:8, :] = tail_ref[...]
    ext_ref[8:8 + tb, :] = x_ref[...]
    tail_ref[...] = x_ref[tb - 8:tb, :]
    y = cw_ref[CONV_WIDTH - 1:CONV_WIDTH, :] * ext_ref[8:8 + tb, :]
    for w in range(CONV_WIDTH - 1):
        off = 8 - (CONV_WIDTH - 1) + w
        y = y + cw_ref[w:w + 1, :] * ext_ref[off:off + tb, :]
    y_ref[...] = y * (1.0 / (1.0 + jnp.exp(-y)))

    ab = ab_ref[...]
    sp_in = ab + dtb_ref[...]
    softplus = jnp.maximum(sp_in, 0.0) + jnp.log(1.0 + jnp.exp(-jnp.abs(sp_in)))
    gtok = -jnp.exp(alog_ref[...]) * softplus
    lane_ok = lax.broadcasted_iota(jnp.int32, (tb, LANES), 1) < DN_HEADS
    gtok = jnp.where(lane_ok, gtok, 0.0)
    beta_ref[...] = 1.0 / (1.0 + jnp.exp(-ab))
    r = lax.broadcasted_iota(jnp.int32, (tb, tb), 0)
    c = lax.broadcasted_iota(jnp.int32, (tb, tb), 1)
    tri = (((r // C) == (c // C)) & (c <= r)).astype(BF16)
    g1, g2, g3 = _split3(gtok)
    gc = _dot(tri, g1) + _dot(tri, g2) + _dot(tri, g3)
    gc_ref[...] = gc
    gct_ref[...] = gc.T

    row = lax.broadcasted_iota(jnp.int32, (C, C), 0)
    col = lax.broadcasted_iota(jnp.int32, (C, C), 1)
    lower = row >= col
    strict = row > col

    def chunk(ci, _):
        r0 = pl.multiple_of(ci * C, C)
        gcol_all = gc_ref[pl.ds(r0, C), :]
        grow_all = gct_ref[0:8, pl.ds(r0, C)]
        beta_all = beta_ref[pl.ds(r0, C), :]
        for h in range(DN_HEADS):
            c0 = h * DN_DIM
            qh = y_ref[pl.ds(r0, C), c0:c0 + DN_DIM]
            kh = y_ref[pl.ds(r0, C), HEAD_COLS + c0:HEAD_COLS + c0 + DN_DIM]
            vh = y_ref[pl.ds(r0, C), 2 * HEAD_COLS + c0:2 * HEAD_COLS + c0 + DN_DIM]
            qh = qh * lax.rsqrt(jnp.sum(qh * qh, axis=-1, keepdims=True) + EPS) * (DN_DIM ** -0.5)
            kh = kh * lax.rsqrt(jnp.sum(kh * kh, axis=-1, keepdims=True) + EPS)
            gcol = gcol_all[:, h:h + 1]
            grow = grow_all[h:h + 1, :]
            beta = beta_all[:, DN_HEADS + h:DN_HEADS + h + 1]
            decay = jnp.where(lower, jnp.exp(jnp.where(lower, gcol - grow, 0.0)), 0.0)
            kb = kh * beta
            vb = vh * beta
            kbf = kh.astype(BF16)
            a = _dot_nt(jnp.concatenate([kb, qh], axis=0).astype(BF16), kbf)
            Lm = jnp.where(strict, a[:C] * decay, 0.0)
            qk = jnp.where(lower, a[C:] * decay, 0.0)
            T = _unit_lower_inverse(Lm, row, col)
            eg = jnp.exp(gcol)
            rhs = jnp.concatenate([vb, kb * eg], axis=1).astype(BF16)
            uw = _dot(T.astype(BF16), rhs)
            u, wm = uw[:, :DN_DIM], uw[:, DN_DIM:]
            state = state_ref[h]
            ws = _dot(jnp.concatenate([wm, qh * eg], axis=0).astype(BF16), state.astype(BF16))
            v_new = u - ws[:C]
            v_new_bf = v_new.astype(BF16)
            o = ws[C:] + _dot(qk.astype(BF16), v_new_bf)
            g_last = gcol[C - 1:C, :]
            k_dec = (kh * jnp.exp(g_last - gcol)).astype(BF16)
            state_ref[h] = state * jnp.exp(g_last) + _dot_tn(k_dec, v_new_bf)
            zh = z_ref[pl.ds(r0, C), c0:c0 + DN_DIM]
            o = o * lax.rsqrt(jnp.mean(o * o, axis=-1, keepdims=True) + EPS) * onw_ref[...]
            o = o * (zh * (1.0 / (1.0 + jnp.exp(-zh))))
            o_ref[pl.ds(r0, C), c0:c0 + DN_DIM] = o.astype(BF16)
        return 0

    lax.fori_loop(0, tb // C, chunk, 0)


def _gdn(dn_pre, z, ab, cw, alog, dtb, onw, B, S, tb):
    nb = S // tb
    W = dn_pre.shape[1]
    const = lambda b, i: (0, 0)
    row = lambda b, i: (b * nb + i, 0)
    return pl.pallas_call(
        functools.partial(_gdn_body, tb=tb),
        grid=(B, nb),
        in_specs=[
            pl.BlockSpec((tb, W), row),
            pl.BlockSpec((tb, HEAD_COLS), row),
            pl.BlockSpec((tb, LANES), row),
            pl.BlockSpec((CONV_WIDTH, W), const),
            pl.BlockSpec((1, LANES), const),
            pl.BlockSpec((1, LANES), const),
            pl.BlockSpec((1, LANES), const),
        ],
        out_specs=pl.BlockSpec((tb, HEAD_COLS), row),
        out_shape=jax.ShapeDtypeStruct((B * S, HEAD_COLS), BF16),
        scratch_shapes=[
            pltpu.VMEM((DN_HEADS, DN_DIM, DN_DIM), F32),
            pltpu.VMEM((8, W), F32),
            pltpu.VMEM((tb + 8, W), F32),
            pltpu.VMEM((tb, W), F32),
            pltpu.VMEM((tb, LANES), F32),
            pltpu.VMEM((LANES, tb), F32),
            pltpu.VMEM((tb, LANES), F32),
        ],
        compiler_params=pltpu.CompilerParams(
            dimension_semantics=("arbitrary", "arbitrary"), vmem_limit_bytes=VMEM_LIMIT),
        name="gated_deltanet",
    )(dn_pre, z, ab, cw, alog, dtb, onw)


def _mlp_body(x_ref, ma_ref, mb_ref, woa_ref, wob_ref, n2_ref, wup_ref, wdn_ref, o_ref, *, ff_chunk):
    x1 = x_ref[...] + _dot(ma_ref[...], woa_ref[...]) + _dot(mb_ref[...], wob_ref[...])
    ms = jnp.mean(x1 * x1, axis=-1, keepdims=True)
    h = (x1 * lax.rsqrt(ms + EPS) * n2_ref[...]).astype(BF16)
    mlp = None
    for c0 in range(0, wup_ref.shape[1], ff_chunk):
        up = jnp.maximum(_dot(h, wup_ref[:, c0:c0 + ff_chunk]), 0.0)
        down = _dot((up * up).astype(BF16), wdn_ref[c0:c0 + ff_chunk, :])
        mlp = down if mlp is None else mlp + down
    o_ref[...] = x1 + mlp


def _mlp(x2, mix_a, mix_b, woa, wob, n2, wup, wdn, tm, ff_chunk):
    T, D = x2.shape
    const = lambda i: (0, 0)
    row = lambda i: (i, 0)
    return pl.pallas_call(
        functools.partial(_mlp_body, ff_chunk=ff_chunk),
        grid=(T // tm,),
        in_specs=[
            pl.BlockSpec((tm, D), row),
            pl.BlockSpec((tm, HEAD_COLS), row),
            pl.BlockSpec((tm, HEAD_COLS), row),
            pl.BlockSpec(woa.shape, const),
            pl.BlockSpec(wob.shape, const),
            pl.BlockSpec((1, D), const),
            pl.BlockSpec(wup.shape, const),
            pl.BlockSpec(wdn.shape, const),
        ],
        out_specs=pl.BlockSpec((tm, D), row),
        out_shape=jax.ShapeDtypeStruct((T, D), F32),
        compiler_params=pltpu.CompilerParams(
            dimension_semantics=("arbitrary",), vmem_limit_bytes=VMEM_LIMIT),
        name="outproj_mlp",
    )(x2, mix_a, mix_b, woa, wob, n2, wup, wdn)


def _layer(x2, B, S, l, norm1_w, w_in, lambda_q1, lambda_k1, lambda_q2, lambda_k2, q_norm_w, k_norm_w,
           da_out_norm_w, conv_w, A_log, dt_bias, dn_out_norm_w, w_out, norm2_w, w_up, w_down,
           tm, tq, tb, ff_chunk):
    D = x2.shape[1]
    qkv_a = 3 * HEAD_COLS
    dn_cols = 3 * HEAD_COLS
    s4 = qkv_a + dn_cols
    s5 = s4 + HEAD_COLS
    w = w_in[l].astype(BF16)
    wa, wdn, wz = w[:, :qkv_a], w[:, qkv_a:s4], w[:, s4:s5]
    wab = jnp.pad(w[:, s5:], ((0, 0), (0, LANES - 2 * DN_HEADS)))
    tile2 = lambda v: jnp.concatenate([v, v]).reshape(1, LANES).astype(F32)
    pad_lanes = lambda v: jnp.pad(v.astype(F32), (0, LANES - v.shape[0])).reshape(1, LANES)

    q, k, v, dn_pre, z, ab = _inproj(
        x2, norm1_w[l].reshape(1, D).astype(F32), wa, wdn, wz, wab,
        tile2(q_norm_w[l]), tile2(k_norm_w[l]), tm)

    lam_init = 0.8 - 0.6 * math.exp(-0.3 * l)
    vec = lambda p: p[l].reshape(1, DA_QK_DIM).astype(F32)
    mix_a = _attention(q, k, v, vec(lambda_q1), vec(lambda_k1), vec(lambda_q2), vec(lambda_k2),
                       da_out_norm_w[l].reshape(1, LANES).astype(F32), B, S, tq, lam_init)

    mix_b = _gdn(dn_pre, z, ab, conv_w[l].astype(F32), pad_lanes(A_log[l]), pad_lanes(dt_bias[l]),
                 dn_out_norm_w[l].reshape(1, LANES).astype(F32), B, S, tb)

    wo = w_out[l].astype(BF16)
    return _mlp(x2, mix_a, mix_b, wo[:HEAD_COLS], wo[HEAD_COLS:], norm2_w[l].reshape(1, D).astype(F32),
                w_up[l].astype(BF16), w_down[l].astype(BF16), tm, ff_chunk)


def kernel(x, norm1_w, w_in, lambda_q1, lambda_k1, lambda_q2, lambda_k2, q_norm_w, k_norm_w, da_out_norm_w,
           conv_w, A_log, dt_bias, dn_out_norm_w, w_out, norm2_w, w_up, w_down):
    B, S, D = x.shape
    x2 = x.reshape(B * S, D)
    tm = min(512, S)
    tq = min(512, S)
    tb = min(512, S)
    for l in range(w_in.shape[0]):
        x2 = _layer(x2, B, S, l, norm1_w, w_in, lambda_q1, lambda_k1, lambda_q2, lambda_k2, q_norm_w,
                    k_norm_w, da_out_norm_w, conv_w, A_log, dt_bias, dn_out_norm_w, w_out, norm2_w,
                    w_up, w_down, tm, tq, tb, 1024)
    return x2.reshape(B, S, D)
```

```python
import functools
import math

import jax
import jax.numpy as jnp
from jax import lax
from jax.experimental import pallas as pl
from jax.experimental.pallas import tpu as pltpu

F32 = jnp.float32
BF16 = jnp.bfloat16
EPS = 1e-6
LANES = 128
NEG_BIG = -1e30

DA_HEADS = 4
DA_QK_DIM = 64
DN_HEADS = 4
DN_DIM = 128
CONV_WIDTH = 4
HEAD_COLS = 512
GDN_CHUNK = 128
VMEM_LIMIT = 56 * 1024 * 1024


def _dot(a, b):
    return jnp.dot(a, b, preferred_element_type=F32)


def _dot_nt(a, b):
    return lax.dot_general(a, b, (((1,), (1,)), ((), ())), preferred_element_type=F32)


def _dot_tn(a, b):
    return lax.dot_general(a, b, (((0,), (0,)), ((), ())), preferred_element_type=F32)


def _split3(x):
    x1 = x.astype(BF16)
    r1 = x - x1.astype(F32)
    x2 = r1.astype(BF16)
    x3 = (r1 - x2.astype(F32)).astype(BF16)
    return x1, x2, x3


def _inproj_body(x_ref, n1_ref, wa_ref, wdn_ref, wz_ref, wab_ref, qnw_ref, knw_ref,
                 q_out, k_out, v_out, dn_out, z_out, ab_out):
    x = x_ref[...]
    ms = jnp.mean(x * x, axis=-1, keepdims=True)
    h = (x * lax.rsqrt(ms + EPS) * n1_ref[...]).astype(BF16)

    qkv = _dot(h, wa_ref[...])
    tm = x.shape[0]
    lo = lax.broadcasted_iota(jnp.int32, (tm, LANES), 1) < DA_QK_DIM

    def qk_norm(t, w):
        sq = t * t
        s_lo = jnp.sum(jnp.where(lo, sq, 0.0), axis=-1, keepdims=True)
        s_hi = jnp.sum(jnp.where(lo, 0.0, sq), axis=-1, keepdims=True)
        ms_ = jnp.where(lo, s_lo, s_hi) * (1.0 / DA_QK_DIM)
        return t * lax.rsqrt(ms_ + EPS) * w

    for hd in range(DA_HEADS):
        c0 = hd * LANES
        qh = qk_norm(qkv[:, c0:c0 + LANES], qnw_ref[...]) * (DA_QK_DIM ** -0.5)
        kh = qk_norm(qkv[:, HEAD_COLS + c0:HEAD_COLS + c0 + LANES], knw_ref[...])
        q_out[:, c0:c0 + LANES] = qh.astype(BF16)
        k_out[:, c0:c0 + LANES] = kh.astype(BF16)
    v_out[...] = qkv[:, 2 * HEAD_COLS:].astype(BF16)

    dn_out[...] = _dot(h, wdn_ref[...])
    z_out[...] = _dot(h, wz_ref[...])
    ab_out[...] = _dot(h, wab_ref[...])


def _inproj(x2, n1, wa, wdn, wz, wab, qnw, knw, tm):
    T, D = x2.shape
    const = lambda i: (0, 0)
    row = lambda i: (i, 0)
    return pl.pallas_call(
        _inproj_body,
        grid=(T // tm,),
        in_specs=[
            pl.BlockSpec((tm, D), row),
            pl.BlockSpec((1, D), const),
            pl.BlockSpec(wa.shape, const),
            pl.BlockSpec(wdn.shape, const),
            pl.BlockSpec(wz.shape, const),
            pl.BlockSpec(wab.shape, const),
            pl.BlockSpec((1, LANES), const),
            pl.BlockSpec((1, LANES), const),
        ],
        out_specs=[
            pl.BlockSpec((tm, HEAD_COLS), row),
            pl.BlockSpec((tm, HEAD_COLS), row),
            pl.BlockSpec((tm, HEAD_COLS), row),
            pl.BlockSpec((tm, 3 * HEAD_COLS), row),
            pl.BlockSpec((tm, HEAD_COLS), row),
            pl.BlockSpec((tm, LANES), row),
        ],
        out_shape=[
            jax.ShapeDtypeStruct((T, HEAD_COLS), BF16),
            jax.ShapeDtypeStruct((T, HEAD_COLS), BF16),
            jax.ShapeDtypeStruct((T, HEAD_COLS), BF16),
            jax.ShapeDtypeStruct((T, 3 * HEAD_COLS), F32),
            jax.ShapeDtypeStruct((T, HEAD_COLS), F32),
            jax.ShapeDtypeStruct((T, LANES), F32),
        ],
        compiler_params=pltpu.CompilerParams(
            dimension_semantics=("arbitrary",), vmem_limit_bytes=VMEM_LIMIT),
        name="inproj",
    )(x2, n1, wa, wdn, wz, wab, qnw, knw)


def _attn_body(q_ref, k_ref, v_ref, lq1_ref, lk1_ref, lq2_ref, lk2_ref, onw_ref, o_ref, *, tq, lam_init):
    i = pl.program_id(2)
    q = q_ref[...]
    lane = lax.broadcasted_iota(jnp.int32, (tq, LANES), 1)
    zero = jnp.zeros_like(q)
    qz = jnp.concatenate([jnp.where(lane < DA_QK_DIM, q, zero),
                          jnp.where(lane < DA_QK_DIM, zero, q)], axis=0)

    def step(j, carry, diagonal):
        m, l, acc = carry
        r0 = pl.multiple_of(j * tq, tq)
        kj = k_ref[pl.ds(r0, tq), :]
        vj = v_ref[pl.ds(r0, tq), :]
        s = _dot_nt(qz, kj)
        if diagonal:
            rows = lax.broadcasted_iota(jnp.int32, (2 * tq, tq), 0)
            cols = lax.broadcasted_iota(jnp.int32, (2 * tq, tq), 1)
            rows = jnp.where(rows >= tq, rows - tq, rows)
            s = jnp.where(rows >= cols, s, NEG_BIG)
        m_new = jnp.maximum(m, jnp.max(s, axis=-1, keepdims=True))
        p = jnp.exp(s - m_new)
        alpha = jnp.exp(m - m_new)
        l = alpha * l + jnp.sum(p, axis=-1, keepdims=True)
        acc = alpha * acc + _dot(p.astype(BF16), vj)
        return m_new, l, acc

    init = (jnp.full((2 * tq, 1), NEG_BIG, F32), jnp.zeros((2 * tq, 1), F32),
            jnp.zeros((2 * tq, LANES), F32))
    carry = lax.fori_loop(0, i, lambda j, c: step(j, c, False), init)
    m, l, acc = step(i, carry, True)

    lam = (jnp.exp(jnp.sum(lq1_ref[...] * lk1_ref[...], axis=-1, keepdims=True))
           - jnp.exp(jnp.sum(lq2_ref[...] * lk2_ref[...], axis=-1, keepdims=True)) + lam_init)
    o = acc[:tq] / l[:tq] - lam * (acc[tq:] / l[tq:])
    ms = jnp.mean(o * o, axis=-1, keepdims=True)
    o = o * lax.rsqrt(ms + EPS) * onw_ref[...] * (1.0 - lam_init)
    o_ref[...] = o.astype(BF16)


def _attention(q, k, v, lq1, lk1, lq2, lk2, onw, B, S, tq, lam_init):
    nq = S // tq
    vec = pl.BlockSpec((1, DA_QK_DIM), lambda b, h, i: (0, 0))
    return pl.pallas_call(
        functools.partial(_attn_body, tq=tq, lam_init=lam_init),
        grid=(B, DA_HEADS, nq),
        in_specs=[
            pl.BlockSpec((tq, LANES), lambda b, h, i: (b * nq + i, h)),
            pl.BlockSpec((S, LANES), lambda b, h, i: (b, h)),
            pl.BlockSpec((S, LANES), lambda b, h, i: (b, h)),
            vec, vec, vec, vec,
            pl.BlockSpec((1, LANES), lambda b, h, i: (0, 0)),
        ],
        out_specs=pl.BlockSpec((tq, LANES), lambda b, h, i: (b * nq + i, h)),
        out_shape=jax.ShapeDtypeStruct((B * S, HEAD_COLS), BF16),
        compiler_params=pltpu.CompilerParams(
            dimension_semantics=("arbitrary", "arbitrary", "arbitrary"), vmem_limit_bytes=VMEM_LIMIT),
        name="diff_attention",
    )(q, k, v, lq1, lk1, lq2, lk2, onw)


def _merge_masks(row, col, n):
    masks = []
    b = 2
    while b < n:
        masks.append(((row // b) == (col // b) + 1) & ((row // (2 * b)) == (col // (2 * b))))
        b *= 2
    return masks


def _gdn_body(x_ref, z_ref, ab_ref, cw_ref, alog_ref, dtb_ref, onw_ref, o_ref,
              state_ref, tail_ref, ext_ref, l_ref, t_ref, rhs_ref, u_ref, wq_ref, qk_ref, kdt_ref, egl_ref,
              *, tb, group):
    C = GDN_CHUNK
    nc = tb // C
    blk = pl.program_id(1)

    @pl.when(blk == 0)
    def _():
        state_ref[...] = jnp.zeros_like(state_ref)
        tail_ref[...] = jnp.zeros_like(tail_ref)

    ext_ref[0:8, :] = tail_ref[...]
    ext_ref[8:8 + tb, :] = x_ref[...]
    tail_ref[...] = x_ref[tb - 8:tb, :]

    def conv_silu(r0, c0):
        y = cw_ref[CONV_WIDTH - 1:CONV_WIDTH, c0:c0 + LANES] * ext_ref[8 + r0:8 + r0 + C, c0:c0 + LANES]
        for w in range(CONV_WIDTH - 1):
            off = 8 - (CONV_WIDTH - 1) + w + r0
            y = y + cw_ref[w:w + 1, c0:c0 + LANES] * ext_ref[off:off + C, c0:c0 + LANES]
        return y * (1.0 / (1.0 + jnp.exp(-y)))

    ab = ab_ref[...]
    sp_in = ab + dtb_ref[...]
    softplus = jnp.maximum(sp_in, 0.0) + jnp.log(1.0 + jnp.exp(-jnp.abs(sp_in)))
    gtok = -jnp.exp(alog_ref[...]) * softplus
    gtok = jnp.where(lax.broadcasted_iota(jnp.int32, (tb, LANES), 1) < DN_HEADS, gtok, 0.0)
    beta_all = 1.0 / (1.0 + jnp.exp(-ab))

    row = lax.broadcasted_iota(jnp.int32, (C, C), 0)
    col = lax.broadcasted_iota(jnp.int32, (C, C), 1)
    lower = row >= col
    strict = row > col
    eye = (row == col).astype(F32)
    first_level = (row == col + 1) & (row % 2 == 1)
    masks = _merge_masks(row, col, C)
    tri = lower.astype(BF16)

    for c in range(nc):
        r0 = c * C
        g1, g2, g3 = _split3(gtok[r0:r0 + C])
        gc = _dot(tri, g1) + _dot(tri, g2) + _dot(tri, g3)
        gct = gc.T
        for h in range(DN_HEADS):
            un = c * DN_HEADS + h
            c0 = h * DN_DIM
            qh = conv_silu(r0, c0)
            kh = conv_silu(r0, HEAD_COLS + c0)
            vh = conv_silu(r0, 2 * HEAD_COLS + c0)
            qh = qh * lax.rsqrt(jnp.sum(qh * qh, axis=-1, keepdims=True) + EPS) * (DN_DIM ** -0.5)
            kh = kh * lax.rsqrt(jnp.sum(kh * kh, axis=-1, keepdims=True) + EPS)
            gcol = gc[:, h:h + 1]
            grow = gct[h:h + 1, :]
            beta = beta_all[r0:r0 + C, DN_HEADS + h:DN_HEADS + h + 1]
            decay = jnp.where(lower, jnp.exp(jnp.where(lower, gcol - grow, 0.0)), 0.0)
            kb = kh * beta
            a = _dot_nt(jnp.concatenate([kb, qh], axis=0).astype(BF16), kh.astype(BF16))
            Lm = jnp.where(strict, a[:C] * decay, 0.0)
            l_ref[un] = Lm.astype(BF16)
            t_ref[un] = eye - jnp.where(first_level, Lm, 0.0)
            qk_ref[un] = jnp.where(lower, a[C:] * decay, 0.0).astype(BF16)
            eg = jnp.exp(gcol)
            rhs_ref[un] = jnp.concatenate([vh * beta, kb * eg], axis=1).astype(BF16)
            wq_ref[un, C:2 * C, :] = (qh * eg).astype(BF16)
            g_last = gcol[C - 1:C, :]
            kdt_ref[un] = (kh * jnp.exp(g_last - gcol)).T.astype(BF16)
            egl_ref[un] = jnp.broadcast_to(jnp.exp(g_last), (8, LANES))

    nu = nc * DN_HEADS
    for mask in masks:
        for g0 in range(0, nu, group):
            units = list(range(g0, min(g0 + group, nu)))
            tbs = [t_ref[un].astype(BF16) for un in units]
            ps = [_dot(jnp.where(mask, l_ref[un], jnp.zeros((), BF16)), t).astype(BF16)
                  for un, t in zip(units, tbs)]
            for un, t, p in zip(units, tbs, ps):
                t_ref[un] = t_ref[un] - _dot(t, p)
    for un in range(nu):
        uw = _dot(t_ref[un].astype(BF16), rhs_ref[un])
        u_ref[un] = uw[:, :DN_DIM]
        wq_ref[un, 0:C, :] = uw[:, DN_DIM:].astype(BF16)

    for c in range(nc):
        r0 = c * C
        for h in range(DN_HEADS):
            un = c * DN_HEADS + h
            c0 = h * DN_DIM
            state = state_ref[h]
            ws = _dot(wq_ref[un], state.astype(BF16))
            v_new = (u_ref[un] - ws[:C]).astype(BF16)
            o = ws[C:] + _dot(qk_ref[un], v_new)
            state_ref[h] = state * egl_ref[un][0:1, :] + _dot(kdt_ref[un], v_new)
            zh = z_ref[r0:r0 + C, c0:c0 + DN_DIM]
            o = o * lax.rsqrt(jnp.mean(o * o, axis=-1, keepdims=True) + EPS) * onw_ref[...]
            o = o * (zh * (1.0 / (1.0 + jnp.exp(-zh))))
            o_ref[r0:r0 + C, c0:c0 + DN_DIM] = o.astype(BF16)


def _gdn(dn_pre, z, ab, cw, alog, dtb, onw, B, S, tb):
    nb = S // tb
    W = dn_pre.shape[1]
    C = GDN_CHUNK
    nu = (tb // C) * DN_HEADS
    const = lambda b, i: (0, 0)
    row = lambda b, i: (b * nb + i, 0)
    return pl.pallas_call(
        functools.partial(_gdn_body, tb=tb, group=8),
        grid=(B, nb),
        in_specs=[
            pl.BlockSpec((tb, W), row),
            pl.BlockSpec((tb, HEAD_COLS), row),
            pl.BlockSpec((tb, LANES), row),
            pl.BlockSpec((CONV_WIDTH, W), const),
            pl.BlockSpec((1, LANES), const),
            pl.BlockSpec((1, LANES), const),
            pl.BlockSpec((1, LANES), const),
        ],
        out_specs=pl.BlockSpec((tb, HEAD_COLS), row),
        out_shape=jax.ShapeDtypeStruct((B * S, HEAD_COLS), BF16),
        scratch_shapes=[
            pltpu.VMEM((DN_HEADS, DN_DIM, DN_DIM), F32),
            pltpu.VMEM((8, W), F32),
            pltpu.VMEM((tb + 8, W), F32),
            pltpu.VMEM((nu, C, C), BF16),
            pltpu.VMEM((nu, C, C), F32),
            pltpu.VMEM((nu, C, 2 * DN_DIM), BF16),
            pltpu.VMEM((nu, C, DN_DIM), F32),
            pltpu.VMEM((nu, 2 * C, DN_DIM), BF16),
            pltpu.VMEM((nu, C, C), BF16),
            pltpu.VMEM((nu, DN_DIM, C), BF16),
            pltpu.VMEM((nu, 8, LANES), F32),
        ],
        compiler_params=pltpu.CompilerParams(
            dimension_semantics=("arbitrary", "arbitrary"), vmem_limit_bytes=VMEM_LIMIT),
        name="gated_deltanet",
    )(dn_pre, z, ab, cw, alog, dtb, onw)


def _mlp_body(x_ref, ma_ref, mb_ref, woa_ref, wob_ref, n2_ref, wup_ref, wdn_ref, o_ref, *, ff_chunk):
    x1 = x_ref[...] + _dot(ma_ref[...], woa_ref[...]) + _dot(mb_ref[...], wob_ref[...])
    ms = jnp.mean(x1 * x1, axis=-1, keepdims=True)
    h = (x1 * lax.rsqrt(ms + EPS) * n2_ref[...]).astype(BF16)
    mlp = None
    for c0 in range(0, wup_ref.shape[1], ff_chunk):
        up = jnp.maximum(_dot(h, wup_ref[:, c0:c0 + ff_chunk]), 0.0)
        down = _dot((up * up).astype(BF16), wdn_ref[c0:c0 + ff_chunk, :])
        mlp = down if mlp is None else mlp + down
    o_ref[...] = x1 + mlp


def _mlp(x2, mix_a, mix_b, woa, wob, n2, wup, wdn, tm, ff_chunk):
    T, D = x2.shape
    const = lambda i: (0, 0)
    row = lambda i: (i, 0)
    return pl.pallas_call(
        functools.partial(_mlp_body, ff_chunk=ff_chunk),
        grid=(T // tm,),
        in_specs=[
            pl.BlockSpec((tm, D), row),
            pl.BlockSpec((tm, HEAD_COLS), row),
            pl.BlockSpec((tm, HEAD_COLS), row),
            pl.BlockSpec(woa.shape, const),
            pl.BlockSpec(wob.shape, const),
            pl.BlockSpec((1, D), const),
            pl.BlockSpec(wup.shape, const),
            pl.BlockSpec(wdn.shape, const),
        ],
        out_specs=pl.BlockSpec((tm, D), row),
        out_shape=jax.ShapeDtypeStruct((T, D), F32),
        compiler_params=pltpu.CompilerParams(
            dimension_semantics=("arbitrary",), vmem_limit_bytes=VMEM_LIMIT),
        name="outproj_mlp",
    )(x2, mix_a, mix_b, woa, wob, n2, wup, wdn)


def _layer(x2, B, S, l, norm1_w, w_in, lambda_q1, lambda_k1, lambda_q2, lambda_k2, q_norm_w, k_norm_w,
           da_out_norm_w, conv_w, A_log, dt_bias, dn_out_norm_w, w_out, norm2_w, w_up, w_down,
           tm, tq, tb, ff_chunk):
    D = x2.shape[1]
    qkv_a = 3 * HEAD_COLS
    dn_cols = 3 * HEAD_COLS
    s4 = qkv_a + dn_cols
    s5 = s4 + HEAD_COLS
    w = w_in[l].astype(BF16)
    wa, wdn, wz = w[:, :qkv_a], w[:, qkv_a:s4], w[:, s4:s5]
    wab = jnp.pad(w[:, s5:], ((0, 0), (0, LANES - 2 * DN_HEADS)))
    tile2 = lambda v: jnp.concatenate([v, v]).reshape(1, LANES).astype(F32)
    pad_lanes = lambda v: jnp.pad(v.astype(F32), (0, LANES - v.shape[0])).reshape(1, LANES)

    q, k, v, dn_pre, z, ab = _inproj(
        x2, norm1_w[l].reshape(1, D).astype(F32), wa, wdn, wz, wab,
        tile2(q_norm_w[l]), tile2(k_norm_w[l]), tm)

    lam_init = 0.8 - 0.6 * math.exp(-0.3 * l)
    vec = lambda p: p[l].reshape(1, DA_QK_DIM).astype(F32)
    mix_a = _attention(q, k, v, vec(lambda_q1), vec(lambda_k1), vec(lambda_q2), vec(lambda_k2),
                       da_out_norm_w[l].reshape(1, LANES).astype(F32), B, S, tq, lam_init)

    mix_b = _gdn(dn_pre, z, ab, conv_w[l].astype(F32), pad_lanes(A_log[l]), pad_lanes(dt_bias[l]),
                 dn_out_norm_w[l].reshape(1, LANES).astype(F32), B, S, tb)

    wo = w_out[l].astype(BF16)
    return _mlp(x2, mix_a, mix_b, wo[:HEAD_COLS], wo[HEAD_COLS:], norm2_w[l].reshape(1, D).astype(F32),
                w_up[l].astype(BF16), w_down[l].astype(BF16), tm, ff_chunk)


def kernel(x, norm1_w, w_in, lambda_q1, lambda_k1, lambda_q2, lambda_k2, q_norm_w, k_norm_w, da_out_norm_w,
           conv_w, A_log, dt_bias, dn_out_norm_w, w_out, norm2_w, w_up, w_down):
    B, S, D = x.shape
    x2 = x.reshape(B * S, D)
    tm = min(512, S)
    tq = min(512, S)
    tb = min(512, S)
    for l in range(w_in.shape[0]):
        x2 = _layer(x2, B, S, l, norm1_w, w_in, lambda_q1, lambda_k1, lambda_q2, lambda_k2, q_norm_w,
                    k_norm_w, da_out_norm_w, conv_w, A_log, dt_bias, dn_out_norm_w, w_out, norm2_w,
                    w_up, w_down, tm, tq, tb, 1024)
    return x2.reshape(B, S, D)
```

```python
import functools
import math

import jax
import jax.numpy as jnp
from jax import lax
from jax.experimental import pallas as pl
from jax.experimental.pallas import tpu as pltpu

F32 = jnp.float32
BF16 = jnp.bfloat16
EPS = 1e-6
LANES = 128
NEG_BIG = -1e30
LOG2E = math.log2(math.e)
MAX_FIXED_SHIFT = 40.0

DA_HEADS = 4
DA_QK_DIM = 64
DN_HEADS = 4
DN_DIM = 128
CONV_WIDTH = 4
HEAD_COLS = 512
GDN_CHUNK = 128
VMEM_LIMIT = 56 * 1024 * 1024


def _dot(a, b):
    return jnp.dot(a, b, preferred_element_type=F32)


def _dot_nt(a, b):
    return lax.dot_general(a, b, (((1,), (1,)), ((), ())), preferred_element_type=F32)


def _dot_tn(a, b):
    return lax.dot_general(a, b, (((0,), (0,)), ((), ())), preferred_element_type=F32)


def _split3(x):
    x1 = x.astype(BF16)
    r1 = x - x1.astype(F32)
    x2 = r1.astype(BF16)
    x3 = (r1 - x2.astype(F32)).astype(BF16)
    return x1, x2, x3


def _inproj_body(x_ref, n1_ref, wa_ref, wdn_ref, wz_ref, wab_ref, qnw_ref, knw_ref,
                 q_out, k_out, v_out, dn_out, z_out, ab_out):
    x = x_ref[...]
    ms = jnp.mean(x * x, axis=-1, keepdims=True)
    h = (x * lax.rsqrt(ms + EPS) * n1_ref[...]).astype(BF16)

    qkv = _dot(h, wa_ref[...])
    tm = x.shape[0]
    lo = lax.broadcasted_iota(jnp.int32, (tm, LANES), 1) < DA_QK_DIM

    def qk_norm(t, w):
        sq = t * t
        s_lo = jnp.sum(jnp.where(lo, sq, 0.0), axis=-1, keepdims=True)
        s_hi = jnp.sum(jnp.where(lo, 0.0, sq), axis=-1, keepdims=True)
        ms_ = jnp.where(lo, s_lo, s_hi) * (1.0 / DA_QK_DIM)
        return t * lax.rsqrt(ms_ + EPS) * w

    for hd in range(DA_HEADS):
        c0 = hd * LANES
        qh = qk_norm(qkv[:, c0:c0 + LANES], qnw_ref[...]) * (DA_QK_DIM ** -0.5 * LOG2E)
        kh = qk_norm(qkv[:, HEAD_COLS + c0:HEAD_COLS + c0 + LANES], knw_ref[...])
        q_out[:, c0:c0 + LANES] = qh.astype(BF16)
        k_out[:, c0:c0 + LANES] = kh.astype(BF16)
    v_out[...] = qkv[:, 2 * HEAD_COLS:].astype(BF16)

    dn_out[...] = _dot(h, wdn_ref[...])
    z_out[...] = _dot(h, wz_ref[...])
    ab_out[...] = _dot(h, wab_ref[...])


def _inproj(x2, n1, wa, wdn, wz, wab, qnw, knw, tm):
    T, D = x2.shape
    const = lambda i: (0, 0)
    row = lambda i: (i, 0)
    return pl.pallas_call(
        _inproj_body,
        grid=(T // tm,),
        in_specs=[
            pl.BlockSpec((tm, D), row),
            pl.BlockSpec((1, D), const),
            pl.BlockSpec(wa.shape, const),
            pl.BlockSpec(wdn.shape, const),
            pl.BlockSpec(wz.shape, const),
            pl.BlockSpec(wab.shape, const),
            pl.BlockSpec((1, LANES), const),
            pl.BlockSpec((1, LANES), const),
        ],
        out_specs=[
            pl.BlockSpec((tm, HEAD_COLS), row),
            pl.BlockSpec((tm, HEAD_COLS), row),
            pl.BlockSpec((tm, HEAD_COLS), row),
            pl.BlockSpec((tm, 3 * HEAD_COLS), row),
            pl.BlockSpec((tm, HEAD_COLS), row),
            pl.BlockSpec((tm, LANES), row),
        ],
        out_shape=[
            jax.ShapeDtypeStruct((T, HEAD_COLS), BF16),
            jax.ShapeDtypeStruct((T, HEAD_COLS), BF16),
            jax.ShapeDtypeStruct((T, HEAD_COLS), BF16),
            jax.ShapeDtypeStruct((T, 3 * HEAD_COLS), F32),
            jax.ShapeDtypeStruct((T, HEAD_COLS), F32),
            jax.ShapeDtypeStruct((T, LANES), F32),
        ],
        compiler_params=pltpu.CompilerParams(
            dimension_semantics=("arbitrary",), vmem_limit_bytes=VMEM_LIMIT),
        name="inproj",
    )(x2, n1, wa, wdn, wz, wab, qnw, knw)


def _stack_maps(q_ref, qz_ref, tq):
    q = q_ref[...]
    lane = lax.broadcasted_iota(jnp.int32, (tq, LANES), 1)
    zero = jnp.zeros_like(q)
    qz_ref[0:tq, :] = jnp.where(lane < DA_QK_DIM, q, zero)
    qz_ref[tq:2 * tq, :] = jnp.where(lane < DA_QK_DIM, zero, q)


def _causal_mask(tq):
    rows = lax.broadcasted_iota(jnp.int32, (2 * tq, tq), 0)
    cols = lax.broadcasted_iota(jnp.int32, (2 * tq, tq), 1)
    return jnp.where(rows >= tq, rows - tq, rows) >= cols


def _attn_finish(acc, l, lq1_ref, lk1_ref, lq2_ref, lk2_ref, onw_ref, o_ref, tq, lam_init):
    lam = (jnp.exp(jnp.sum(lq1_ref[...] * lk1_ref[...], axis=-1, keepdims=True))
           - jnp.exp(jnp.sum(lq2_ref[...] * lk2_ref[...], axis=-1, keepdims=True)) + lam_init)
    o = acc[:tq] / l[:tq] - lam * (acc[tq:] / l[tq:])
    ms = jnp.mean(o * o, axis=-1, keepdims=True)
    o = o * lax.rsqrt(ms + EPS) * onw_ref[...] * (1.0 - lam_init)
    o_ref[...] = o.astype(BF16)


def _attn_shift_body(shift_ref, q_ref, k_ref, v_ref, lq1_ref, lk1_ref, lq2_ref, lk2_ref, onw_ref, o_ref,
                     qz_ref, p_ref, l_ref, acc_ref, *, tq, lam_init, unroll):
    i = pl.program_id(2)
    _stack_maps(q_ref, qz_ref, tq)
    p_ref[...] = jnp.zeros_like(p_ref)
    l_ref[...] = jnp.zeros_like(l_ref)
    acc_ref[...] = jnp.zeros_like(acc_ref)
    shift = shift_ref[0]

    def probs(j, diagonal):
        r0 = pl.multiple_of(j * tq, tq)
        s = _dot_nt(qz_ref[...], k_ref[pl.ds(r0, tq), :]) - shift
        if diagonal:
            s = jnp.where(_causal_mask(tq), s, NEG_BIG)
        p = jnp.exp2(s)
        psum = p[:, 0:LANES]
        for c0 in range(LANES, tq, LANES):
            psum = psum + p[:, c0:c0 + LANES]
        return p.astype(BF16), psum

    def pv_prev(j):
        r0 = pl.multiple_of(jnp.maximum(j - 1, 0) * tq, tq)
        return _dot(p_ref[...], v_ref[pl.ds(r0, tq), :])

    def steps(j0, n):
        acc = pv_prev(j0)
        psum = None
        for u in range(n):
            p, ps = probs(j0 + u, False)
            psum = ps if psum is None else psum + ps
            if u + 1 < n:
                acc = acc + _dot(p, v_ref[pl.ds(pl.multiple_of((j0 + u) * tq, tq), tq), :])
            else:
                p_ref[...] = p
        acc_ref[...] += acc
        l_ref[...] += psum

    n_main = i // unroll

    def main(t, _):
        steps(t * unroll, unroll)
        return 0

    def rest(j, _):
        steps(j, 1)
        return 0

    lax.fori_loop(0, n_main, main, 0)
    lax.fori_loop(n_main * unroll, i, rest, 0)
    acc = acc_ref[...] + pv_prev(i)
    p, psum = probs(i, True)
    l = jnp.sum(l_ref[...] + psum, axis=-1, keepdims=True)
    acc = acc + _dot(p, v_ref[pl.ds(pl.multiple_of(i * tq, tq), tq), :])
    _attn_finish(acc, l, lq1_ref, lk1_ref, lq2_ref, lk2_ref, onw_ref, o_ref, tq, lam_init)


def _attn_online_body(shift_ref, q_ref, k_ref, v_ref, lq1_ref, lk1_ref, lq2_ref, lk2_ref, onw_ref, o_ref,
                      qz_ref, *, tq, lam_init):
    del shift_ref
    i = pl.program_id(2)
    _stack_maps(q_ref, qz_ref, tq)

    def step(j, carry, diagonal):
        m, l, acc = carry
        r0 = pl.multiple_of(j * tq, tq)
        s = _dot_nt(qz_ref[...], k_ref[pl.ds(r0, tq), :])
        if diagonal:
            s = jnp.where(_causal_mask(tq), s, NEG_BIG)
        m_new = jnp.maximum(m, jnp.max(s, axis=-1, keepdims=True))
        p = jnp.exp2(s - m_new)
        alpha = jnp.exp2(m - m_new)
        l = alpha * l + jnp.sum(p, axis=-1, keepdims=True)
        acc = alpha * acc + _dot(p.astype(BF16), v_ref[pl.ds(r0, tq), :])
        return m_new, l, acc

    init = (jnp.full((2 * tq, 1), NEG_BIG, F32), jnp.zeros((2 * tq, 1), F32),
            jnp.zeros((2 * tq, LANES), F32))
    carry = lax.fori_loop(0, i, lambda j, c: step(j, c, False), init)
    _, l, acc = step(i, carry, True)
    _attn_finish(acc, l, lq1_ref, lk1_ref, lq2_ref, lk2_ref, onw_ref, o_ref, tq, lam_init)


def _attention(shift, q, k, v, lq1, lk1, lq2, lk2, onw, B, S, tq, lam_init, online):
    nq = S // tq
    vec = pl.BlockSpec((1, DA_QK_DIM), lambda b, h, i: (0, 0))
    if online:
        body = functools.partial(_attn_online_body, tq=tq, lam_init=lam_init)
        scratch = [pltpu.VMEM((2 * tq, LANES), BF16)]
    else:
        body = functools.partial(_attn_shift_body, tq=tq, lam_init=lam_init, unroll=2)
        scratch = [pltpu.VMEM((2 * tq, LANES), BF16),
                   pltpu.VMEM((2 * tq, tq), BF16),
                   pltpu.VMEM((2 * tq, LANES), F32),
                   pltpu.VMEM((2 * tq, LANES), F32)]
    return pl.pallas_call(
        body,
        grid=(B, DA_HEADS, nq),
        in_specs=[
            pl.BlockSpec(memory_space=pltpu.SMEM),
            pl.BlockSpec((tq, LANES), lambda b, h, i: (b * nq + i, h)),
            pl.BlockSpec((S, LANES), lambda b, h, i: (b, h)),
            pl.BlockSpec((S, LANES), lambda b, h, i: (b, h)),
            vec, vec, vec, vec,
            pl.BlockSpec((1, LANES), lambda b, h, i: (0, 0)),
        ],
        out_specs=pl.BlockSpec((tq, LANES), lambda b, h, i: (b * nq + i, h)),
        out_shape=jax.ShapeDtypeStruct((B * S, HEAD_COLS), BF16),
        scratch_shapes=scratch,
        compiler_params=pltpu.CompilerParams(
            dimension_semantics=("arbitrary", "arbitrary", "arbitrary"), vmem_limit_bytes=VMEM_LIMIT),
        name="diff_attention_online" if online else "diff_attention",
    )(shift, q, k, v, lq1, lk1, lq2, lk2, onw)


def _merge_masks(row, col, n):
    masks = []
    b = 2
    while b < n:
        masks.append(((row // b) == (col // b) + 1) & ((row // (2 * b)) == (col // (2 * b))))
        b *= 2
    return masks


def _gdn_body(x_ref, z_ref, ab_ref, cw_ref, alog_ref, dtb_ref, onw_ref, o_ref,
              state_ref, tail_ref, ext_ref, l_ref, t_ref, rhs_ref, u_ref, wq_ref, qk_ref, kdt_ref, egl_ref,
              *, tb, group):
    C = GDN_CHUNK
    nc = tb // C
    blk = pl.program_id(1)

    @pl.when(blk == 0)
    def _():
        state_ref[...] = jnp.zeros_like(state_ref)
        tail_ref[...] = jnp.zeros_like(tail_ref)

    ext_ref[0:8, :] = tail_ref[...]
    ext_ref[8:8 + tb, :] = x_ref[...]
    tail_ref[...] = x_ref[tb - 8:tb, :]

    def conv_silu(r0, c0):
        y = cw_ref[CONV_WIDTH - 1:CONV_WIDTH, c0:c0 + LANES] * ext_ref[8 + r0:8 + r0 + C, c0:c0 + LANES]
        for w in range(CONV_WIDTH - 1):
            off = 8 - (CONV_WIDTH - 1) + w + r0
            y = y + cw_ref[w:w + 1, c0:c0 + LANES] * ext_ref[off:off + C, c0:c0 + LANES]
        return y * (1.0 / (1.0 + jnp.exp(-y)))

    ab = ab_ref[...]
    sp_in = ab + dtb_ref[...]
    softplus = jnp.maximum(sp_in, 0.0) + jnp.log(1.0 + jnp.exp(-jnp.abs(sp_in)))
    gtok = -jnp.exp(alog_ref[...]) * softplus
    gtok = jnp.where(lax.broadcasted_iota(jnp.int32, (tb, LANES), 1) < DN_HEADS, gtok, 0.0)
    beta_all = 1.0 / (1.0 + jnp.exp(-ab))

    row = lax.broadcasted_iota(jnp.int32, (C, C), 0)
    col = lax.broadcasted_iota(jnp.int32, (C, C), 1)
    lower = row >= col
    strict = row > col
    eye = (row == col).astype(F32)
    first_level = (row == col + 1) & (row % 2 == 1)
    masks = _merge_masks(row, col, C)
    tri = lower.astype(BF16)

    for c in range(nc):
        r0 = c * C
        g1, g2, g3 = _split3(gtok[r0:r0 + C])
        gc = _dot(tri, g1) + _dot(tri, g2) + _dot(tri, g3)
        gct = gc.T
        for h in range(DN_HEADS):
            un = c * DN_HEADS + h
            c0 = h * DN_DIM
            qh = conv_silu(r0, c0)
            kh = conv_silu(r0, HEAD_COLS + c0)
            vh = conv_silu(r0, 2 * HEAD_COLS + c0)
            qh = qh * lax.rsqrt(jnp.sum(qh * qh, axis=-1, keepdims=True) + EPS) * (DN_DIM ** -0.5)
            kh = kh * lax.rsqrt(jnp.sum(kh * kh, axis=-1, keepdims=True) + EPS)
            gcol = gc[:, h:h + 1]
            grow = gct[h:h + 1, :]
            beta = beta_all[r0:r0 + C, DN_HEADS + h:DN_HEADS + h + 1]
            decay = jnp.where(lower, jnp.exp(jnp.where(lower, gcol - grow, 0.0)), 0.0)
            kb = kh * beta
            a = _dot_nt(jnp.concatenate([kb, qh], axis=0).astype(BF16), kh.astype(BF16))
            Lm = jnp.where(strict, a[:C] * decay, 0.0)
            l_ref[un] = Lm.astype(BF16)
            t_ref[un] = eye - jnp.where(first_level, Lm, 0.0)
            qk_ref[un] = jnp.where(lower, a[C:] * decay, 0.0).astype(BF16)
            eg = jnp.exp(gcol)
            rhs_ref[un] = jnp.concatenate([vh * beta, kb * eg], axis=1).astype(BF16)
            wq_ref[un, C:2 * C, :] = (qh * eg).astype(BF16)
            g_last = gcol[C - 1:C, :]
            kdt_ref[un] = (kh * jnp.exp(g_last - gcol)).T.astype(BF16)
            egl_ref[un] = jnp.broadcast_to(jnp.exp(g_last), (8, LANES))

    nu = nc * DN_HEADS
    for mask in masks:
        for g0 in range(0, nu, group):
            units = list(range(g0, min(g0 + group, nu)))
            tbs = [t_ref[un].astype(BF16) for un in units]
            ps = [_dot(jnp.where(mask, l_ref[un], jnp.zeros((), BF16)), t).astype(BF16)
                  for un, t in zip(units, tbs)]
            for un, t, p in zip(units, tbs, ps):
                t_ref[un] = t_ref[un] - _dot(t, p)
    for un in range(nu):
        uw = _dot(t_ref[un].astype(BF16), rhs_ref[un])
        u_ref[un] = uw[:, :DN_DIM]
        wq_ref[un, 0:C, :] = uw[:, DN_DIM:].astype(BF16)

    for c in range(nc):
        r0 = c * C
        for h in range(DN_HEADS):
            un = c * DN_HEADS + h
            c0 = h * DN_DIM
            state = state_ref[h]
            ws = _dot(wq_ref[un], state.astype(BF16))
            v_new = (u_ref[un] - ws[:C]).astype(BF16)
            o = ws[C:] + _dot(qk_ref[un], v_new)
            state_ref[h] = state * egl_ref[un][0:1, :] + _dot(kdt_ref[un], v_new)
            zh = z_ref[r0:r0 + C, c0:c0 + DN_DIM]
            o = o * lax.rsqrt(jnp.mean(o * o, axis=-1, keepdims=True) + EPS) * onw_ref[...]
            o = o * (zh * (1.0 / (1.0 + jnp.exp(-zh))))
            o_ref[r0:r0 + C, c0:c0 + DN_DIM] = o.astype(BF16)


def _gdn(dn_pre, z, ab, cw, alog, dtb, onw, B, S, tb):
    nb = S // tb
    W = dn_pre.shape[1]
    C = GDN_CHUNK
    nu = (tb // C) * DN_HEADS
    const = lambda b, i: (0, 0)
    row = lambda b, i: (b * nb + i, 0)
    return pl.pallas_call(
        functools.partial(_gdn_body, tb=tb, group=8),
        grid=(B, nb),
        in_specs=[
            pl.BlockSpec((tb, W), row),
            pl.BlockSpec((tb, HEAD_COLS), row),
            pl.BlockSpec((tb, LANES), row),
            pl.BlockSpec((CONV_WIDTH, W), const),
            pl.BlockSpec((1, LANES), const),
            pl.BlockSpec((1, LANES), const),
            pl.BlockSpec((1, LANES), const),
        ],
        out_specs=pl.BlockSpec((tb, HEAD_COLS), row),
        out_shape=jax.ShapeDtypeStruct((B * S, HEAD_COLS), BF16),
        scratch_shapes=[
            pltpu.VMEM((DN_HEADS, DN_DIM, DN_DIM), F32),
            pltpu.VMEM((8, W), F32),
            pltpu.VMEM((tb + 8, W), F32),
            pltpu.VMEM((nu, C, C), BF16),
            pltpu.VMEM((nu, C, C), F32),
            pltpu.VMEM((nu, C, 2 * DN_DIM), BF16),
            pltpu.VMEM((nu, C, DN_DIM), F32),
            pltpu.VMEM((nu, 2 * C, DN_DIM), BF16),
            pltpu.VMEM((nu, C, C), BF16),
            pltpu.VMEM((nu, DN_DIM, C), BF16),
            pltpu.VMEM((nu, 8, LANES), F32),
        ],
        compiler_params=pltpu.CompilerParams(
            dimension_semantics=("arbitrary", "arbitrary"), vmem_limit_bytes=VMEM_LIMIT),
        name="gated_deltanet",
    )(dn_pre, z, ab, cw, alog, dtb, onw)


def _mlp_body(x_ref, ma_ref, mb_ref, woa_ref, wob_ref, n2_ref, wup_ref, wdn_ref, o_ref, *, ff_chunk):
    x1 = x_ref[...] + _dot(ma_ref[...], woa_ref[...]) + _dot(mb_ref[...], wob_ref[...])
    ms = jnp.mean(x1 * x1, axis=-1, keepdims=True)
    h = (x1 * lax.rsqrt(ms + EPS) * n2_ref[...]).astype(BF16)
    mlp = None
    for c0 in range(0, wup_ref.shape[1], ff_chunk):
        up = jnp.maximum(_dot(h, wup_ref[:, c0:c0 + ff_chunk]), 0.0)
        down = _dot((up * up).astype(BF16), wdn_ref[c0:c0 + ff_chunk, :])
        mlp = down if mlp is None else mlp + down
    o_ref[...] = x1 + mlp


def _mlp(x2, mix_a, mix_b, woa, wob, n2, wup, wdn, tm, ff_chunk):
    T, D = x2.shape
    const = lambda i: (0, 0)
    row = lambda i: (i, 0)
    return pl.pallas_call(
        functools.partial(_mlp_body, ff_chunk=ff_chunk),
        grid=(T // tm,),
        in_specs=[
            pl.BlockSpec((tm, D), row),
            pl.BlockSpec((tm, HEAD_COLS), row),
            pl.BlockSpec((tm, HEAD_COLS), row),
            pl.BlockSpec(woa.shape, const),
            pl.BlockSpec(wob.shape, const),
            pl.BlockSpec((1, D), const),
            pl.BlockSpec(wup.shape, const),
            pl.BlockSpec(wdn.shape, const),
        ],
        out_specs=pl.BlockSpec((tm, D), row),
        out_shape=jax.ShapeDtypeStruct((T, D), F32),
        compiler_params=pltpu.CompilerParams(
            dimension_semantics=("arbitrary",), vmem_limit_bytes=VMEM_LIMIT),
        name="outproj_mlp",
    )(x2, mix_a, mix_b, woa, wob, n2, wup, wdn)


def _layer(x2, B, S, l, norm1_w, w_in, lambda_q1, lambda_k1, lambda_q2, lambda_k2, q_norm_w, k_norm_w,
           da_out_norm_w, conv_w, A_log, dt_bias, dn_out_norm_w, w_out, norm2_w, w_up, w_down,
           tm, tq, tb, ff_chunk):
    D = x2.shape[1]
    qkv_a = 3 * HEAD_COLS
    dn_cols = 3 * HEAD_COLS
    s4 = qkv_a + dn_cols
    s5 = s4 + HEAD_COLS
    w = w_in[l].astype(BF16)
    wa, wdn, wz = w[:, :qkv_a], w[:, qkv_a:s4], w[:, s4:s5]
    wab = jnp.pad(w[:, s5:], ((0, 0), (0, LANES - 2 * DN_HEADS)))
    tile2 = lambda v: jnp.concatenate([v, v]).reshape(1, LANES).astype(F32)
    pad_lanes = lambda v: jnp.pad(v.astype(F32), (0, LANES - v.shape[0])).reshape(1, LANES)

    q, k, v, dn_pre, z, ab = _inproj(
        x2, norm1_w[l].reshape(1, D).astype(F32), wa, wdn, wz, wab,
        tile2(q_norm_w[l]), tile2(k_norm_w[l]), tm)

    lam_init = 0.8 - 0.6 * math.exp(-0.3 * l)
    vec = lambda p: p[l].reshape(1, DA_QK_DIM).astype(F32)
    bound = 8.0 * jnp.max(jnp.abs(q_norm_w[l].astype(F32))) * jnp.max(jnp.abs(k_norm_w[l].astype(F32)))
    attn_args = ((bound * LOG2E).reshape(1), q, k, v, vec(lambda_q1), vec(lambda_k1), vec(lambda_q2),
                 vec(lambda_k2), da_out_norm_w[l].reshape(1, LANES).astype(F32))
    mix_a = lax.cond(bound <= MAX_FIXED_SHIFT,
                     lambda a: _attention(*a, B, S, tq, lam_init, online=False),
                     lambda a: _attention(*a, B, S, tq, lam_init, online=True), attn_args)

    mix_b = _gdn(dn_pre, z, ab, conv_w[l].astype(F32), pad_lanes(A_log[l]), pad_lanes(dt_bias[l]),
                 dn_out_norm_w[l].reshape(1, LANES).astype(F32), B, S, tb)

    wo = w_out[l].astype(BF16)
    return _mlp(x2, mix_a, mix_b, wo[:HEAD_COLS], wo[HEAD_COLS:], norm2_w[l].reshape(1, D).astype(F32),
                w_up[l].astype(BF16), w_down[l].astype(BF16), tm, ff_chunk)


def kernel(x, norm1_w, w_in, lambda_q1, lambda_k1, lambda_q2, lambda_k2, q_norm_w, k_norm_w, da_out_norm_w,
           conv_w, A_log, dt_bias, dn_out_norm_w, w_out, norm2_w, w_up, w_down):
    B, S, D = x.shape
    x2 = x.reshape(B * S, D)
    tm = min(512, S)
    tq = min(512, S)
    tb = min(512, S)
    for l in range(w_in.shape[0]):
        x2 = _layer(x2, B, S, l, norm1_w, w_in, lambda_q1, lambda_k1, lambda_q2, lambda_k2, q_norm_w,
                    k_norm_w, da_out_norm_w, conv_w, A_log, dt_bias, dn_out_norm_w, w_out, norm2_w,
                    w_up, w_down, tm, tq, tb, 1024)
    return x2.reshape(B, S, D)
```

```python
import functools
import math

import jax
import jax.numpy as jnp
from jax import lax
from jax.experimental import pallas as pl
from jax.experimental.pallas import tpu as pltpu

F32 = jnp.float32
BF16 = jnp.bfloat16
EPS = 1e-6
LANES = 128
NEG_BIG = -1e30
LOG2E = math.log2(math.e)
MAX_FIXED_SHIFT = 40.0

DA_HEADS = 4
DA_QK_DIM = 64
DN_HEADS = 4
DN_DIM = 128
CONV_WIDTH = 4
HEAD_COLS = 512
GDN_CHUNK = 128
VMEM_LIMIT = 56 * 1024 * 1024


def _dot(a, b):
    return jnp.dot(a, b, preferred_element_type=F32)


def _dot_nt(a, b):
    return lax.dot_general(a, b, (((1,), (1,)), ((), ())), preferred_element_type=F32)


def _dot_tn(a, b):
    return lax.dot_general(a, b, (((0,), (0,)), ((), ())), preferred_element_type=F32)


def _silu(x):
    h = 0.5 * x
    return h + h * jnp.tanh(h)


def _split3(x):
    x1 = x.astype(BF16)
    r1 = x - x1.astype(F32)
    x2 = r1.astype(BF16)
    x3 = (r1 - x2.astype(F32)).astype(BF16)
    return x1, x2, x3


def _inproj_body(x_ref, n1_ref, wa_ref, wdn_ref, wz_ref, wab_ref, qnw_ref, knw_ref, cw_ref,
                 q_out, k_out, v_out, gq_out, gk_out, gv_out, z_out, ab_out, tail_ref, ext_ref,
                 *, tiles_per_seq):
    tm = x_ref.shape[0]
    x = x_ref[...]
    ms = jnp.mean(x * x, axis=-1, keepdims=True)
    h = (x * lax.rsqrt(ms + EPS) * n1_ref[...]).astype(BF16)

    @pl.when(pl.program_id(0) % tiles_per_seq == 0)
    def _():
        tail_ref[...] = jnp.zeros_like(tail_ref)

    dn = _dot(h, wdn_ref[...])
    ext_ref[0:8, :] = tail_ref[...]
    ext_ref[8:8 + tm, :] = dn
    tail_ref[...] = dn[tm - 8:tm, :]
    for r0 in range(0, tm, GDN_CHUNK):
        for cb in range(3 * DN_HEADS):
            c0 = cb * LANES
            a = ext_ref[r0:r0 + GDN_CHUNK + 8, c0:c0 + LANES]
            y = cw_ref[CONV_WIDTH - 1:CONV_WIDTH, c0:c0 + LANES] * a[8:]
            for w in range(CONV_WIDTH - 1):
                shifted = pltpu.roll(a, CONV_WIDTH - 1 - w, axis=0)[8:]
                y = y + cw_ref[w:w + 1, c0:c0 + LANES] * shifted
            y = _silu(y)
            if cb < DN_HEADS:
                y = y * (lax.rsqrt(jnp.sum(y * y, axis=-1, keepdims=True) + EPS) * (DN_DIM ** -0.5))
            elif cb < 2 * DN_HEADS:
                y = y * lax.rsqrt(jnp.sum(y * y, axis=-1, keepdims=True) + EPS)
            out = (gq_out, gk_out, gv_out)[cb // DN_HEADS]
            c1 = (cb % DN_HEADS) * LANES
            out[r0:r0 + GDN_CHUNK, c1:c1 + LANES] = y.astype(BF16)

    qkv = _dot(h, wa_ref[...])
    lo = lax.broadcasted_iota(jnp.int32, (tm, LANES), 1) < DA_QK_DIM

    def qk_norm(t, w):
        sq = t * t
        s_lo = jnp.sum(jnp.where(lo, sq, 0.0), axis=-1, keepdims=True)
        s_hi = jnp.sum(jnp.where(lo, 0.0, sq), axis=-1, keepdims=True)
        ms_ = jnp.where(lo, s_lo, s_hi) * (1.0 / DA_QK_DIM)
        return t * lax.rsqrt(ms_ + EPS) * w

    for hd in range(DA_HEADS):
        c0 = hd * LANES
        qh = qk_norm(qkv[:, c0:c0 + LANES], qnw_ref[...]) * (DA_QK_DIM ** -0.5 * LOG2E)
        kh = qk_norm(qkv[:, HEAD_COLS + c0:HEAD_COLS + c0 + LANES], knw_ref[...])
        q_out[:, c0:c0 + LANES] = qh.astype(BF16)
        k_out[:, c0:c0 + LANES] = kh.astype(BF16)
    v_out[...] = qkv[:, 2 * HEAD_COLS:].astype(BF16)

    z_out[...] = _dot(h, wz_ref[...])
    ab_out[...] = _dot(h, wab_ref[...])


def _inproj(x2, n1, wa, wdn, wz, wab, qnw, knw, cw, S, tm):
    T, D = x2.shape
    W = wdn.shape[1]
    const = lambda i: (0, 0)
    row = lambda i: (i, 0)
    slab = pl.BlockSpec((tm, HEAD_COLS), row)
    slab_bf16 = jax.ShapeDtypeStruct((T, HEAD_COLS), BF16)
    return pl.pallas_call(
        functools.partial(_inproj_body, tiles_per_seq=S // tm),
        grid=(T // tm,),
        in_specs=[
            pl.BlockSpec((tm, D), row),
            pl.BlockSpec((1, D), const),
            pl.BlockSpec(wa.shape, const),
            pl.BlockSpec(wdn.shape, const),
            pl.BlockSpec(wz.shape, const),
            pl.BlockSpec(wab.shape, const),
            pl.BlockSpec((1, LANES), const),
            pl.BlockSpec((1, LANES), const),
            pl.BlockSpec((CONV_WIDTH, W), const),
        ],
        out_specs=[slab, slab, slab, slab, slab, slab, slab, pl.BlockSpec((tm, LANES), row)],
        out_shape=[slab_bf16, slab_bf16, slab_bf16, slab_bf16, slab_bf16, slab_bf16,
                   jax.ShapeDtypeStruct((T, HEAD_COLS), F32),
                   jax.ShapeDtypeStruct((T, LANES), F32)],
        scratch_shapes=[
            pltpu.VMEM((8, W), F32),
            pltpu.VMEM((tm + 8, W), F32),
        ],
        compiler_params=pltpu.CompilerParams(
            dimension_semantics=("arbitrary",), vmem_limit_bytes=VMEM_LIMIT),
        name="inproj",
    )(x2, n1, wa, wdn, wz, wab, qnw, knw, cw)


def _stack_maps(q_ref, qz_ref, tq):
    q = q_ref[...]
    lane = lax.broadcasted_iota(jnp.int32, (tq, LANES), 1)
    zero = jnp.zeros_like(q)
    qz_ref[0:tq, :] = jnp.where(lane < DA_QK_DIM, q, zero)
    qz_ref[tq:2 * tq, :] = jnp.where(lane < DA_QK_DIM, zero, q)


def _causal_mask(tq):
    rows = lax.broadcasted_iota(jnp.int32, (2 * tq, tq), 0)
    cols = lax.broadcasted_iota(jnp.int32, (2 * tq, tq), 1)
    return jnp.where(rows >= tq, rows - tq, rows) >= cols


def _attn_finish(acc, l, lq1_ref, lk1_ref, lq2_ref, lk2_ref, onw_ref, o_ref, tq, lam_init):
    lam = (jnp.exp(jnp.sum(lq1_ref[...] * lk1_ref[...], axis=-1, keepdims=True))
           - jnp.exp(jnp.sum(lq2_ref[...] * lk2_ref[...], axis=-1, keepdims=True)) + lam_init)
    o = acc[:tq] / l[:tq] - lam * (acc[tq:] / l[tq:])
    ms = jnp.mean(o * o, axis=-1, keepdims=True)
    o = o * lax.rsqrt(ms + EPS) * onw_ref[...] * (1.0 - lam_init)
    o_ref[...] = o.astype(BF16)


def _attn_shift_body(shift_ref, q_ref, k_ref, v_ref, lq1_ref, lk1_ref, lq2_ref, lk2_ref, onw_ref, o_ref,
                     qz_ref, p_ref, l_ref, acc_ref, *, tq, lam_init, unroll):
    i = pl.program_id(2)
    _stack_maps(q_ref, qz_ref, tq)
    p_ref[...] = jnp.zeros_like(p_ref)
    l_ref[...] = jnp.zeros_like(l_ref)
    acc_ref[...] = jnp.zeros_like(acc_ref)
    shift = shift_ref[0]

    def probs(j, diagonal):
        r0 = pl.multiple_of(j * tq, tq)
        s = _dot_nt(qz_ref[...], k_ref[pl.ds(r0, tq), :]) - shift
        if diagonal:
            s = jnp.where(_causal_mask(tq), s, NEG_BIG)
        p = jnp.exp2(s)
        psum = p[:, 0:LANES]
        for c0 in range(LANES, tq, LANES):
            psum = psum + p[:, c0:c0 + LANES]
        return p.astype(BF16), psum

    def pv_prev(j):
        r0 = pl.multiple_of(jnp.maximum(j - 1, 0) * tq, tq)
        return _dot(p_ref[...], v_ref[pl.ds(r0, tq), :])

    def steps(j0, n):
        acc = pv_prev(j0)
        psum = None
        for u in range(n):
            p, ps = probs(j0 + u, False)
            psum = ps if psum is None else psum + ps
            if u + 1 < n:
                acc = acc + _dot(p, v_ref[pl.ds(pl.multiple_of((j0 + u) * tq, tq), tq), :])
            else:
                p_ref[...] = p
        acc_ref[...] += acc
        l_ref[...] += psum

    n_main = i // unroll

    def main(t, _):
        steps(t * unroll, unroll)
        return 0

    def rest(j, _):
        steps(j, 1)
        return 0

    lax.fori_loop(0, n_main, main, 0)
    lax.fori_loop(n_main * unroll, i, rest, 0)
    acc = acc_ref[...] + pv_prev(i)
    p, psum = probs(i, True)
    l = jnp.sum(l_ref[...] + psum, axis=-1, keepdims=True)
    acc = acc + _dot(p, v_ref[pl.ds(pl.multiple_of(i * tq, tq), tq), :])
    _attn_finish(acc, l, lq1_ref, lk1_ref, lq2_ref, lk2_ref, onw_ref, o_ref, tq, lam_init)


def _attn_online_body(shift_ref, q_ref, k_ref, v_ref, lq1_ref, lk1_ref, lq2_ref, lk2_ref, onw_ref, o_ref,
                      qz_ref, *, tq, lam_init):
    del shift_ref
    i = pl.program_id(2)
    _stack_maps(q_ref, qz_ref, tq)

    def step(j, carry, diagonal):
        m, l, acc = carry
        r0 = pl.multiple_of(j * tq, tq)
        s = _dot_nt(qz_ref[...], k_ref[pl.ds(r0, tq), :])
        if diagonal:
            s = jnp.where(_causal_mask(tq), s, NEG_BIG)
        m_new = jnp.maximum(m, jnp.max(s, axis=-1, keepdims=True))
        p = jnp.exp2(s - m_new)
        alpha = jnp.exp2(m - m_new)
        l = alpha * l + jnp.sum(p, axis=-1, keepdims=True)
        acc = alpha * acc + _dot(p.astype(BF16), v_ref[pl.ds(r0, tq), :])
        return m_new, l, acc

    init = (jnp.full((2 * tq, 1), NEG_BIG, F32), jnp.zeros((2 * tq, 1), F32),
            jnp.zeros((2 * tq, LANES), F32))
    carry = lax.fori_loop(0, i, lambda j, c: step(j, c, False), init)
    _, l, acc = step(i, carry, True)
    _attn_finish(acc, l, lq1_ref, lk1_ref, lq2_ref, lk2_ref, onw_ref, o_ref, tq, lam_init)


def _attention(shift, q, k, v, lq1, lk1, lq2, lk2, onw, B, S, tq, lam_init, online):
    nq = S // tq
    vec = pl.BlockSpec((1, DA_QK_DIM), lambda b, h, i: (0, 0))
    if online:
        body = functools.partial(_attn_online_body, tq=tq, lam_init=lam_init)
        scratch = [pltpu.VMEM((2 * tq, LANES), BF16)]
    else:
        body = functools.partial(_attn_shift_body, tq=tq, lam_init=lam_init, unroll=2)
        scratch = [pltpu.VMEM((2 * tq, LANES), BF16),
                   pltpu.VMEM((2 * tq, tq), BF16),
                   pltpu.VMEM((2 * tq, LANES), F32),
                   pltpu.VMEM((2 * tq, LANES), F32)]
    return pl.pallas_call(
        body,
        grid=(B, DA_HEADS, nq),
        in_specs=[
            pl.BlockSpec(memory_space=pltpu.SMEM),
            pl.BlockSpec((tq, LANES), lambda b, h, i: (b * nq + i, h)),
            pl.BlockSpec((S, LANES), lambda b, h, i: (b, h)),
            pl.BlockSpec((S, LANES), lambda b, h, i: (b, h)),
            vec, vec, vec, vec,
            pl.BlockSpec((1, LANES), lambda b, h, i: (0, 0)),
        ],
        out_specs=pl.BlockSpec((tq, LANES), lambda b, h, i: (b * nq + i, h)),
        out_shape=jax.ShapeDtypeStruct((B * S, HEAD_COLS), BF16),
        scratch_shapes=scratch,
        compiler_params=pltpu.CompilerParams(
            dimension_semantics=("arbitrary", "arbitrary", "arbitrary"), vmem_limit_bytes=VMEM_LIMIT),
        name="diff_attention_online" if online else "diff_attention",
    )(shift, q, k, v, lq1, lk1, lq2, lk2, onw)


def _merge_masks(row, col, n):
    masks = []
    b = 2
    while b < n:
        masks.append(((row // b) == (col // b) + 1) & ((row // (2 * b)) == (col // (2 * b))))
        b *= 2
    return masks


def _gdn_body(q_ref, k_ref, v_ref, z_ref, ab_ref, alog_ref, dtb_ref, onw_ref, o_ref,
              state_ref, l_ref, t_ref, rhs_ref, u_ref, wq_ref, qk_ref, kdt_ref, egl_ref,
              *, tb, group):
    C = GDN_CHUNK
    nc = tb // C
    blk = pl.program_id(1)

    @pl.when(blk == 0)
    def _():
        state_ref[...] = jnp.zeros_like(state_ref)

    ab = ab_ref[...]
    sp_in = ab + dtb_ref[...]
    softplus = jnp.maximum(sp_in, 0.0) + jnp.log(1.0 + jnp.exp(-jnp.abs(sp_in)))
    gtok = -jnp.exp(alog_ref[...]) * softplus
    gtok = jnp.where(lax.broadcasted_iota(jnp.int32, (tb, LANES), 1) < DN_HEADS, gtok, 0.0)
    beta_all = 1.0 / (1.0 + jnp.exp(-ab))

    row = lax.broadcasted_iota(jnp.int32, (C, C), 0)
    col = lax.broadcasted_iota(jnp.int32, (C, C), 1)
    lower = row >= col
    strict = row > col
    eye = (row == col).astype(F32)
    first_level = (row == col + 1) & (row % 2 == 1)
    masks = _merge_masks(row, col, C)
    tri = lower.astype(BF16)

    for c in range(nc):
        r0 = c * C
        g1, g2, g3 = _split3(gtok[r0:r0 + C])
        gc = _dot(tri, g1) + _dot(tri, g2) + _dot(tri, g3)
        gct = gc.T
        for h in range(DN_HEADS):
            un = c * DN_HEADS + h
            c0 = h * DN_DIM
            qh = q_ref[r0:r0 + C, c0:c0 + DN_DIM].astype(F32)
            kh_bf = k_ref[r0:r0 + C, c0:c0 + DN_DIM]
            kh = kh_bf.astype(F32)
            vh = v_ref[r0:r0 + C, c0:c0 + DN_DIM].astype(F32)
            gcol = gc[:, h:h + 1]
            grow = gct[h:h + 1, :]
            beta = beta_all[r0:r0 + C, DN_HEADS + h:DN_HEADS + h + 1]
            decay = jnp.where(lower, jnp.exp(jnp.where(lower, gcol - grow, 0.0)), 0.0)
            kb = kh * beta
            a = _dot_nt(jnp.concatenate([kb, qh], axis=0).astype(BF16), kh_bf)
            Lm = jnp.where(strict, a[:C] * decay, 0.0)
            l_ref[un] = Lm.astype(BF16)
            t_ref[un] = eye - jnp.where(first_level, Lm, 0.0)
            qk_ref[un] = jnp.where(lower, a[C:] * decay, 0.0).astype(BF16)
            eg = jnp.exp(gcol)
            rhs_ref[un] = jnp.concatenate([vh * beta, kb * eg], axis=1).astype(BF16)
            wq_ref[un, C:2 * C, :] = (qh * eg).astype(BF16)
            g_last = gcol[C - 1:C, :]
            kdt_ref[un] = (kh * jnp.exp(g_last - gcol)).T.astype(BF16)
            egl_ref[un] = jnp.broadcast_to(jnp.exp(g_last), (8, LANES))

    nu = nc * DN_HEADS
    for mask in masks:
        for g0 in range(0, nu, group):
            units = list(range(g0, min(g0 + group, nu)))
            tbs = [t_ref[un].astype(BF16) for un in units]
            ps = [_dot(jnp.where(mask, l_ref[un], jnp.zeros((), BF16)), t).astype(BF16)
                  for un, t in zip(units, tbs)]
            for un, t, p in zip(units, tbs, ps):
                t_ref[un] = t_ref[un] - _dot(t, p)
    for un in range(nu):
        uw = _dot(t_ref[un].astype(BF16), rhs_ref[un])
        u_ref[un] = uw[:, :DN_DIM]
        wq_ref[un, 0:C, :] = uw[:, DN_DIM:].astype(BF16)

    for c in range(nc):
        r0 = c * C
        for h in range(DN_HEADS):
            un = c * DN_HEADS + h
            c0 = h * DN_DIM
            state = state_ref[h]
            ws = _dot(wq_ref[un], state.astype(BF16))
            v_new = (u_ref[un] - ws[:C]).astype(BF16)
            o = ws[C:] + _dot(qk_ref[un], v_new)
            state_ref[h] = state * egl_ref[un][0:1, :] + _dot(kdt_ref[un], v_new)
            zh = z_ref[r0:r0 + C, c0:c0 + DN_DIM]
            o = o * lax.rsqrt(jnp.mean(o * o, axis=-1, keepdims=True) + EPS) * onw_ref[...]
            o = o * _silu(zh)
            o_ref[r0:r0 + C, c0:c0 + DN_DIM] = o.astype(BF16)


def _gdn(gq, gk, gv, z, ab, alog, dtb, onw, B, S, tb):
    nb = S // tb
    C = GDN_CHUNK
    nu = (tb // C) * DN_HEADS
    const = lambda b, i: (0, 0)
    row = lambda b, i: (b * nb + i, 0)
    return pl.pallas_call(
        functools.partial(_gdn_body, tb=tb, group=8),
        grid=(B, nb),
        in_specs=[
            pl.BlockSpec((tb, HEAD_COLS), row),
            pl.BlockSpec((tb, HEAD_COLS), row),
            pl.BlockSpec((tb, HEAD_COLS), row),
            pl.BlockSpec((tb, HEAD_COLS), row),
            pl.BlockSpec((tb, LANES), row),
            pl.BlockSpec((1, LANES), const),
            pl.BlockSpec((1, LANES), const),
            pl.BlockSpec((1, LANES), const),
        ],
        out_specs=pl.BlockSpec((tb, HEAD_COLS), row),
        out_shape=jax.ShapeDtypeStruct((B * S, HEAD_COLS), BF16),
        scratch_shapes=[
            pltpu.VMEM((DN_HEADS, DN_DIM, DN_DIM), F32),
            pltpu.VMEM((nu, C, C), BF16),
            pltpu.VMEM((nu, C, C), F32),
            pltpu.VMEM((nu, C, 2 * DN_DIM), BF16),
            pltpu.VMEM((nu, C, DN_DIM), F32),
            pltpu.VMEM((nu, 2 * C, DN_DIM), BF16),
            pltpu.VMEM((nu, C, C), BF16),
            pltpu.VMEM((nu, DN_DIM, C), BF16),
            pltpu.VMEM((nu, 8, LANES), F32),
        ],
        compiler_params=pltpu.CompilerParams(
            dimension_semantics=("arbitrary", "arbitrary"), vmem_limit_bytes=VMEM_LIMIT),
        name="gated_deltanet",
    )(gq, gk, gv, z, ab, alog, dtb, onw)


def _mlp_body(x_ref, ma_ref, mb_ref, woa_ref, wob_ref, n2_ref, wup_ref, wdn_ref, o_ref, *, ff_chunk):
    x1 = x_ref[...] + _dot(ma_ref[...], woa_ref[...]) + _dot(mb_ref[...], wob_ref[...])
    ms = jnp.mean(x1 * x1, axis=-1, keepdims=True)
    h = (x1 * lax.rsqrt(ms + EPS) * n2_ref[...]).astype(BF16)
    mlp = None
    for c0 in range(0, wup_ref.shape[1], ff_chunk):
        up = jnp.maximum(_dot(h, wup_ref[:, c0:c0 + ff_chunk]), 0.0)
        down = _dot((up * up).astype(BF16), wdn_ref[c0:c0 + ff_chunk, :])
        mlp = down if mlp is None else mlp + down
    o_ref[...] = x1 + mlp


def _mlp(x2, mix_a, mix_b, woa, wob, n2, wup, wdn, tm, ff_chunk):
    T, D = x2.shape
    const = lambda i: (0, 0)
    row = lambda i: (i, 0)
    return pl.pallas_call(
        functools.partial(_mlp_body, ff_chunk=ff_chunk),
        grid=(T // tm,),
        in_specs=[
            pl.BlockSpec((tm, D), row),
            pl.BlockSpec((tm, HEAD_COLS), row),
            pl.BlockSpec((tm, HEAD_COLS), row),
            pl.BlockSpec(woa.shape, const),
            pl.BlockSpec(wob.shape, const),
            pl.BlockSpec((1, D), const),
            pl.BlockSpec(wup.shape, const),
            pl.BlockSpec(wdn.shape, const),
        ],
        out_specs=pl.BlockSpec((tm, D), row),
        out_shape=jax.ShapeDtypeStruct((T, D), F32),
        compiler_params=pltpu.CompilerParams(
            dimension_semantics=("arbitrary",), vmem_limit_bytes=VMEM_LIMIT),
        name="outproj_mlp",
    )(x2, mix_a, mix_b, woa, wob, n2, wup, wdn)


def _layer(x2, B, S, l, norm1_w, w_in, lambda_q1, lambda_k1, lambda_q2, lambda_k2, q_norm_w, k_norm_w,
           da_out_norm_w, conv_w, A_log, dt_bias, dn_out_norm_w, w_out, norm2_w, w_up, w_down,
           tm, tq, tb, ff_chunk):
    D = x2.shape[1]
    qkv_a = 3 * HEAD_COLS
    dn_cols = 3 * HEAD_COLS
    s4 = qkv_a + dn_cols
    s5 = s4 + HEAD_COLS
    w = w_in[l].astype(BF16)
    wa, wdn, wz = w[:, :qkv_a], w[:, qkv_a:s4], w[:, s4:s5]
    wab = jnp.pad(w[:, s5:], ((0, 0), (0, LANES - 2 * DN_HEADS)))
    tile2 = lambda v: jnp.concatenate([v, v]).reshape(1, LANES).astype(F32)
    pad_lanes = lambda v: jnp.pad(v.astype(F32), (0, LANES - v.shape[0])).reshape(1, LANES)

    q, k, v, gq, gk, gv, z, ab = _inproj(
        x2, norm1_w[l].reshape(1, D).astype(F32), wa, wdn, wz, wab,
        tile2(q_norm_w[l]), tile2(k_norm_w[l]), conv_w[l].astype(F32), S, tm)

    lam_init = 0.8 - 0.6 * math.exp(-0.3 * l)
    vec = lambda p: p[l].reshape(1, DA_QK_DIM).astype(F32)
    bound = 8.0 * jnp.max(jnp.abs(q_norm_w[l].astype(F32))) * jnp.max(jnp.abs(k_norm_w[l].astype(F32)))
    attn_args = ((bound * LOG2E).reshape(1), q, k, v, vec(lambda_q1), vec(lambda_k1), vec(lambda_q2),
                 vec(lambda_k2), da_out_norm_w[l].reshape(1, LANES).astype(F32))
    mix_a = lax.cond(bound <= MAX_FIXED_SHIFT,
                     lambda a: _attention(*a, B, S, tq, lam_init, online=False),
                     lambda a: _attention(*a, B, S, tq, lam_init, online=True), attn_args)

    mix_b = _gdn(gq, gk, gv, z, ab, pad_lanes(A_log[l]), pad_lanes(dt_bias[l]),
                 dn_out_norm_w[l].reshape(1, LANES).astype(F32), B, S, tb)

    wo = w_out[l].astype(BF16)
    return _mlp(x2, mix_a, mix_b, wo[:HEAD_COLS], wo[HEAD_COLS:], norm2_w[l].reshape(1, D).astype(F32),
                w_up[l].astype(BF16), w_down[l].astype(BF16), tm, ff_chunk)


def kernel(x, norm1_w, w_in, lambda_q1, lambda_k1, lambda_q2, lambda_k2, q_norm_w, k_norm_w, da_out_norm_w,
           conv_w, A_log, dt_bias, dn_out_norm_w, w_out, norm2_w, w_up, w_down):
    B, S, D = x.shape
    x2 = x.reshape(B * S, D)
    tm = min(512, S)
    tq = min(512, S)
    tb = min(512, S)
    for l in range(w_in.shape[0]):
        x2 = _layer(x2, B, S, l, norm1_w, w_in, lambda_q1, lambda_k1, lambda_q2, lambda_k2, q_norm_w,
                    k_norm_w, da_out_norm_w, conv_w, A_log, dt_bias, dn_out_norm_w, w_out, norm2_w,
                    w_up, w_down, tm, tq, tb, 1024)
    return x2.reshape(B, S, D)
```

```python
import functools
import math

import jax
import jax.numpy as jnp
from jax import lax
from jax.experimental import pallas as pl
from jax.experimental.pallas import tpu as pltpu

F32 = jnp.float32
BF16 = jnp.bfloat16
EPS = 1e-6
LANES = 128
NEG_BIG = -1e30
LOG2E = math.log2(math.e)
MAX_FIXED_SHIFT = 40.0

DA_HEADS = 4
DA_QK_DIM = 64
DN_HEADS = 4
DN_DIM = 128
CONV_WIDTH = 4
HEAD_COLS = 512
GDN_CHUNK = 128
VMEM_LIMIT = 56 * 1024 * 1024


def _dot(a, b):
    return jnp.dot(a, b, preferred_element_type=F32)


def _dot_nt(a, b):
    return lax.dot_general(a, b, (((1,), (1,)), ((), ())), preferred_element_type=F32)


def _dot_tn(a, b):
    return lax.dot_general(a, b, (((0,), (0,)), ((), ())), preferred_element_type=F32)


def _silu(x):
    h = 0.5 * x
    return h + h * jnp.tanh(h)


def _split3(x):
    x1 = x.astype(BF16)
    r1 = x - x1.astype(F32)
    x2 = r1.astype(BF16)
    x3 = (r1 - x2.astype(F32)).astype(BF16)
    return x1, x2, x3


def _inproj_body(x_ref, n1_ref, wa_ref, wdn_ref, wz_ref, wab_ref, qnw_ref, knw_ref, cw_ref,
                 q_out, k_out, v_out, gq_out, gk_out, gv_out, z_out, ab_out, tail_ref, ext_ref,
                 *, tiles_per_seq):
    tm = x_ref.shape[0]
    x = x_ref[...]
    ms = jnp.mean(x * x, axis=-1, keepdims=True)
    h = (x * lax.rsqrt(ms + EPS) * n1_ref[...]).astype(BF16)

    @pl.when(pl.program_id(0) % tiles_per_seq == 0)
    def _():
        tail_ref[...] = jnp.zeros_like(tail_ref)

    dn = _dot(h, wdn_ref[...])
    ext_ref[0:8, :] = tail_ref[...]
    ext_ref[8:8 + tm, :] = dn
    tail_ref[...] = dn[tm - 8:tm, :]
    for r0 in range(0, tm, GDN_CHUNK):
        for cb in range(3 * DN_HEADS):
            c0 = cb * LANES
            a = ext_ref[r0:r0 + GDN_CHUNK + 8, c0:c0 + LANES]
            y = cw_ref[CONV_WIDTH - 1:CONV_WIDTH, c0:c0 + LANES] * a[8:]
            for w in range(CONV_WIDTH - 1):
                shifted = pltpu.roll(a, CONV_WIDTH - 1 - w, axis=0)[8:]
                y = y + cw_ref[w:w + 1, c0:c0 + LANES] * shifted
            y = _silu(y)
            if cb < DN_HEADS:
                y = y * (lax.rsqrt(jnp.sum(y * y, axis=-1, keepdims=True) + EPS) * (DN_DIM ** -0.5))
            elif cb < 2 * DN_HEADS:
                y = y * lax.rsqrt(jnp.sum(y * y, axis=-1, keepdims=True) + EPS)
            out = (gq_out, gk_out, gv_out)[cb // DN_HEADS]
            c1 = (cb % DN_HEADS) * LANES
            out[r0:r0 + GDN_CHUNK, c1:c1 + LANES] = y.astype(BF16)

    qkv = _dot(h, wa_ref[...])
    lo = lax.broadcasted_iota(jnp.int32, (tm, LANES), 1) < DA_QK_DIM

    def qk_norm(t, w):
        sq = t * t
        s_lo = jnp.sum(jnp.where(lo, sq, 0.0), axis=-1, keepdims=True)
        s_hi = jnp.sum(jnp.where(lo, 0.0, sq), axis=-1, keepdims=True)
        ms_ = jnp.where(lo, s_lo, s_hi) * (1.0 / DA_QK_DIM)
        return t * lax.rsqrt(ms_ + EPS) * w

    for hd in range(DA_HEADS):
        c0 = hd * LANES
        qh = qk_norm(qkv[:, c0:c0 + LANES], qnw_ref[...]) * (DA_QK_DIM ** -0.5 * LOG2E)
        kh = qk_norm(qkv[:, HEAD_COLS + c0:HEAD_COLS + c0 + LANES], knw_ref[...])
        q_out[:, c0:c0 + LANES] = qh.astype(BF16)
        k_out[:, c0:c0 + LANES] = kh.astype(BF16)
    v_out[...] = qkv[:, 2 * HEAD_COLS:].astype(BF16)

    z_out[...] = _dot(h, wz_ref[...])
    ab_out[...] = _dot(h, wab_ref[...])


def _inproj(x2, n1, wa, wdn, wz, wab, qnw, knw, cw, S, tm):
    T, D = x2.shape
    W = wdn.shape[1]
    const = lambda i: (0, 0)
    row = lambda i: (i, 0)
    slab = pl.BlockSpec((tm, HEAD_COLS), row)
    slab_bf16 = jax.ShapeDtypeStruct((T, HEAD_COLS), BF16)
    return pl.pallas_call(
        functools.partial(_inproj_body, tiles_per_seq=S // tm),
        grid=(T // tm,),
        in_specs=[
            pl.BlockSpec((tm, D), row),
            pl.BlockSpec((1, D), const),
            pl.BlockSpec(wa.shape, const),
            pl.BlockSpec(wdn.shape, const),
            pl.BlockSpec(wz.shape, const),
            pl.BlockSpec(wab.shape, const),
            pl.BlockSpec((1, LANES), const),
            pl.BlockSpec((1, LANES), const),
            pl.BlockSpec((CONV_WIDTH, W), const),
        ],
        out_specs=[slab, slab, slab, slab, slab, slab, slab, pl.BlockSpec((tm, LANES), row)],
        out_shape=[slab_bf16, slab_bf16, slab_bf16, slab_bf16, slab_bf16, slab_bf16,
                   jax.ShapeDtypeStruct((T, HEAD_COLS), F32),
                   jax.ShapeDtypeStruct((T, LANES), F32)],
        scratch_shapes=[
            pltpu.VMEM((8, W), F32),
            pltpu.VMEM((tm + 8, W), F32),
        ],
        compiler_params=pltpu.CompilerParams(
            dimension_semantics=("arbitrary",), vmem_limit_bytes=VMEM_LIMIT),
        name="inproj",
    )(x2, n1, wa, wdn, wz, wab, qnw, knw, cw)


def _stack_maps(q_ref, qz_ref, tq):
    q = q_ref[...]
    lane = lax.broadcasted_iota(jnp.int32, (tq, LANES), 1)
    zero = jnp.zeros_like(q)
    qz_ref[0:tq, :] = jnp.where(lane < DA_QK_DIM, q, zero)
    qz_ref[tq:2 * tq, :] = jnp.where(lane < DA_QK_DIM, zero, q)


def _causal_mask(tq):
    rows = lax.broadcasted_iota(jnp.int32, (2 * tq, tq), 0)
    cols = lax.broadcasted_iota(jnp.int32, (2 * tq, tq), 1)
    return jnp.where(rows >= tq, rows - tq, rows) >= cols


def _attn_finish(acc, l, lq1_ref, lk1_ref, lq2_ref, lk2_ref, onw_ref, o_ref, tq, lam_init):
    lam = (jnp.exp(jnp.sum(lq1_ref[...] * lk1_ref[...], axis=-1, keepdims=True))
           - jnp.exp(jnp.sum(lq2_ref[...] * lk2_ref[...], axis=-1, keepdims=True)) + lam_init)
    o = acc[:tq] / l[:tq] - lam * (acc[tq:] / l[tq:])
    ms = jnp.mean(o * o, axis=-1, keepdims=True)
    o = o * lax.rsqrt(ms + EPS) * onw_ref[...] * (1.0 - lam_init)
    o_ref[...] = o.astype(BF16)


def _attn_shift_body(shift_ref, q_ref, k_ref, v_ref, lq1_ref, lk1_ref, lq2_ref, lk2_ref, onw_ref, o_ref,
                     qz_ref, p_ref, l_ref, acc_ref, *, tq, lam_init, unroll):
    i = pl.program_id(2)
    _stack_maps(q_ref, qz_ref, tq)
    shift = shift_ref[0]

    def probs(j, diagonal):
        r0 = pl.multiple_of(j * tq, tq)
        s = _dot_nt(qz_ref[...], k_ref[pl.ds(r0, tq), :]) - shift
        if diagonal:
            s = jnp.where(_causal_mask(tq), s, NEG_BIG)
        p = jnp.exp2(s)
        psum = p[:, 0:LANES]
        for c0 in range(LANES, tq, LANES):
            psum = psum + p[:, c0:c0 + LANES]
        return p.astype(BF16), psum

    p, psum = probs(i, True)
    p_ref[...] = p
    l_ref[...] = psum
    acc_ref[...] = jnp.zeros_like(acc_ref)

    def pv_prev(j):
        r0 = pl.multiple_of(jnp.where(j == 0, i, j - 1) * tq, tq)
        return _dot(p_ref[...], v_ref[pl.ds(r0, tq), :])

    def steps(j0, n):
        acc = pv_prev(j0)
        psum = None
        for u in range(n):
            p, ps = probs(j0 + u, False)
            psum = ps if psum is None else psum + ps
            if u + 1 < n:
                acc = acc + _dot(p, v_ref[pl.ds(pl.multiple_of((j0 + u) * tq, tq), tq), :])
            else:
                p_ref[...] = p
        acc_ref[...] += acc
        l_ref[...] += psum

    start = 0
    n = unroll
    while n >= 1:
        trips = (i - start) // n

        def group(t, _, start=start, n=n):
            steps(start + t * n, n)
            return 0

        lax.fori_loop(0, trips, group, 0)
        start = start + trips * n
        n //= 2
    acc = acc_ref[...] + pv_prev(i)
    l = jnp.sum(l_ref[...], axis=-1, keepdims=True)
    _attn_finish(acc, l, lq1_ref, lk1_ref, lq2_ref, lk2_ref, onw_ref, o_ref, tq, lam_init)


def _attn_online_body(shift_ref, q_ref, k_ref, v_ref, lq1_ref, lk1_ref, lq2_ref, lk2_ref, onw_ref, o_ref,
                      qz_ref, *, tq, lam_init):
    del shift_ref
    i = pl.program_id(2)
    _stack_maps(q_ref, qz_ref, tq)

    def step(j, carry, diagonal):
        m, l, acc = carry
        r0 = pl.multiple_of(j * tq, tq)
        s = _dot_nt(qz_ref[...], k_ref[pl.ds(r0, tq), :])
        if diagonal:
            s = jnp.where(_causal_mask(tq), s, NEG_BIG)
        m_new = jnp.maximum(m, jnp.max(s, axis=-1, keepdims=True))
        p = jnp.exp2(s - m_new)
        alpha = jnp.exp2(m - m_new)
        l = alpha * l + jnp.sum(p, axis=-1, keepdims=True)
        acc = alpha * acc + _dot(p.astype(BF16), v_ref[pl.ds(r0, tq), :])
        return m_new, l, acc

    init = (jnp.full((2 * tq, 1), NEG_BIG, F32), jnp.zeros((2 * tq, 1), F32),
            jnp.zeros((2 * tq, LANES), F32))
    carry = lax.fori_loop(0, i, lambda j, c: step(j, c, False), init)
    _, l, acc = step(i, carry, True)
    _attn_finish(acc, l, lq1_ref, lk1_ref, lq2_ref, lk2_ref, onw_ref, o_ref, tq, lam_init)


def _attention(shift, q, k, v, lq1, lk1, lq2, lk2, onw, B, S, tq, lam_init, online):
    nq = S // tq
    vec = pl.BlockSpec((1, DA_QK_DIM), lambda b, h, i: (0, 0))
    if online:
        body = functools.partial(_attn_online_body, tq=tq, lam_init=lam_init)
        scratch = [pltpu.VMEM((2 * tq, LANES), BF16)]
    else:
        body = functools.partial(_attn_shift_body, tq=tq, lam_init=lam_init, unroll=4)
        scratch = [pltpu.VMEM((2 * tq, LANES), BF16),
                   pltpu.VMEM((2 * tq, tq), BF16),
                   pltpu.VMEM((2 * tq, LANES), F32),
                   pltpu.VMEM((2 * tq, LANES), F32)]
    return pl.pallas_call(
        body,
        grid=(B, DA_HEADS, nq),
        in_specs=[
            pl.BlockSpec(memory_space=pltpu.SMEM),
            pl.BlockSpec((tq, LANES), lambda b, h, i: (b * nq + i, h)),
            pl.BlockSpec((S, LANES), lambda b, h, i: (b, h)),
            pl.BlockSpec((S, LANES), lambda b, h, i: (b, h)),
            vec, vec, vec, vec,
            pl.BlockSpec((1, LANES), lambda b, h, i: (0, 0)),
        ],
        out_specs=pl.BlockSpec((tq, LANES), lambda b, h, i: (b * nq + i, h)),
        out_shape=jax.ShapeDtypeStruct((B * S, HEAD_COLS), BF16),
        scratch_shapes=scratch,
        compiler_params=pltpu.CompilerParams(
            dimension_semantics=("arbitrary", "arbitrary", "arbitrary"), vmem_limit_bytes=VMEM_LIMIT),
        name="diff_attention_online" if online else "diff_attention",
    )(shift, q, k, v, lq1, lk1, lq2, lk2, onw)


def _merge_masks(row, col, n):
    masks = []
    b = 2
    while b < n:
        masks.append(((row // b) == (col // b) + 1) & ((row // (2 * b)) == (col // (2 * b))))
        b *= 2
    return masks


def _gdn_body(q_ref, k_ref, v_ref, z_ref, ab_ref, alog_ref, dtb_ref, onw_ref, o_ref,
              state_ref, l_ref, t_ref, rhs_ref, u_ref, wq_ref, qk_ref, kdt_ref, egl_ref,
              *, tb, group):
    C = GDN_CHUNK
    nc = tb // C
    blk = pl.program_id(1)

    @pl.when(blk == 0)
    def _():
        state_ref[...] = jnp.zeros_like(state_ref)

    ab = ab_ref[...]
    sp_in = ab + dtb_ref[...]
    softplus = jnp.maximum(sp_in, 0.0) + jnp.log(1.0 + jnp.exp(-jnp.abs(sp_in)))
    gtok = -jnp.exp(alog_ref[...]) * softplus
    gtok = jnp.where(lax.broadcasted_iota(jnp.int32, (tb, LANES), 1) < DN_HEADS, gtok, 0.0)
    beta_all = 1.0 / (1.0 + jnp.exp(-ab))

    row = lax.broadcasted_iota(jnp.int32, (C, C), 0)
    col = lax.broadcasted_iota(jnp.int32, (C, C), 1)
    lower = row >= col
    strict = row > col
    eye = (row == col).astype(F32)
    first_level = (row == col + 1) & (row % 2 == 1)
    masks = _merge_masks(row, col, C)
    tri = lower.astype(BF16)

    for c in range(nc):
        r0 = c * C
        g1, g2, g3 = _split3(gtok[r0:r0 + C])
        gc = _dot(tri, g1) + _dot(tri, g2) + _dot(tri, g3)
        gct = gc.T
        for h in range(DN_HEADS):
            un = c * DN_HEADS + h
            c0 = h * DN_DIM
            qh = q_ref[r0:r0 + C, c0:c0 + DN_DIM].astype(F32)
            kh_bf = k_ref[r0:r0 + C, c0:c0 + DN_DIM]
            kh = kh_bf.astype(F32)
            vh = v_ref[r0:r0 + C, c0:c0 + DN_DIM].astype(F32)
            gcol = gc[:, h:h + 1]
            grow = gct[h:h + 1, :]
            beta = beta_all[r0:r0 + C, DN_HEADS + h:DN_HEADS + h + 1]
            decay = jnp.where(lower, jnp.exp(jnp.where(lower, gcol - grow, 0.0)), 0.0)
            kb = kh * beta
            a = _dot_nt(jnp.concatenate([kb, qh], axis=0).astype(BF16), kh_bf)
            Lm = jnp.where(strict, a[:C] * decay, 0.0)
            l_ref[un] = Lm.astype(BF16)
            t_ref[un] = eye - jnp.where(first_level, Lm, 0.0)
            qk_ref[un] = jnp.where(lower, a[C:] * decay, 0.0).astype(BF16)
            eg = jnp.exp(gcol)
            rhs_ref[un] = jnp.concatenate([vh * beta, kb * eg], axis=1).astype(BF16)
            wq_ref[un, C:2 * C, :] = (qh * eg).astype(BF16)
            g_last = gcol[C - 1:C, :]
            kdt_ref[un] = (kh * jnp.exp(g_last - gcol)).T.astype(BF16)
            egl_ref[un] = jnp.broadcast_to(jnp.exp(g_last), (8, LANES))

    nu = nc * DN_HEADS
    for mask in masks:
        for g0 in range(0, nu, group):
            units = list(range(g0, min(g0 + group, nu)))
            tbs = [t_ref[un].astype(BF16) for un in units]
            ps = [_dot(jnp.where(mask, l_ref[un], jnp.zeros((), BF16)), t).astype(BF16)
                  for un, t in zip(units, tbs)]
            for un, t, p in zip(units, tbs, ps):
                t_ref[un] = t_ref[un] - _dot(t, p)
    for un in range(nu):
        uw = _dot(t_ref[un].astype(BF16), rhs_ref[un])
        u_ref[un] = uw[:, :DN_DIM]
        wq_ref[un, 0:C, :] = uw[:, DN_DIM:].astype(BF16)

    for c in range(nc):
        r0 = c * C
        for h in range(DN_HEADS):
            un = c * DN_HEADS + h
            c0 = h * DN_DIM
            state = state_ref[h]
            ws = _dot(wq_ref[un], state.astype(BF16))
            v_new = (u_ref[un] - ws[:C]).astype(BF16)
            o = ws[C:] + _dot(qk_ref[un], v_new)
            state_ref[h] = state * egl_ref[un][0:1, :] + _dot(kdt_ref[un], v_new)
            zh = z_ref[r0:r0 + C, c0:c0 + DN_DIM]
            o = o * lax.rsqrt(jnp.mean(o * o, axis=-1, keepdims=True) + EPS) * onw_ref[...]
            o = o * _silu(zh)
            o_ref[r0:r0 + C, c0:c0 + DN_DIM] = o.astype(BF16)


def _gdn(gq, gk, gv, z, ab, alog, dtb, onw, B, S, tb):
    nb = S // tb
    C = GDN_CHUNK
    nu = (tb // C) * DN_HEADS
    const = lambda b, i: (0, 0)
    row = lambda b, i: (b * nb + i, 0)
    return pl.pallas_call(
        functools.partial(_gdn_body, tb=tb, group=8),
        grid=(B, nb),
        in_specs=[
            pl.BlockSpec((tb, HEAD_COLS), row),
            pl.BlockSpec((tb, HEAD_COLS), row),
            pl.BlockSpec((tb, HEAD_COLS), row),
            pl.BlockSpec((tb, HEAD_COLS), row),
            pl.BlockSpec((tb, LANES), row),
            pl.BlockSpec((1, LANES), const),
            pl.BlockSpec((1, LANES), const),
            pl.BlockSpec((1, LANES), const),
        ],
        out_specs=pl.BlockSpec((tb, HEAD_COLS), row),
        out_shape=jax.ShapeDtypeStruct((B * S, HEAD_COLS), BF16),
        scratch_shapes=[
            pltpu.VMEM((DN_HEADS, DN_DIM, DN_DIM), F32),
            pltpu.VMEM((nu, C, C), BF16),
            pltpu.VMEM((nu, C, C), F32),
            pltpu.VMEM((nu, C, 2 * DN_DIM), BF16),
            pltpu.VMEM((nu, C, DN_DIM), F32),
            pltpu.VMEM((nu, 2 * C, DN_DIM), BF16),
            pltpu.VMEM((nu, C, C), BF16),
            pltpu.VMEM((nu, DN_DIM, C), BF16),
            pltpu.VMEM((nu, 8, LANES), F32),
        ],
        compiler_params=pltpu.CompilerParams(
            dimension_semantics=("arbitrary", "arbitrary"), vmem_limit_bytes=VMEM_LIMIT),
        name="gated_deltanet",
    )(gq, gk, gv, z, ab, alog, dtb, onw)


def _mlp_body(x_ref, ma_ref, mb_ref, woa_ref, wob_ref, n2_ref, wup_ref, wdn_ref, o_ref, *, ff_chunk):
    x1 = x_ref[...] + _dot(ma_ref[...], woa_ref[...]) + _dot(mb_ref[...], wob_ref[...])
    ms = jnp.mean(x1 * x1, axis=-1, keepdims=True)
    h = (x1 * lax.rsqrt(ms + EPS) * n2_ref[...]).astype(BF16)
    mlp = None
    for c0 in range(0, wup_ref.shape[1], ff_chunk):
        up = jnp.maximum(_dot(h, wup_ref[:, c0:c0 + ff_chunk]), 0.0)
        down = _dot((up * up).astype(BF16), wdn_ref[c0:c0 + ff_chunk, :])
        mlp = down if mlp is None else mlp + down
    o_ref[...] = x1 + mlp


def _mlp(x2, mix_a, mix_b, woa, wob, n2, wup, wdn, tm, ff_chunk):
    T, D = x2.shape
    const = lambda i: (0, 0)
    row = lambda i: (i, 0)
    return pl.pallas_call(
        functools.partial(_mlp_body, ff_chunk=ff_chunk),
        grid=(T // tm,),
        in_specs=[
            pl.BlockSpec((tm, D), row),
            pl.BlockSpec((tm, HEAD_COLS), row),
            pl.BlockSpec((tm, HEAD_COLS), row),
            pl.BlockSpec(woa.shape, const),
            pl.BlockSpec(wob.shape, const),
            pl.BlockSpec((1, D), const),
            pl.BlockSpec(wup.shape, const),
            pl.BlockSpec(wdn.shape, const),
        ],
        out_specs=pl.BlockSpec((tm, D), row),
        out_shape=jax.ShapeDtypeStruct((T, D), F32),
        compiler_params=pltpu.CompilerParams(
            dimension_semantics=("arbitrary",), vmem_limit_bytes=VMEM_LIMIT),
        name="outproj_mlp",
    )(x2, mix_a, mix_b, woa, wob, n2, wup, wdn)


def _layer(x2, B, S, l, norm1_w, w_in, lambda_q1, lambda_k1, lambda_q2, lambda_k2, q_norm_w, k_norm_w,
           da_out_norm_w, conv_w, A_log, dt_bias, dn_out_norm_w, w_out, norm2_w, w_up, w_down,
           tm, tq, tb, ff_chunk):
    D = x2.shape[1]
    qkv_a = 3 * HEAD_COLS
    dn_cols = 3 * HEAD_COLS
    s4 = qkv_a + dn_cols
    s5 = s4 + HEAD_COLS
    w = w_in[l].astype(BF16)
    wa, wdn, wz = w[:, :qkv_a], w[:, qkv_a:s4], w[:, s4:s5]
    wab = jnp.pad(w[:, s5:], ((0, 0), (0, LANES - 2 * DN_HEADS)))
    tile2 = lambda v: jnp.concatenate([v, v]).reshape(1, LANES).astype(F32)
    pad_lanes = lambda v: jnp.pad(v.astype(F32), (0, LANES - v.shape[0])).reshape(1, LANES)

    q, k, v, gq, gk, gv, z, ab = _inproj(
        x2, norm1_w[l].reshape(1, D).astype(F32), wa, wdn, wz, wab,
        tile2(q_norm_w[l]), tile2(k_norm_w[l]), conv_w[l].astype(F32), S, tm)

    lam_init = 0.8 - 0.6 * math.exp(-0.3 * l)
    vec = lambda p: p[l].reshape(1, DA_QK_DIM).astype(F32)
    bound = 8.0 * jnp.max(jnp.abs(q_norm_w[l].astype(F32))) * jnp.max(jnp.abs(k_norm_w[l].astype(F32)))
    attn_args = ((bound * LOG2E).reshape(1), q, k, v, vec(lambda_q1), vec(lambda_k1), vec(lambda_q2),
                 vec(lambda_k2), da_out_norm_w[l].reshape(1, LANES).astype(F32))
    mix_a = lax.cond(bound <= MAX_FIXED_SHIFT,
                     lambda a: _attention(*a, B, S, tq, lam_init, online=False),
                     lambda a: _attention(*a, B, S, tq, lam_init, online=True), attn_args)

    mix_b = _gdn(gq, gk, gv, z, ab, pad_lanes(A_log[l]), pad_lanes(dt_bias[l]),
                 dn_out_norm_w[l].reshape(1, LANES).astype(F32), B, S, tb)

    wo = w_out[l].astype(BF16)
    return _mlp(x2, mix_a, mix_b, wo[:HEAD_COLS], wo[HEAD_COLS:], norm2_w[l].reshape(1, D).astype(F32),
                w_up[l].astype(BF16), w_down[l].astype(BF16), tm, ff_chunk)


def kernel(x, norm1_w, w_in, lambda_q1, lambda_k1, lambda_q2, lambda_k2, q_norm_w, k_norm_w, da_out_norm_w,
           conv_w, A_log, dt_bias, dn_out_norm_w, w_out, norm2_w, w_up, w_down):
    B, S, D = x.shape
    x2 = x.reshape(B * S, D)
    tm = min(512, S)
    tq = min(512, S)
    tb = min(512, S)
    for l in range(w_in.shape[0]):
        x2 = _layer(x2, B, S, l, norm1_w, w_in, lambda_q1, lambda_k1, lambda_q2, lambda_k2, q_norm_w,
                    k_norm_w, da_out_norm_w, conv_w, A_log, dt_bias, dn_out_norm_w, w_out, norm2_w,
                    w_up, w_down, tm, tq, tb, 1024)
    return x2.reshape(B, S, D)
```

```python
import functools
import math

import jax
import jax.numpy as jnp
from jax import lax
from jax.experimental import pallas as pl
from jax.experimental.pallas import tpu as pltpu

F32 = jnp.float32
BF16 = jnp.bfloat16
EPS = 1e-6
LANES = 128
NEG_BIG = -1e30
LOG2E = math.log2(math.e)
MAX_FIXED_SHIFT = 40.0

DA_HEADS = 4
DA_QK_DIM = 64
DN_HEADS = 4
DN_DIM = 128
CONV_WIDTH = 4
HEAD_COLS = 512
GDN_CHUNK = 128
VMEM_LIMIT = 56 * 1024 * 1024


def _dot(a, b):
    return jnp.dot(a, b, preferred_element_type=F32)


def _dot_nt(a, b):
    return lax.dot_general(a, b, (((1,), (1,)), ((), ())), preferred_element_type=F32)


def _dot_tn(a, b):
    return lax.dot_general(a, b, (((0,), (0,)), ((), ())), preferred_element_type=F32)


def _silu(x):
    h = 0.5 * x
    return h + h * jnp.tanh(h)


def _split3(x):
    x1 = x.astype(BF16)
    r1 = x - x1.astype(F32)
    x2 = r1.astype(BF16)
    x3 = (r1 - x2.astype(F32)).astype(BF16)
    return x1, x2, x3


def _inproj_body(x_ref, n1_ref, wa_ref, wdn_ref, wz_ref, wab_ref, qnw_ref, knw_ref, cw_ref,
                 q_out, k_out, v_out, gq_out, gk_out, gv_out, z_out, ab_out, tail_ref, ext_ref,
                 *, tiles_per_seq):
    tm = x_ref.shape[0]
    x = x_ref[...]
    ms = jnp.mean(x * x, axis=-1, keepdims=True)
    h = (x * lax.rsqrt(ms + EPS) * n1_ref[...]).astype(BF16)

    @pl.when(pl.program_id(0) % tiles_per_seq == 0)
    def _():
        tail_ref[...] = jnp.zeros_like(tail_ref)

    dn = _dot(h, wdn_ref[...])
    ext_ref[0:8, :] = tail_ref[...]
    ext_ref[8:8 + tm, :] = dn
    tail_ref[...] = dn[tm - 8:tm, :]
    for r0 in range(0, tm, GDN_CHUNK):
        for cb in range(3 * DN_HEADS):
            c0 = cb * LANES
            a = ext_ref[r0:r0 + GDN_CHUNK + 8, c0:c0 + LANES]
            y = cw_ref[CONV_WIDTH - 1:CONV_WIDTH, c0:c0 + LANES] * a[8:]
            for w in range(CONV_WIDTH - 1):
                shifted = pltpu.roll(a, CONV_WIDTH - 1 - w, axis=0)[8:]
                y = y + cw_ref[w:w + 1, c0:c0 + LANES] * shifted
            y = _silu(y)
            if cb < DN_HEADS:
                y = y * (lax.rsqrt(jnp.sum(y * y, axis=-1, keepdims=True) + EPS) * (DN_DIM ** -0.5))
            elif cb < 2 * DN_HEADS:
                y = y * lax.rsqrt(jnp.sum(y * y, axis=-1, keepdims=True) + EPS)
            out = (gq_out, gk_out, gv_out)[cb // DN_HEADS]
            c1 = (cb % DN_HEADS) * LANES
            out[r0:r0 + GDN_CHUNK, c1:c1 + LANES] = y.astype(BF16)

    qkv = _dot(h, wa_ref[...])
    lo = lax.broadcasted_iota(jnp.int32, (tm, LANES), 1) < DA_QK_DIM

    def qk_norm(t, w):
        sq = t * t
        s_lo = jnp.sum(jnp.where(lo, sq, 0.0), axis=-1, keepdims=True)
        s_hi = jnp.sum(jnp.where(lo, 0.0, sq), axis=-1, keepdims=True)
        ms_ = jnp.where(lo, s_lo, s_hi) * (1.0 / DA_QK_DIM)
        return t * lax.rsqrt(ms_ + EPS) * w

    for hd in range(DA_HEADS):
        c0 = hd * LANES
        qh = qk_norm(qkv[:, c0:c0 + LANES], qnw_ref[...]) * (DA_QK_DIM ** -0.5 * LOG2E)
        kh = qk_norm(qkv[:, HEAD_COLS + c0:HEAD_COLS + c0 + LANES], knw_ref[...])
        q_out[:, c0:c0 + LANES] = qh.astype(BF16)
        k_out[:, c0:c0 + LANES] = kh.astype(BF16)
    v_out[...] = qkv[:, 2 * HEAD_COLS:].astype(BF16)

    z_out[...] = _dot(h, wz_ref[...])
    ab_out[...] = _dot(h, wab_ref[...])


def _inproj(x2, n1, wa, wdn, wz, wab, qnw, knw, cw, S, tm):
    T, D = x2.shape
    W = wdn.shape[1]
    const = lambda i: (0, 0)
    row = lambda i: (i, 0)
    slab = pl.BlockSpec((tm, HEAD_COLS), row)
    slab_bf16 = jax.ShapeDtypeStruct((T, HEAD_COLS), BF16)
    return pl.pallas_call(
        functools.partial(_inproj_body, tiles_per_seq=S // tm),
        grid=(T // tm,),
        in_specs=[
            pl.BlockSpec((tm, D), row),
            pl.BlockSpec((1, D), const),
            pl.BlockSpec(wa.shape, const),
            pl.BlockSpec(wdn.shape, const),
            pl.BlockSpec(wz.shape, const),
            pl.BlockSpec(wab.shape, const),
            pl.BlockSpec((1, LANES), const),
            pl.BlockSpec((1, LANES), const),
            pl.BlockSpec((CONV_WIDTH, W), const),
        ],
        out_specs=[slab, slab, slab, slab, slab, slab, slab, pl.BlockSpec((tm, LANES), row)],
        out_shape=[slab_bf16, slab_bf16, slab_bf16, slab_bf16, slab_bf16, slab_bf16,
                   jax.ShapeDtypeStruct((T, HEAD_COLS), F32),
                   jax.ShapeDtypeStruct((T, LANES), F32)],
        scratch_shapes=[
            pltpu.VMEM((8, W), F32),
            pltpu.VMEM((tm + 8, W), F32),
        ],
        compiler_params=pltpu.CompilerParams(
            dimension_semantics=("arbitrary",), vmem_limit_bytes=VMEM_LIMIT),
        name="inproj",
    )(x2, n1, wa, wdn, wz, wab, qnw, knw, cw)


def _stack_maps(q_ref, qz_ref, tq):
    q = q_ref[...]
    lane = lax.broadcasted_iota(jnp.int32, (tq, LANES), 1)
    zero = jnp.zeros_like(q)
    qz_ref[0:tq, :] = jnp.where(lane < DA_QK_DIM, q, zero)
    qz_ref[tq:2 * tq, :] = jnp.where(lane < DA_QK_DIM, zero, q)


def _causal_mask(tq):
    rows = lax.broadcasted_iota(jnp.int32, (2 * tq, tq), 0)
    cols = lax.broadcasted_iota(jnp.int32, (2 * tq, tq), 1)
    return jnp.where(rows >= tq, rows - tq, rows) >= cols


def _attn_finish(acc, l, lq1_ref, lk1_ref, lq2_ref, lk2_ref, onw_ref, o_ref, tq, lam_init):
    lam = (jnp.exp(jnp.sum(lq1_ref[...] * lk1_ref[...], axis=-1, keepdims=True))
           - jnp.exp(jnp.sum(lq2_ref[...] * lk2_ref[...], axis=-1, keepdims=True)) + lam_init)
    o = acc[:tq] / l[:tq] - lam * (acc[tq:] / l[tq:])
    ms = jnp.mean(o * o, axis=-1, keepdims=True)
    o = o * lax.rsqrt(ms + EPS) * onw_ref[...] * (1.0 - lam_init)
    o_ref[...] = o.astype(BF16)


def _attn_shift_body(shift_ref, q_ref, k_ref, v_ref, lq1_ref, lk1_ref, lq2_ref, lk2_ref, onw_ref, o_ref,
                     qz_ref, p_ref, l_ref, acc_ref, *, tq, lam_init, unroll):
    i = pl.program_id(2)
    _stack_maps(q_ref, qz_ref, tq)
    shift = shift_ref[0]

    def probs(j, diagonal):
        r0 = pl.multiple_of(j * tq, tq)
        s = _dot_nt(qz_ref[...], k_ref[pl.ds(r0, tq), :]) - shift
        if diagonal:
            s = jnp.where(_causal_mask(tq), s, NEG_BIG)
        p = jnp.exp2(s)
        psum = p[:, 0:LANES]
        for c0 in range(LANES, tq, LANES):
            psum = psum + p[:, c0:c0 + LANES]
        return p.astype(BF16), psum

    p, psum = probs(i, True)
    p_ref[...] = p
    l_ref[...] = psum
    acc_ref[...] = jnp.zeros_like(acc_ref)

    def pv_prev(j):
        r0 = pl.multiple_of(jnp.where(j == 0, i, j - 1) * tq, tq)
        return _dot(p_ref[...], v_ref[pl.ds(r0, tq), :])

    def steps(j0, n):
        acc = pv_prev(j0)
        psum = None
        for u in range(n):
            p, ps = probs(j0 + u, False)
            psum = ps if psum is None else psum + ps
            if u + 1 < n:
                acc = acc + _dot(p, v_ref[pl.ds(pl.multiple_of((j0 + u) * tq, tq), tq), :])
            else:
                p_ref[...] = p
        acc_ref[...] += acc
        l_ref[...] += psum

    start = 0
    n = unroll
    while n >= 1:
        trips = (i - start) // n

        def group(t, _, start=start, n=n):
            steps(start + t * n, n)
            return 0

        lax.fori_loop(0, trips, group, 0)
        start = start + trips * n
        n //= 2
    acc = acc_ref[...] + pv_prev(i)
    l = jnp.sum(l_ref[...], axis=-1, keepdims=True)
    _attn_finish(acc, l, lq1_ref, lk1_ref, lq2_ref, lk2_ref, onw_ref, o_ref, tq, lam_init)


def _attn_online_body(shift_ref, q_ref, k_ref, v_ref, lq1_ref, lk1_ref, lq2_ref, lk2_ref, onw_ref, o_ref,
                      qz_ref, *, tq, lam_init):
    del shift_ref
    i = pl.program_id(2)
    _stack_maps(q_ref, qz_ref, tq)

    def step(j, carry, diagonal):
        m, l, acc = carry
        r0 = pl.multiple_of(j * tq, tq)
        s = _dot_nt(qz_ref[...], k_ref[pl.ds(r0, tq), :])
        if diagonal:
            s = jnp.where(_causal_mask(tq), s, NEG_BIG)
        m_new = jnp.maximum(m, jnp.max(s, axis=-1, keepdims=True))
        p = jnp.exp2(s - m_new)
        alpha = jnp.exp2(m - m_new)
        l = alpha * l + jnp.sum(p, axis=-1, keepdims=True)
        acc = alpha * acc + _dot(p.astype(BF16), v_ref[pl.ds(r0, tq), :])
        return m_new, l, acc

    init = (jnp.full((2 * tq, 1), NEG_BIG, F32), jnp.zeros((2 * tq, 1), F32),
            jnp.zeros((2 * tq, LANES), F32))
    carry = lax.fori_loop(0, i, lambda j, c: step(j, c, False), init)
    _, l, acc = step(i, carry, True)
    _attn_finish(acc, l, lq1_ref, lk1_ref, lq2_ref, lk2_ref, onw_ref, o_ref, tq, lam_init)


def _attention(shift, q, k, v, lq1, lk1, lq2, lk2, onw, B, S, tq, lam_init, online):
    nq = S // tq
    vec = pl.BlockSpec((1, DA_QK_DIM), lambda b, h, i: (0, 0))
    if online:
        body = functools.partial(_attn_online_body, tq=tq, lam_init=lam_init)
        scratch = [pltpu.VMEM((2 * tq, LANES), BF16)]
    else:
        body = functools.partial(_attn_shift_body, tq=tq, lam_init=lam_init, unroll=4)
        scratch = [pltpu.VMEM((2 * tq, LANES), BF16),
                   pltpu.VMEM((2 * tq, tq), BF16),
                   pltpu.VMEM((2 * tq, LANES), F32),
                   pltpu.VMEM((2 * tq, LANES), F32)]
    return pl.pallas_call(
        body,
        grid=(B, DA_HEADS, nq),
        in_specs=[
            pl.BlockSpec(memory_space=pltpu.SMEM),
            pl.BlockSpec((tq, LANES), lambda b, h, i: (b * nq + i, h)),
            pl.BlockSpec((S, LANES), lambda b, h, i: (b, h)),
            pl.BlockSpec((S, LANES), lambda b, h, i: (b, h)),
            vec, vec, vec, vec,
            pl.BlockSpec((1, LANES), lambda b, h, i: (0, 0)),
        ],
        out_specs=pl.BlockSpec((tq, LANES), lambda b, h, i: (b * nq + i, h)),
        out_shape=jax.ShapeDtypeStruct((B * S, HEAD_COLS), BF16),
        scratch_shapes=scratch,
        compiler_params=pltpu.CompilerParams(
            dimension_semantics=("arbitrary", "arbitrary", "arbitrary"), vmem_limit_bytes=VMEM_LIMIT),
        name="diff_attention_online" if online else "diff_attention",
    )(shift, q, k, v, lq1, lk1, lq2, lk2, onw)


def _merge_masks(row, col, n):
    masks = []
    b = 2
    while b < n:
        masks.append(((row // b) == (col // b) + 1) & ((row // (2 * b)) == (col // (2 * b))))
        b *= 2
    return masks


def _gdn_body(q_ref, k_ref, v_ref, z_ref, ab_ref, alog_ref, dtb_ref, onw_ref, o_ref,
              state_ref, l_ref, t_ref, rhs_ref, ou_ref, ku_ref, qa_ref, qk_ref, kdt_ref, egl_ref,
              fmask_ref, lmask_ref, *, tb, group):
    C = GDN_CHUNK
    nc = tb // C
    blk = pl.program_id(1)
    n_levels = lmask_ref.shape[0]

    @pl.when(blk == 0)
    def _():
        state_ref[...] = jnp.zeros_like(state_ref)
        row = lax.broadcasted_iota(jnp.int32, (C, C), 0)
        col = lax.broadcasted_iota(jnp.int32, (C, C), 1)
        fmask_ref[0] = (row >= col).astype(F32)
        fmask_ref[1] = (row > col).astype(F32)
        fmask_ref[2] = (row == col).astype(F32)
        fmask_ref[3] = ((row == col + 1) & (row % 2 == 1)).astype(F32)
        for lv, m in enumerate(_merge_masks(row, col, C)):
            lmask_ref[lv] = m.astype(BF16)

    ab = ab_ref[...]
    sp_in = ab + dtb_ref[...]
    softplus = jnp.maximum(sp_in, 0.0) + jnp.log(1.0 + jnp.exp(-jnp.abs(sp_in)))
    gtok = -jnp.exp(alog_ref[...]) * softplus
    gtok = jnp.where(lax.broadcasted_iota(jnp.int32, (tb, LANES), 1) < DN_HEADS, gtok, 0.0)
    beta_all = 1.0 / (1.0 + jnp.exp(-ab))

    tri = fmask_ref[0].astype(BF16)

    for c in range(nc):
        r0 = c * C
        g1, g2, g3 = _split3(gtok[r0:r0 + C])
        gc = _dot(tri, g1) + _dot(tri, g2) + _dot(tri, g3)
        gct = gc.T
        for h in range(DN_HEADS):
            un = c * DN_HEADS + h
            c0 = h * DN_DIM
            qh = q_ref[r0:r0 + C, c0:c0 + DN_DIM].astype(F32)
            kh_bf = k_ref[r0:r0 + C, c0:c0 + DN_DIM]
            kh = kh_bf.astype(F32)
            vh = v_ref[r0:r0 + C, c0:c0 + DN_DIM].astype(F32)
            gcol = gc[:, h:h + 1]
            grow = gct[h:h + 1, :]
            beta = beta_all[r0:r0 + C, DN_HEADS + h:DN_HEADS + h + 1]
            decay = jnp.exp(jnp.minimum(gcol - grow, 0.0))
            kb = kh * beta
            a = _dot_nt(jnp.concatenate([kb, qh], axis=0).astype(BF16), kh_bf)
            Lm = a[:C] * decay * fmask_ref[1]
            l_ref[un] = Lm.astype(BF16)
            t_ref[un] = (fmask_ref[2] - Lm * fmask_ref[3]).astype(BF16)
            qk_ref[un] = (a[C:] * decay * fmask_ref[0]).astype(BF16)
            eg = jnp.exp(gcol)
            rhs_ref[un] = jnp.concatenate([vh * beta, kb * eg], axis=1).astype(BF16)
            qa_ref[un, 0:C, :] = (qh * eg).astype(BF16)
            g_last = gcol[C - 1:C, :]
            kdt_ref[un] = (kh * jnp.exp(g_last - gcol)).T.astype(BF16)
            egl_ref[un] = jnp.broadcast_to(jnp.exp(g_last), (8, LANES))

    nu = nc * DN_HEADS
    for lv in range(n_levels):
        for g0 in range(0, nu, group):
            units = list(range(g0, min(g0 + group, nu)))
            ps = [_dot(l_ref[un] * lmask_ref[lv], t_ref[un]).astype(BF16) for un in units]
            for un, p in zip(units, ps):
                t_ref[un] = t_ref[un] - _dot(t_ref[un], p).astype(BF16)
    for un in range(nu):
        uw = _dot(t_ref[un], rhs_ref[un]).astype(BF16)
        r = _dot(jnp.concatenate([qk_ref[un], kdt_ref[un]], axis=0), uw)
        ou_ref[un] = r[:C, :DN_DIM]
        qa_ref[un, 0:C, :] = (qa_ref[un, 0:C, :].astype(F32) - r[:C, DN_DIM:]).astype(BF16)
        qa_ref[un, C:C + DN_DIM, :] = (-r[C:, DN_DIM:]).astype(BF16)
        ku_ref[un] = r[C:, :DN_DIM]

    for c in range(nc):
        r0 = c * C
        for h in range(DN_HEADS):
            un = c * DN_HEADS + h
            c0 = h * DN_DIM
            state = state_ref[h]
            r = _dot(qa_ref[un], state.astype(BF16))
            o = ou_ref[un] + r[:C]
            state_ref[h] = state * egl_ref[un][0:1, :] + r[C:] + ku_ref[un]
            zh = z_ref[r0:r0 + C, c0:c0 + DN_DIM]
            o = o * lax.rsqrt(jnp.mean(o * o, axis=-1, keepdims=True) + EPS) * onw_ref[...]
            o = o * _silu(zh)
            o_ref[r0:r0 + C, c0:c0 + DN_DIM] = o.astype(BF16)


def _gdn(gq, gk, gv, z, ab, alog, dtb, onw, B, S, tb):
    nb = S // tb
    C = GDN_CHUNK
    nu = (tb // C) * DN_HEADS
    const = lambda b, i: (0, 0)
    row = lambda b, i: (b * nb + i, 0)
    return pl.pallas_call(
        functools.partial(_gdn_body, tb=tb, group=8),
        grid=(B, nb),
        in_specs=[
            pl.BlockSpec((tb, HEAD_COLS), row),
            pl.BlockSpec((tb, HEAD_COLS), row),
            pl.BlockSpec((tb, HEAD_COLS), row),
            pl.BlockSpec((tb, HEAD_COLS), row),
            pl.BlockSpec((tb, LANES), row),
            pl.BlockSpec((1, LANES), const),
            pl.BlockSpec((1, LANES), const),
            pl.BlockSpec((1, LANES), const),
        ],
        out_specs=pl.BlockSpec((tb, HEAD_COLS), row),
        out_shape=jax.ShapeDtypeStruct((B * S, HEAD_COLS), BF16),
        scratch_shapes=[
            pltpu.VMEM((DN_HEADS, DN_DIM, DN_DIM), F32),
            pltpu.VMEM((nu, C, C), BF16),
            pltpu.VMEM((nu, C, C), BF16),
            pltpu.VMEM((nu, C, 2 * DN_DIM), BF16),
            pltpu.VMEM((nu, C, DN_DIM), F32),
            pltpu.VMEM((nu, DN_DIM, DN_DIM), F32),
            pltpu.VMEM((nu, C + DN_DIM, DN_DIM), BF16),
            pltpu.VMEM((nu, C, C), BF16),
            pltpu.VMEM((nu, DN_DIM, C), BF16),
            pltpu.VMEM((nu, 8, LANES), F32),
            pltpu.VMEM((4, C, C), F32),
            pltpu.VMEM((C.bit_length() - 2, C, C), BF16),
        ],
        compiler_params=pltpu.CompilerParams(
            dimension_semantics=("arbitrary", "arbitrary"), vmem_limit_bytes=VMEM_LIMIT),
        name="gated_deltanet",
    )(gq, gk, gv, z, ab, alog, dtb, onw)


def _mlp_body(x_ref, ma_ref, mb_ref, woa_ref, wob_ref, n2_ref, wup_ref, wdn_ref, o_ref, *, ff_chunk):
    x1 = x_ref[...] + _dot(ma_ref[...], woa_ref[...]) + _dot(mb_ref[...], wob_ref[...])
    ms = jnp.mean(x1 * x1, axis=-1, keepdims=True)
    h = (x1 * lax.rsqrt(ms + EPS) * n2_ref[...]).astype(BF16)
    mlp = None
    for c0 in range(0, wup_ref.shape[1], ff_chunk):
        up = jnp.maximum(_dot(h, wup_ref[:, c0:c0 + ff_chunk]), 0.0)
        down = _dot((up * up).astype(BF16), wdn_ref[c0:c0 + ff_chunk, :])
        mlp = down if mlp is None else mlp + down
    o_ref[...] = x1 + mlp


def _mlp(x2, mix_a, mix_b, woa, wob, n2, wup, wdn, tm, ff_chunk):
    T, D = x2.shape
    const = lambda i: (0, 0)
    row = lambda i: (i, 0)
    return pl.pallas_call(
        functools.partial(_mlp_body, ff_chunk=ff_chunk),
        grid=(T // tm,),
        in_specs=[
            pl.BlockSpec((tm, D), row),
            pl.BlockSpec((tm, HEAD_COLS), row),
            pl.BlockSpec((tm, HEAD_COLS), row),
            pl.BlockSpec(woa.shape, const),
            pl.BlockSpec(wob.shape, const),
            pl.BlockSpec((1, D), const),
            pl.BlockSpec(wup.shape, const),
            pl.BlockSpec(wdn.shape, const),
        ],
        out_specs=pl.BlockSpec((tm, D), row),
        out_shape=jax.ShapeDtypeStruct((T, D), F32),
        compiler_params=pltpu.CompilerParams(
            dimension_semantics=("arbitrary",), vmem_limit_bytes=VMEM_LIMIT),
        name="outproj_mlp",
    )(x2, mix_a, mix_b, woa, wob, n2, wup, wdn)


def _layer(x2, B, S, l, norm1_w, w_in, lambda_q1, lambda_k1, lambda_q2, lambda_k2, q_norm_w, k_norm_w,
           da_out_norm_w, conv_w, A_log, dt_bias, dn_out_norm_w, w_out, norm2_w, w_up, w_down,
           tm, tq, tb, ff_chunk):
    D = x2.shape[1]
    qkv_a = 3 * HEAD_COLS
    dn_cols = 3 * HEAD_COLS
    s4 = qkv_a + dn_cols
    s5 = s4 + HEAD_COLS
    w = w_in[l].astype(BF16)
    wa, wdn, wz = w[:, :qkv_a], w[:, qkv_a:s4], w[:, s4:s5]
    wab = jnp.pad(w[:, s5:], ((0, 0), (0, LANES - 2 * DN_HEADS)))
    tile2 = lambda v: jnp.concatenate([v, v]).reshape(1, LANES).astype(F32)
    pad_lanes = lambda v: jnp.pad(v.astype(F32), (0, LANES - v.shape[0])).reshape(1, LANES)

    q, k, v, gq, gk, gv, z, ab = _inproj(
        x2, norm1_w[l].reshape(1, D).astype(F32), wa, wdn, wz, wab,
        tile2(q_norm_w[l]), tile2(k_norm_w[l]), conv_w[l].astype(F32), S, tm)

    lam_init = 0.8 - 0.6 * math.exp(-0.3 * l)
    vec = lambda p: p[l].reshape(1, DA_QK_DIM).astype(F32)
    bound = 8.0 * jnp.max(jnp.abs(q_norm_w[l].astype(F32))) * jnp.max(jnp.abs(k_norm_w[l].astype(F32)))
    attn_args = ((bound * LOG2E).reshape(1), q, k, v, vec(lambda_q1), vec(lambda_k1), vec(lambda_q2),
                 vec(lambda_k2), da_out_norm_w[l].reshape(1, LANES).astype(F32))
    mix_a = lax.cond(bound <= MAX_FIXED_SHIFT,
                     lambda a: _attention(*a, B, S, tq, lam_init, online=False),
                     lambda a: _attention(*a, B, S, tq, lam_init, online=True), attn_args)

    mix_b = _gdn(gq, gk, gv, z, ab, pad_lanes(A_log[l]), pad_lanes(dt_bias[l]),
                 dn_out_norm_w[l].reshape(1, LANES).astype(F32), B, S, tb)

    wo = w_out[l].astype(BF16)
    return _mlp(x2, mix_a, mix_b, wo[:HEAD_COLS], wo[HEAD_COLS:], norm2_w[l].reshape(1, D).astype(F32),
                w_up[l].astype(BF16), w_down[l].astype(BF16), tm, ff_chunk)


def kernel(x, norm1_w, w_in, lambda_q1, lambda_k1, lambda_q2, lambda_k2, q_norm_w, k_norm_w, da_out_norm_w,
           conv_w, A_log, dt_bias, dn_out_norm_w, w_out, norm2_w, w_up, w_down):
    B, S, D = x.shape
    x2 = x.reshape(B * S, D)
    tm = min(512, S)
    tq = min(512, S)
    tb = min(512, S)
    for l in range(w_in.shape[0]):
        x2 = _layer(x2, B, S, l, norm1_w, w_in, lambda_q1, lambda_k1, lambda_q2, lambda_k2, q_norm_w,
                    k_norm_w, da_out_norm_w, conv_w, A_log, dt_bias, dn_out_norm_w, w_out, norm2_w,
                    w_up, w_down, tm, tq, tb, 1024)
    return x2.reshape(B, S, D)
```

```python
import functools
import math

import jax
import jax.numpy as jnp
from jax import lax
from jax.experimental import pallas as pl
from jax.experimental.pallas import tpu as pltpu

F32 = jnp.float32
BF16 = jnp.bfloat16
EPS = 1e-6
LANES = 128
NEG_BIG = -1e30
LOG2E = math.log2(math.e)
MAX_FIXED_SHIFT = 40.0

DA_HEADS = 4
DA_QK_DIM = 64
DN_HEADS = 4
DN_DIM = 128
CONV_WIDTH = 4
HEAD_COLS = 512
GDN_CHUNK = 128
VMEM_LIMIT = 56 * 1024 * 1024


def _dot(a, b):
    return jnp.dot(a, b, preferred_element_type=F32)


def _dot_nt(a, b):
    return lax.dot_general(a, b, (((1,), (1,)), ((), ())), preferred_element_type=F32)


def _dot_tn(a, b):
    return lax.dot_general(a, b, (((0,), (0,)), ((), ())), preferred_element_type=F32)


def _silu(x):
    h = 0.5 * x
    return h + h * jnp.tanh(h)


def _split3(x):
    x1 = x.astype(BF16)
    r1 = x - x1.astype(F32)
    x2 = r1.astype(BF16)
    x3 = (r1 - x2.astype(F32)).astype(BF16)
    return x1, x2, x3


def _inproj_body(x_ref, n1_ref, wa_ref, wdn_ref, wz_ref, wab_ref, qnw_ref, knw_ref, cw_ref,
                 q_out, k_out, v_out, gq_out, gk_out, gv_out, z_out, ab_out, tail_ref, ext_ref,
                 *, tiles_per_seq):
    tm = x_ref.shape[0]
    x = x_ref[...]
    ms = jnp.mean(x * x, axis=-1, keepdims=True)
    h = (x * lax.rsqrt(ms + EPS) * n1_ref[...]).astype(BF16)

    @pl.when(pl.program_id(0) % tiles_per_seq == 0)
    def _():
        tail_ref[...] = jnp.zeros_like(tail_ref)

    dn = _dot(h, wdn_ref[...])
    ext_ref[0:8, :] = tail_ref[...]
    ext_ref[8:8 + tm, :] = dn
    tail_ref[...] = dn[tm - 8:tm, :]
    for r0 in range(0, tm, GDN_CHUNK):
        for cb in range(3 * DN_HEADS):
            c0 = cb * LANES
            a = ext_ref[r0:r0 + GDN_CHUNK + 8, c0:c0 + LANES]
            y = cw_ref[CONV_WIDTH - 1:CONV_WIDTH, c0:c0 + LANES] * a[8:]
            for w in range(CONV_WIDTH - 1):
                shifted = pltpu.roll(a, CONV_WIDTH - 1 - w, axis=0)[8:]
                y = y + cw_ref[w:w + 1, c0:c0 + LANES] * shifted
            y = _silu(y)
            if cb < DN_HEADS:
                y = y * (lax.rsqrt(jnp.sum(y * y, axis=-1, keepdims=True) + EPS) * (DN_DIM ** -0.5))
            elif cb < 2 * DN_HEADS:
                y = y * lax.rsqrt(jnp.sum(y * y, axis=-1, keepdims=True) + EPS)
            out = (gq_out, gk_out, gv_out)[cb // DN_HEADS]
            c1 = (cb % DN_HEADS) * LANES
            out[r0:r0 + GDN_CHUNK, c1:c1 + LANES] = y.astype(BF16)

    qkv = _dot(h, wa_ref[...])
    lo = lax.broadcasted_iota(jnp.int32, (tm, LANES), 1) < DA_QK_DIM

    def qk_norm(t, w):
        sq = t * t
        s_lo = jnp.sum(jnp.where(lo, sq, 0.0), axis=-1, keepdims=True)
        s_hi = jnp.sum(jnp.where(lo, 0.0, sq), axis=-1, keepdims=True)
        ms_ = jnp.where(lo, s_lo, s_hi) * (1.0 / DA_QK_DIM)
        return t * lax.rsqrt(ms_ + EPS) * w

    for hd in range(DA_HEADS):
        c0 = hd * LANES
        qh = qk_norm(qkv[:, c0:c0 + LANES], qnw_ref[...]) * (DA_QK_DIM ** -0.5 * LOG2E)
        kh = qk_norm(qkv[:, HEAD_COLS + c0:HEAD_COLS + c0 + LANES], knw_ref[...])
        q_out[:, c0:c0 + LANES] = qh.astype(BF16)
        k_out[:, c0:c0 + LANES] = kh.astype(BF16)
    v_out[...] = qkv[:, 2 * HEAD_COLS:].astype(BF16)

    z_out[...] = _dot(h, wz_ref[...])
    ab_out[...] = _dot(h, wab_ref[...])


def _inproj(x2, n1, wa, wdn, wz, wab, qnw, knw, cw, S, tm):
    T, D = x2.shape
    W = wdn.shape[1]
    const = lambda i: (0, 0)
    row = lambda i: (i, 0)
    slab = pl.BlockSpec((tm, HEAD_COLS), row)
    slab_bf16 = jax.ShapeDtypeStruct((T, HEAD_COLS), BF16)
    return pl.pallas_call(
        functools.partial(_inproj_body, tiles_per_seq=S // tm),
        grid=(T // tm,),
        in_specs=[
            pl.BlockSpec((tm, D), row),
            pl.BlockSpec((1, D), const),
            pl.BlockSpec(wa.shape, const),
            pl.BlockSpec(wdn.shape, const),
            pl.BlockSpec(wz.shape, const),
            pl.BlockSpec(wab.shape, const),
            pl.BlockSpec((1, LANES), const),
            pl.BlockSpec((1, LANES), const),
            pl.BlockSpec((CONV_WIDTH, W), const),
        ],
        out_specs=[slab, slab, slab, slab, slab, slab, slab, pl.BlockSpec((tm, LANES), row)],
        out_shape=[slab_bf16, slab_bf16, slab_bf16, slab_bf16, slab_bf16, slab_bf16,
                   jax.ShapeDtypeStruct((T, HEAD_COLS), F32),
                   jax.ShapeDtypeStruct((T, LANES), F32)],
        scratch_shapes=[
            pltpu.VMEM((8, W), F32),
            pltpu.VMEM((tm + 8, W), F32),
        ],
        compiler_params=pltpu.CompilerParams(
            dimension_semantics=("arbitrary",), vmem_limit_bytes=VMEM_LIMIT),
        name="inproj",
    )(x2, n1, wa, wdn, wz, wab, qnw, knw, cw)


def _stack_maps(q_ref, qz_ref, tq):
    q = q_ref[...]
    lane = lax.broadcasted_iota(jnp.int32, (tq, LANES), 1)
    zero = jnp.zeros_like(q)
    qz_ref[0:tq, :] = jnp.where(lane < DA_QK_DIM, q, zero)
    qz_ref[tq:2 * tq, :] = jnp.where(lane < DA_QK_DIM, zero, q)


def _causal_mask(tq):
    rows = lax.broadcasted_iota(jnp.int32, (2 * tq, tq), 0)
    cols = lax.broadcasted_iota(jnp.int32, (2 * tq, tq), 1)
    return jnp.where(rows >= tq, rows - tq, rows) >= cols


def _attn_finish(acc, l, lq1_ref, lk1_ref, lq2_ref, lk2_ref, onw_ref, o_ref, tq, lam_init):
    lam = (jnp.exp(jnp.sum(lq1_ref[...] * lk1_ref[...], axis=-1, keepdims=True))
           - jnp.exp(jnp.sum(lq2_ref[...] * lk2_ref[...], axis=-1, keepdims=True)) + lam_init)
    o = acc[:tq] / l[:tq] - lam * (acc[tq:] / l[tq:])
    ms = jnp.mean(o * o, axis=-1, keepdims=True)
    o = o * lax.rsqrt(ms + EPS) * onw_ref[...] * (1.0 - lam_init)
    o_ref[...] = o.astype(BF16)


def _attn_shift_body(shift_ref, q_ref, k_ref, v_ref, lq1_ref, lk1_ref, lq2_ref, lk2_ref, onw_ref, o_ref,
                     qz_ref, p_ref, l_ref, acc_ref, *, tq, lam_init, unroll, heads):
    i = pl.program_id(2)
    shift = shift_ref[0]
    cols = lambda hh: slice(hh * LANES, (hh + 1) * LANES)
    rows = lambda j: pl.ds(pl.multiple_of(j * tq, tq), tq)

    def probs(hh, j, diagonal):
        s = _dot_nt(qz_ref[hh], k_ref[rows(j), cols(hh)]) - shift
        if diagonal:
            s = jnp.where(_causal_mask(tq), s, NEG_BIG)
        p = jnp.exp2(s)
        psum = p[:, 0:LANES]
        for c0 in range(LANES, tq, LANES):
            psum = psum + p[:, c0:c0 + LANES]
        return p.astype(BF16), psum

    for hh in range(heads):
        _stack_maps(q_ref.at[:, cols(hh)], qz_ref.at[hh], tq)
        p, psum = probs(hh, i, True)
        p_ref[hh] = p
        l_ref[hh] = psum
        acc_ref[hh] = jnp.zeros((2 * tq, LANES), F32)

    def pv_prev(hh, j):
        return _dot(p_ref[hh], v_ref[rows(jnp.where(j == 0, i, j - 1)), cols(hh)])

    def steps(j0, n):
        acc = [pv_prev(hh, j0) for hh in range(heads)]
        psum = [None] * heads
        for u in range(n):
            for hh in range(heads):
                p, ps = probs(hh, j0 + u, False)
                psum[hh] = ps if psum[hh] is None else psum[hh] + ps
                if u + 1 < n:
                    acc[hh] = acc[hh] + _dot(p, v_ref[rows(j0 + u), cols(hh)])
                else:
                    p_ref[hh] = p
        for hh in range(heads):
            acc_ref[hh] += acc[hh]
            l_ref[hh] += psum[hh]

    start = 0
    n = unroll
    while n >= 1:
        trips = (i - start) // n

        def group(t, _, start=start, n=n):
            steps(start + t * n, n)
            return 0

        lax.fori_loop(0, trips, group, 0)
        start = start + trips * n
        n //= 2
    for hh in range(heads):
        acc = acc_ref[hh] + pv_prev(hh, i)
        l = jnp.sum(l_ref[hh], axis=-1, keepdims=True)
        _attn_finish(acc, l, lq1_ref, lk1_ref, lq2_ref, lk2_ref, onw_ref, o_ref.at[:, cols(hh)], tq, lam_init)


def _attn_online_body(shift_ref, q_ref, k_ref, v_ref, lq1_ref, lk1_ref, lq2_ref, lk2_ref, onw_ref, o_ref,
                      qz_ref, *, tq, lam_init):
    del shift_ref
    i = pl.program_id(2)
    _stack_maps(q_ref, qz_ref, tq)

    def step(j, carry, diagonal):
        m, l, acc = carry
        r0 = pl.multiple_of(j * tq, tq)
        s = _dot_nt(qz_ref[...], k_ref[pl.ds(r0, tq), :])
        if diagonal:
            s = jnp.where(_causal_mask(tq), s, NEG_BIG)
        m_new = jnp.maximum(m, jnp.max(s, axis=-1, keepdims=True))
        p = jnp.exp2(s - m_new)
        alpha = jnp.exp2(m - m_new)
        l = alpha * l + jnp.sum(p, axis=-1, keepdims=True)
        acc = alpha * acc + _dot(p.astype(BF16), v_ref[pl.ds(r0, tq), :])
        return m_new, l, acc

    init = (jnp.full((2 * tq, 1), NEG_BIG, F32), jnp.zeros((2 * tq, 1), F32),
            jnp.zeros((2 * tq, LANES), F32))
    carry = lax.fori_loop(0, i, lambda j, c: step(j, c, False), init)
    _, l, acc = step(i, carry, True)
    _attn_finish(acc, l, lq1_ref, lk1_ref, lq2_ref, lk2_ref, onw_ref, o_ref, tq, lam_init)


def _attention(shift, q, k, v, lq1, lk1, lq2, lk2, onw, B, S, tq, lam_init, online):
    nq = S // tq
    vec = pl.BlockSpec((1, DA_QK_DIM), lambda b, h, i: (0, 0))
    if online:
        heads = 1
        body = functools.partial(_attn_online_body, tq=tq, lam_init=lam_init)
        scratch = [pltpu.VMEM((2 * tq, LANES), BF16)]
    else:
        heads = 2
        body = functools.partial(_attn_shift_body, tq=tq, lam_init=lam_init, unroll=4, heads=heads)
        scratch = [pltpu.VMEM((heads, 2 * tq, LANES), BF16),
                   pltpu.VMEM((heads, 2 * tq, tq), BF16),
                   pltpu.VMEM((heads, 2 * tq, LANES), F32),
                   pltpu.VMEM((heads, 2 * tq, LANES), F32)]
    width = heads * LANES
    return pl.pallas_call(
        body,
        grid=(B, DA_HEADS // heads, nq),
        in_specs=[
            pl.BlockSpec(memory_space=pltpu.SMEM),
            pl.BlockSpec((tq, width), lambda b, h, i: (b * nq + i, h)),
            pl.BlockSpec((S, width), lambda b, h, i: (b, h)),
            pl.BlockSpec((S, width), lambda b, h, i: (b, h)),
            vec, vec, vec, vec,
            pl.BlockSpec((1, LANES), lambda b, h, i: (0, 0)),
        ],
        out_specs=pl.BlockSpec((tq, width), lambda b, h, i: (b * nq + i, h)),
        out_shape=jax.ShapeDtypeStruct((B * S, HEAD_COLS), BF16),
        scratch_shapes=scratch,
        compiler_params=pltpu.CompilerParams(
            dimension_semantics=("arbitrary", "arbitrary", "arbitrary"), vmem_limit_bytes=VMEM_LIMIT),
        name="diff_attention_online" if online else "diff_attention",
    )(shift, q, k, v, lq1, lk1, lq2, lk2, onw)


def _merge_masks(row, col, n):
    masks = []
    b = 2
    while b < n:
        masks.append(((row // b) == (col // b) + 1) & ((row // (2 * b)) == (col // (2 * b))))
        b *= 2
    return masks


def _gdn_body(q_ref, k_ref, v_ref, z_ref, ab_ref, alog_ref, dtb_ref, onw_ref, o_ref,
              state_ref, l_ref, t_ref, rhs_ref, ou_ref, ku_ref, qa_ref, qk_ref, kdt_ref, egl_ref,
              fmask_ref, lmask_ref, *, tb, group):
    C = GDN_CHUNK
    nc = tb // C
    blk = pl.program_id(1)
    n_levels = lmask_ref.shape[0]

    @pl.when(blk == 0)
    def _():
        state_ref[...] = jnp.zeros_like(state_ref)
        row = lax.broadcasted_iota(jnp.int32, (C, C), 0)
        col = lax.broadcasted_iota(jnp.int32, (C, C), 1)
        fmask_ref[0] = (row >= col).astype(F32)
        fmask_ref[1] = (row > col).astype(F32)
        fmask_ref[2] = (row == col).astype(F32)
        fmask_ref[3] = ((row == col + 1) & (row % 2 == 1)).astype(F32)
        for lv, m in enumerate(_merge_masks(row, col, C)):
            lmask_ref[lv] = m.astype(BF16)

    ab = ab_ref[...]
    sp_in = ab + dtb_ref[...]
    softplus = jnp.maximum(sp_in, 0.0) + jnp.log(1.0 + jnp.exp(-jnp.abs(sp_in)))
    gtok = -jnp.exp(alog_ref[...]) * softplus
    gtok = jnp.where(lax.broadcasted_iota(jnp.int32, (tb, LANES), 1) < DN_HEADS, gtok, 0.0)
    beta_all = 1.0 / (1.0 + jnp.exp(-ab))

    tri = fmask_ref[0].astype(BF16)

    for c in range(nc):
        r0 = c * C
        g1, g2, g3 = _split3(gtok[r0:r0 + C])
        gc = _dot(tri, g1) + _dot(tri, g2) + _dot(tri, g3)
        gct = gc.T
        for h in range(DN_HEADS):
            un = c * DN_HEADS + h
            c0 = h * DN_DIM
            qh = q_ref[r0:r0 + C, c0:c0 + DN_DIM].astype(F32)
            kh_bf = k_ref[r0:r0 + C, c0:c0 + DN_DIM]
            kh = kh_bf.astype(F32)
            vh = v_ref[r0:r0 + C, c0:c0 + DN_DIM].astype(F32)
            gcol = gc[:, h:h + 1]
            grow = gct[h:h + 1, :]
            beta = beta_all[r0:r0 + C, DN_HEADS + h:DN_HEADS + h + 1]
            decay = jnp.exp(jnp.minimum(gcol - grow, 0.0))
            kb = kh * beta
            a = _dot_nt(jnp.concatenate([kb, qh], axis=0).astype(BF16), kh_bf)
            Lm = a[:C] * decay * fmask_ref[1]
            l_ref[un] = Lm.astype(BF16)
            t_ref[un] = (fmask_ref[2] - Lm * fmask_ref[3]).astype(BF16)
            qk_ref[un] = (a[C:] * decay * fmask_ref[0]).astype(BF16)
            eg = jnp.exp(gcol)
            rhs_ref[un] = jnp.concatenate([vh * beta, kb * eg], axis=1).astype(BF16)
            qa_ref[un, 0:C, :] = (qh * eg).astype(BF16)
            g_last = gcol[C - 1:C, :]
            kdt_ref[un] = (kh * jnp.exp(g_last - gcol)).T.astype(BF16)
            egl_ref[un] = jnp.broadcast_to(jnp.exp(g_last), (8, LANES))

    nu = nc * DN_HEADS
    for lv in range(n_levels):
        for g0 in range(0, nu, group):
            units = list(range(g0, min(g0 + group, nu)))
            ps = [_dot(l_ref[un] * lmask_ref[lv], t_ref[un]).astype(BF16) for un in units]
            for un, p in zip(units, ps):
                t_ref[un] = t_ref[un] - _dot(t_ref[un], p).astype(BF16)
    for un in range(nu):
        uw = _dot(t_ref[un], rhs_ref[un]).astype(BF16)
        r = _dot(jnp.concatenate([qk_ref[un], kdt_ref[un]], axis=0), uw)
        ou_ref[un] = r[:C, :DN_DIM]
        qa_ref[un, 0:C, :] = (qa_ref[un, 0:C, :].astype(F32) - r[:C, DN_DIM:]).astype(BF16)
        qa_ref[un, C:C + DN_DIM, :] = (-r[C:, DN_DIM:]).astype(BF16)
        ku_ref[un] = r[C:, :DN_DIM]

    for c in range(nc):
        r0 = c * C
        for h in range(DN_HEADS):
            un = c * DN_HEADS + h
            c0 = h * DN_DIM
            state = state_ref[h]
            r = _dot(qa_ref[un], state.astype(BF16))
            o = ou_ref[un] + r[:C]
            state_ref[h] = state * egl_ref[un][0:1, :] + r[C:] + ku_ref[un]
            zh = z_ref[r0:r0 + C, c0:c0 + DN_DIM]
            o = o * lax.rsqrt(jnp.mean(o * o, axis=-1, keepdims=True) + EPS) * onw_ref[...]
            o = o * _silu(zh)
            o_ref[r0:r0 + C, c0:c0 + DN_DIM] = o.astype(BF16)


def _gdn(gq, gk, gv, z, ab, alog, dtb, onw, B, S, tb):
    nb = S // tb
    C = GDN_CHUNK
    nu = (tb // C) * DN_HEADS
    const = lambda b, i: (0, 0)
    row = lambda b, i: (b * nb + i, 0)
    return pl.pallas_call(
        functools.partial(_gdn_body, tb=tb, group=8),
        grid=(B, nb),
        in_specs=[
            pl.BlockSpec((tb, HEAD_COLS), row),
            pl.BlockSpec((tb, HEAD_COLS), row),
            pl.BlockSpec((tb, HEAD_COLS), row),
            pl.BlockSpec((tb, HEAD_COLS), row),
            pl.BlockSpec((tb, LANES), row),
            pl.BlockSpec((1, LANES), const),
            pl.BlockSpec((1, LANES), const),
            pl.BlockSpec((1, LANES), const),
        ],
        out_specs=pl.BlockSpec((tb, HEAD_COLS), row),
        out_shape=jax.ShapeDtypeStruct((B * S, HEAD_COLS), BF16),
        scratch_shapes=[
            pltpu.VMEM((DN_HEADS, DN_DIM, DN_DIM), F32),
            pltpu.VMEM((nu, C, C), BF16),
            pltpu.VMEM((nu, C, C), BF16),
            pltpu.VMEM((nu, C, 2 * DN_DIM), BF16),
            pltpu.VMEM((nu, C, DN_DIM), F32),
            pltpu.VMEM((nu, DN_DIM, DN_DIM), F32),
            pltpu.VMEM((nu, C + DN_DIM, DN_DIM), BF16),
            pltpu.VMEM((nu, C, C), BF16),
            pltpu.VMEM((nu, DN_DIM, C), BF16),
            pltpu.VMEM((nu, 8, LANES), F32),
            pltpu.VMEM((4, C, C), F32),
            pltpu.VMEM((C.bit_length() - 2, C, C), BF16),
        ],
        compiler_params=pltpu.CompilerParams(
            dimension_semantics=("arbitrary", "arbitrary"), vmem_limit_bytes=VMEM_LIMIT),
        name="gated_deltanet",
    )(gq, gk, gv, z, ab, alog, dtb, onw)


def _mlp_body(x_ref, ma_ref, mb_ref, woa_ref, wob_ref, n2_ref, wup_ref, wdn_ref, o_ref, *, ff_chunk):
    x1 = x_ref[...] + _dot(ma_ref[...], woa_ref[...]) + _dot(mb_ref[...], wob_ref[...])
    ms = jnp.mean(x1 * x1, axis=-1, keepdims=True)
    h = (x1 * lax.rsqrt(ms + EPS) * n2_ref[...]).astype(BF16)
    mlp = None
    for c0 in range(0, wup_ref.shape[1], ff_chunk):
        up = jnp.maximum(_dot(h, wup_ref[:, c0:c0 + ff_chunk]), 0.0)
        down = _dot((up * up).astype(BF16), wdn_ref[c0:c0 + ff_chunk, :])
        mlp = down if mlp is None else mlp + down
    o_ref[...] = x1 + mlp


def _mlp(x2, mix_a, mix_b, woa, wob, n2, wup, wdn, tm, ff_chunk):
    T, D = x2.shape
    const = lambda i: (0, 0)
    row = lambda i: (i, 0)
    return pl.pallas_call(
        functools.partial(_mlp_body, ff_chunk=ff_chunk),
        grid=(T // tm,),
        in_specs=[
            pl.BlockSpec((tm, D), row),
            pl.BlockSpec((tm, HEAD_COLS), row),
            pl.BlockSpec((tm, HEAD_COLS), row),
            pl.BlockSpec(woa.shape, const),
            pl.BlockSpec(wob.shape, const),
            pl.BlockSpec((1, D), const),
            pl.BlockSpec(wup.shape, const),
            pl.BlockSpec(wdn.shape, const),
        ],
        out_specs=pl.BlockSpec((tm, D), row),
        out_shape=jax.ShapeDtypeStruct((T, D), F32),
        compiler_params=pltpu.CompilerParams(
            dimension_semantics=("arbitrary",), vmem_limit_bytes=VMEM_LIMIT),
        name="outproj_mlp",
    )(x2, mix_a, mix_b, woa, wob, n2, wup, wdn)


def _layer(x2, B, S, l, norm1_w, w_in, lambda_q1, lambda_k1, lambda_q2, lambda_k2, q_norm_w, k_norm_w,
           da_out_norm_w, conv_w, A_log, dt_bias, dn_out_norm_w, w_out, norm2_w, w_up, w_down,
           tm, tq, tb, ff_chunk):
    D = x2.shape[1]
    qkv_a = 3 * HEAD_COLS
    dn_cols = 3 * HEAD_COLS
    s4 = qkv_a + dn_cols
    s5 = s4 + HEAD_COLS
    w = w_in[l].astype(BF16)
    wa, wdn, wz = w[:, :qkv_a], w[:, qkv_a:s4], w[:, s4:s5]
    wab = jnp.pad(w[:, s5:], ((0, 0), (0, LANES - 2 * DN_HEADS)))
    tile2 = lambda v: jnp.concatenate([v, v]).reshape(1, LANES).astype(F32)
    pad_lanes = lambda v: jnp.pad(v.astype(F32), (0, LANES - v.shape[0])).reshape(1, LANES)

    q, k, v, gq, gk, gv, z, ab = _inproj(
        x2, norm1_w[l].reshape(1, D).astype(F32), wa, wdn, wz, wab,
        tile2(q_norm_w[l]), tile2(k_norm_w[l]), conv_w[l].astype(F32), S, tm)

    lam_init = 0.8 - 0.6 * math.exp(-0.3 * l)
    vec = lambda p: p[l].reshape(1, DA_QK_DIM).astype(F32)
    bound = 8.0 * jnp.max(jnp.abs(q_norm_w[l].astype(F32))) * jnp.max(jnp.abs(k_norm_w[l].astype(F32)))
    attn_args = ((bound * LOG2E).reshape(1), q, k, v, vec(lambda_q1), vec(lambda_k1), vec(lambda_q2),
                 vec(lambda_k2), da_out_norm_w[l].reshape(1, LANES).astype(F32))
    mix_a = lax.cond(bound <= MAX_FIXED_SHIFT,
                     lambda a: _attention(*a, B, S, tq, lam_init, online=False),
                     lambda a: _attention(*a, B, S, tq, lam_init, online=True), attn_args)

    mix_b = _gdn(gq, gk, gv, z, ab, pad_lanes(A_log[l]), pad_lanes(dt_bias[l]),
                 dn_out_norm_w[l].reshape(1, LANES).astype(F32), B, S, tb)

    wo = w_out[l].astype(BF16)
    return _mlp(x2, mix_a, mix_b, wo[:HEAD_COLS], wo[HEAD_COLS:], norm2_w[l].reshape(1, D).astype(F32),
                w_up[l].astype(BF16), w_down[l].astype(BF16), tm, ff_chunk)


def kernel(x, norm1_w, w_in, lambda_q1, lambda_k1, lambda_q2, lambda_k2, q_norm_w, k_norm_w, da_out_norm_w,
           conv_w, A_log, dt_bias, dn_out_norm_w, w_out, norm2_w, w_up, w_down):
    B, S, D = x.shape
    x2 = x.reshape(B * S, D)
    tm = min(512, S)
    tq = min(512, S)
    tb = min(512, S)
    for l in range(w_in.shape[0]):
        x2 = _layer(x2, B, S, l, norm1_w, w_in, lambda_q1, lambda_k1, lambda_q2, lambda_k2, q_norm_w,
                    k_norm_w, da_out_norm_w, conv_w, A_log, dt_bias, dn_out_norm_w, w_out, norm2_w,
                    w_up, w_down, tm, tq, tb, 1024)
    return x2.reshape(B, S, D)
```

```python
import functools
import math

import jax
import jax.numpy as jnp
from jax import lax
from jax.experimental import pallas as pl
from jax.experimental.pallas import tpu as pltpu

F32 = jnp.float32
BF16 = jnp.bfloat16
EPS = 1e-6
LANES = 128
NEG_BIG = -1e30
LOG2E = math.log2(math.e)
MAX_FIXED_SHIFT = 40.0

DA_HEADS = 4
DA_QK_DIM = 64
DN_HEADS = 4
DN_DIM = 128
CONV_WIDTH = 4
HEAD_COLS = 512
GDN_CHUNK = 128
VMEM_LIMIT = 56 * 1024 * 1024


def _dot(a, b):
    return jnp.dot(a, b, preferred_element_type=F32)


def _dot_nt(a, b):
    return lax.dot_general(a, b, (((1,), (1,)), ((), ())), preferred_element_type=F32)


def _dot_tn(a, b):
    return lax.dot_general(a, b, (((0,), (0,)), ((), ())), preferred_element_type=F32)


def _silu(x):
    h = 0.5 * x
    return h + h * jnp.tanh(h)


def _split3(x):
    x1 = x.astype(BF16)
    r1 = x - x1.astype(F32)
    x2 = r1.astype(BF16)
    x3 = (r1 - x2.astype(F32)).astype(BF16)
    return x1, x2, x3


def _inproj_body(x_ref, n1_ref, wa_ref, wdn_ref, wz_ref, wab_ref, qnw_ref, knw_ref, cw_ref,
                 q_out, k_out, v_out, gq_out, gk_out, gv_out, z_out, ab_out, tail_ref, ext_ref,
                 *, tiles_per_seq):
    tm = x_ref.shape[0]
    x = x_ref[...]
    ms = jnp.mean(x * x, axis=-1, keepdims=True)
    h = (x * lax.rsqrt(ms + EPS) * n1_ref[...]).astype(BF16)

    @pl.when(pl.program_id(0) % tiles_per_seq == 0)
    def _():
        tail_ref[...] = jnp.zeros_like(tail_ref)

    dn = _dot(h, wdn_ref[...])
    ext_ref[0:8, :] = tail_ref[...]
    ext_ref[8:8 + tm, :] = dn
    tail_ref[...] = dn[tm - 8:tm, :]
    for r0 in range(0, tm, GDN_CHUNK):
        for cb in range(3 * DN_HEADS):
            c0 = cb * LANES
            a = ext_ref[r0:r0 + GDN_CHUNK + 8, c0:c0 + LANES]
            y = cw_ref[CONV_WIDTH - 1:CONV_WIDTH, c0:c0 + LANES] * a[8:]
            for w in range(CONV_WIDTH - 1):
                shifted = pltpu.roll(a, CONV_WIDTH - 1 - w, axis=0)[8:]
                y = y + cw_ref[w:w + 1, c0:c0 + LANES] * shifted
            y = _silu(y)
            if cb < DN_HEADS:
                y = y * (lax.rsqrt(jnp.sum(y * y, axis=-1, keepdims=True) + EPS) * (DN_DIM ** -0.5))
            elif cb < 2 * DN_HEADS:
                y = y * lax.rsqrt(jnp.sum(y * y, axis=-1, keepdims=True) + EPS)
            out = (gq_out, gk_out, gv_out)[cb // DN_HEADS]
            c1 = (cb % DN_HEADS) * LANES
            out[r0:r0 + GDN_CHUNK, c1:c1 + LANES] = y.astype(BF16)

    qkv = _dot(h, wa_ref[...])
    lo = lax.broadcasted_iota(jnp.int32, (tm, LANES), 1) < DA_QK_DIM

    def qk_norm(t, w):
        sq = t * t
        s_lo = jnp.sum(jnp.where(lo, sq, 0.0), axis=-1, keepdims=True)
        s_hi = jnp.sum(jnp.where(lo, 0.0, sq), axis=-1, keepdims=True)
        ms_ = jnp.where(lo, s_lo, s_hi) * (1.0 / DA_QK_DIM)
        return t * lax.rsqrt(ms_ + EPS) * w

    for hd in range(DA_HEADS):
        c0 = hd * LANES
        qh = qk_norm(qkv[:, c0:c0 + LANES], qnw_ref[...]) * (DA_QK_DIM ** -0.5 * LOG2E)
        kh = qk_norm(qkv[:, HEAD_COLS + c0:HEAD_COLS + c0 + LANES], knw_ref[...])
        q_out[:, c0:c0 + LANES] = qh.astype(BF16)
        k_out[:, c0:c0 + LANES] = kh.astype(BF16)
    v_out[...] = qkv[:, 2 * HEAD_COLS:].astype(BF16)

    z_out[...] = _dot(h, wz_ref[...])
    ab_out[...] = _dot(h, wab_ref[...])


def _inproj(x2, n1, wa, wdn, wz, wab, qnw, knw, cw, S, tm):
    T, D = x2.shape
    W = wdn.shape[1]
    const = lambda i: (0, 0)
    row = lambda i: (i, 0)
    slab = pl.BlockSpec((tm, HEAD_COLS), row)
    slab_bf16 = jax.ShapeDtypeStruct((T, HEAD_COLS), BF16)
    return pl.pallas_call(
        functools.partial(_inproj_body, tiles_per_seq=S // tm),
        grid=(T // tm,),
        in_specs=[
            pl.BlockSpec((tm, D), row),
            pl.BlockSpec((1, D), const),
            pl.BlockSpec(wa.shape, const),
            pl.BlockSpec(wdn.shape, const),
            pl.BlockSpec(wz.shape, const),
            pl.BlockSpec(wab.shape, const),
            pl.BlockSpec((1, LANES), const),
            pl.BlockSpec((1, LANES), const),
            pl.BlockSpec((CONV_WIDTH, W), const),
        ],
        out_specs=[slab, slab, slab, slab, slab, slab, slab, pl.BlockSpec((tm, LANES), row)],
        out_shape=[slab_bf16, slab_bf16, slab_bf16, slab_bf16, slab_bf16, slab_bf16,
                   jax.ShapeDtypeStruct((T, HEAD_COLS), F32),
                   jax.ShapeDtypeStruct((T, LANES), F32)],
        scratch_shapes=[
            pltpu.VMEM((8, W), F32),
            pltpu.VMEM((tm + 8, W), F32),
        ],
        compiler_params=pltpu.CompilerParams(
            dimension_semantics=("arbitrary",), vmem_limit_bytes=VMEM_LIMIT),
        name="inproj",
    )(x2, n1, wa, wdn, wz, wab, qnw, knw, cw)


def _stack_maps(q_ref, qz_ref, tq):
    q = q_ref[...]
    lane = lax.broadcasted_iota(jnp.int32, (tq, LANES), 1)
    zero = jnp.zeros_like(q)
    qz_ref[0:tq, :] = jnp.where(lane < DA_QK_DIM, q, zero)
    qz_ref[tq:2 * tq, :] = jnp.where(lane < DA_QK_DIM, zero, q)


def _causal_mask(tq):
    rows = lax.broadcasted_iota(jnp.int32, (2 * tq, tq), 0)
    cols = lax.broadcasted_iota(jnp.int32, (2 * tq, tq), 1)
    return jnp.where(rows >= tq, rows - tq, rows) >= cols


def _attn_finish(acc, l, lq1_ref, lk1_ref, lq2_ref, lk2_ref, onw_ref, o_ref, tq, lam_init):
    lam = (jnp.exp(jnp.sum(lq1_ref[...] * lk1_ref[...], axis=-1, keepdims=True))
           - jnp.exp(jnp.sum(lq2_ref[...] * lk2_ref[...], axis=-1, keepdims=True)) + lam_init)
    o = acc[:tq] / l[:tq] - lam * (acc[tq:] / l[tq:])
    ms = jnp.mean(o * o, axis=-1, keepdims=True)
    o = o * lax.rsqrt(ms + EPS) * onw_ref[...] * (1.0 - lam_init)
    o_ref[...] = o.astype(BF16)


def _attn_shift_body(shift_ref, q_ref, k_ref, v_ref, lq1_ref, lk1_ref, lq2_ref, lk2_ref, onw_ref, o_ref,
                     qz_ref, p_ref, l_ref, acc_ref, *, tq, lam_init, unroll, heads):
    i = pl.program_id(2)
    shift = shift_ref[0]
    cols = lambda hh: slice(hh * LANES, (hh + 1) * LANES)
    rows = lambda j: pl.ds(pl.multiple_of(j * tq, tq), tq)

    def probs(hh, j, diagonal):
        s = _dot_nt(qz_ref[hh], k_ref[rows(j), cols(hh)]) - shift
        if diagonal:
            s = jnp.where(_causal_mask(tq), s, NEG_BIG)
        p = jnp.exp2(s)
        psum = p[:, 0:LANES]
        for c0 in range(LANES, tq, LANES):
            psum = psum + p[:, c0:c0 + LANES]
        return p.astype(BF16), psum

    for hh in range(heads):
        _stack_maps(q_ref.at[:, cols(hh)], qz_ref.at[hh], tq)
        p, psum = probs(hh, i, True)
        p_ref[hh] = p
        l_ref[hh] = psum
        acc_ref[hh] = jnp.zeros((2 * tq, LANES), F32)

    def pv_prev(hh, j):
        return _dot(p_ref[hh], v_ref[rows(jnp.where(j == 0, i, j - 1)), cols(hh)])

    def steps(j0, n):
        acc = [pv_prev(hh, j0) for hh in range(heads)]
        psum = [None] * heads
        for u in range(n):
            for hh in range(heads):
                p, ps = probs(hh, j0 + u, False)
                psum[hh] = ps if psum[hh] is None else psum[hh] + ps
                if u + 1 < n:
                    acc[hh] = acc[hh] + _dot(p, v_ref[rows(j0 + u), cols(hh)])
                else:
                    p_ref[hh] = p
        for hh in range(heads):
            acc_ref[hh] += acc[hh]
            l_ref[hh] += psum[hh]

    start = 0
    n = unroll
    while n >= 1:
        trips = (i - start) // n

        def group(t, _, start=start, n=n):
            steps(start + t * n, n)
            return 0

        lax.fori_loop(0, trips, group, 0)
        start = start + trips * n
        n //= 2
    for hh in range(heads):
        acc = acc_ref[hh] + pv_prev(hh, i)
        l = jnp.sum(l_ref[hh], axis=-1, keepdims=True)
        _attn_finish(acc, l, lq1_ref, lk1_ref, lq2_ref, lk2_ref, onw_ref, o_ref.at[:, cols(hh)], tq, lam_init)


def _attn_online_body(shift_ref, q_ref, k_ref, v_ref, lq1_ref, lk1_ref, lq2_ref, lk2_ref, onw_ref, o_ref,
                      qz_ref, *, tq, lam_init):
    del shift_ref
    i = pl.program_id(2)
    _stack_maps(q_ref, qz_ref, tq)

    def step(j, carry, diagonal):
        m, l, acc = carry
        r0 = pl.multiple_of(j * tq, tq)
        s = _dot_nt(qz_ref[...], k_ref[pl.ds(r0, tq), :])
        if diagonal:
            s = jnp.where(_causal_mask(tq), s, NEG_BIG)
        m_new = jnp.maximum(m, jnp.max(s, axis=-1, keepdims=True))
        p = jnp.exp2(s - m_new)
        alpha = jnp.exp2(m - m_new)
        l = alpha * l + jnp.sum(p, axis=-1, keepdims=True)
        acc = alpha * acc + _dot(p.astype(BF16), v_ref[pl.ds(r0, tq), :])
        return m_new, l, acc

    init = (jnp.full((2 * tq, 1), NEG_BIG, F32), jnp.zeros((2 * tq, 1), F32),
            jnp.zeros((2 * tq, LANES), F32))
    carry = lax.fori_loop(0, i, lambda j, c: step(j, c, False), init)
    _, l, acc = step(i, carry, True)
    _attn_finish(acc, l, lq1_ref, lk1_ref, lq2_ref, lk2_ref, onw_ref, o_ref, tq, lam_init)


def _attention(shift, q, k, v, lq1, lk1, lq2, lk2, onw, B, S, tq, lam_init, online):
    nq = S // tq
    vec = pl.BlockSpec((1, DA_QK_DIM), lambda b, h, i: (0, 0))
    if online:
        heads = 1
        body = functools.partial(_attn_online_body, tq=tq, lam_init=lam_init)
        scratch = [pltpu.VMEM((2 * tq, LANES), BF16)]
    else:
        heads = DA_HEADS
        body = functools.partial(_attn_shift_body, tq=tq, lam_init=lam_init, unroll=4, heads=heads)
        scratch = [pltpu.VMEM((heads, 2 * tq, LANES), BF16),
                   pltpu.VMEM((heads, 2 * tq, tq), BF16),
                   pltpu.VMEM((heads, 2 * tq, LANES), F32),
                   pltpu.VMEM((heads, 2 * tq, LANES), F32)]
    width = heads * LANES
    return pl.pallas_call(
        body,
        grid=(B, DA_HEADS // heads, nq),
        in_specs=[
            pl.BlockSpec(memory_space=pltpu.SMEM),
            pl.BlockSpec((tq, width), lambda b, h, i: (b * nq + i, h)),
            pl.BlockSpec((S, width), lambda b, h, i: (b, h)),
            pl.BlockSpec((S, width), lambda b, h, i: (b, h)),
            vec, vec, vec, vec,
            pl.BlockSpec((1, LANES), lambda b, h, i: (0, 0)),
        ],
        out_specs=pl.BlockSpec((tq, width), lambda b, h, i: (b * nq + i, h)),
        out_shape=jax.ShapeDtypeStruct((B * S, HEAD_COLS), BF16),
        scratch_shapes=scratch,
        compiler_params=pltpu.CompilerParams(
            dimension_semantics=("arbitrary", "arbitrary", "arbitrary"), vmem_limit_bytes=VMEM_LIMIT),
        name="diff_attention_online" if online else "diff_attention",
    )(shift, q, k, v, lq1, lk1, lq2, lk2, onw)


def _merge_masks(row, col, n):
    masks = []
    b = 2
    while b < n:
        masks.append(((row // b) == (col // b) + 1) & ((row // (2 * b)) == (col // (2 * b))))
        b *= 2
    return masks


def _gdn_body(q_ref, k_ref, v_ref, z_ref, ab_ref, alog_ref, dtb_ref, onw_ref, o_ref,
              state_ref, l_ref, t_ref, rhs_ref, ou_ref, ku_ref, qa_ref, qk_ref, kdt_ref, egl_ref,
              fmask_ref, lmask_ref, *, tb, group):
    C = GDN_CHUNK
    nc = tb // C
    blk = pl.program_id(1)
    n_levels = lmask_ref.shape[0]

    @pl.when(blk == 0)
    def _():
        state_ref[...] = jnp.zeros_like(state_ref)
        row = lax.broadcasted_iota(jnp.int32, (C, C), 0)
        col = lax.broadcasted_iota(jnp.int32, (C, C), 1)
        fmask_ref[0] = (row >= col).astype(F32)
        fmask_ref[1] = (row > col).astype(F32)
        fmask_ref[2] = (row == col).astype(F32)
        fmask_ref[3] = ((row == col + 1) & (row % 2 == 1)).astype(F32)
        for lv, m in enumerate(_merge_masks(row, col, C)):
            lmask_ref[lv] = m.astype(BF16)

    ab = ab_ref[...]
    sp_in = ab + dtb_ref[...]
    softplus = jnp.maximum(sp_in, 0.0) + jnp.log(1.0 + jnp.exp(-jnp.abs(sp_in)))
    gtok = -jnp.exp(alog_ref[...]) * softplus
    gtok = jnp.where(lax.broadcasted_iota(jnp.int32, (tb, LANES), 1) < DN_HEADS, gtok, 0.0)
    beta_all = 1.0 / (1.0 + jnp.exp(-ab))

    tri = fmask_ref[0].astype(BF16)

    for c in range(nc):
        r0 = c * C
        g1, g2, g3 = _split3(gtok[r0:r0 + C])
        gc = _dot(tri, g1) + _dot(tri, g2) + _dot(tri, g3)
        gct = gc.T
        for h in range(DN_HEADS):
            un = c * DN_HEADS + h
            c0 = h * DN_DIM
            qh = q_ref[r0:r0 + C, c0:c0 + DN_DIM].astype(F32)
            kh_bf = k_ref[r0:r0 + C, c0:c0 + DN_DIM]
            kh = kh_bf.astype(F32)
            vh = v_ref[r0:r0 + C, c0:c0 + DN_DIM].astype(F32)
            gcol = gc[:, h:h + 1]
            grow = gct[h:h + 1, :]
            beta = beta_all[r0:r0 + C, DN_HEADS + h:DN_HEADS + h + 1]
            decay = jnp.exp(jnp.minimum(gcol - grow, 0.0))
            kb = kh * beta
            a = _dot_nt(jnp.concatenate([kb, qh], axis=0).astype(BF16), kh_bf)
            Lm = a[:C] * decay * fmask_ref[1]
            l_ref[un] = Lm.astype(BF16)
            t_ref[un] = (fmask_ref[2] - Lm * fmask_ref[3]).astype(BF16)
            qk_ref[un] = (a[C:] * decay * fmask_ref[0]).astype(BF16)
            eg = jnp.exp(gcol)
            rhs_ref[un] = jnp.concatenate([vh * beta, kb * eg], axis=1).astype(BF16)
            qa_ref[un, 0:C, :] = (qh * eg).astype(BF16)
            g_last = gcol[C - 1:C, :]
            kdt_ref[un] = (kh * jnp.exp(g_last - gcol)).T.astype(BF16)
            egl_ref[un] = jnp.broadcast_to(jnp.exp(g_last), (8, LANES))

    nu = nc * DN_HEADS
    for lv in range(n_levels):
        for g0 in range(0, nu, group):
            units = list(range(g0, min(g0 + group, nu)))
            ps = [_dot(l_ref[un] * lmask_ref[lv], t_ref[un]).astype(BF16) for un in units]
            for un, p in zip(units, ps):
                t_ref[un] = t_ref[un] - _dot(t_ref[un], p).astype(BF16)
    for un in range(nu):
        uw = _dot(t_ref[un], rhs_ref[un]).astype(BF16)
        r = _dot(jnp.concatenate([qk_ref[un], kdt_ref[un]], axis=0), uw)
        ou_ref[un] = r[:C, :DN_DIM]
        qa_ref[un, 0:C, :] = (qa_ref[un, 0:C, :].astype(F32) - r[:C, DN_DIM:]).astype(BF16)
        qa_ref[un, C:C + DN_DIM, :] = (-r[C:, DN_DIM:]).astype(BF16)
        ku_ref[un] = r[C:, :DN_DIM]

    for c in range(nc):
        r0 = c * C
        for h in range(DN_HEADS):
            un = c * DN_HEADS + h
            c0 = h * DN_DIM
            state = state_ref[h]
            r = _dot(qa_ref[un], state.astype(BF16))
            o = ou_ref[un] + r[:C]
            state_ref[h] = state * egl_ref[un][0:1, :] + r[C:] + ku_ref[un]
            zh = z_ref[r0:r0 + C, c0:c0 + DN_DIM]
            o = o * lax.rsqrt(jnp.mean(o * o, axis=-1, keepdims=True) + EPS) * onw_ref[...]
            o = o * _silu(zh)
            o_ref[r0:r0 + C, c0:c0 + DN_DIM] = o.astype(BF16)


def _gdn(gq, gk, gv, z, ab, alog, dtb, onw, B, S, tb):
    nb = S // tb
    C = GDN_CHUNK
    nu = (tb // C) * DN_HEADS
    const = lambda b, i: (0, 0)
    row = lambda b, i: (b * nb + i, 0)
    return pl.pallas_call(
        functools.partial(_gdn_body, tb=tb, group=8),
        grid=(B, nb),
        in_specs=[
            pl.BlockSpec((tb, HEAD_COLS), row),
            pl.BlockSpec((tb, HEAD_COLS), row),
            pl.BlockSpec((tb, HEAD_COLS), row),
            pl.BlockSpec((tb, HEAD_COLS), row),
            pl.BlockSpec((tb, LANES), row),
            pl.BlockSpec((1, LANES), const),
            pl.BlockSpec((1, LANES), const),
            pl.BlockSpec((1, LANES), const),
        ],
        out_specs=pl.BlockSpec((tb, HEAD_COLS), row),
        out_shape=jax.ShapeDtypeStruct((B * S, HEAD_COLS), BF16),
        scratch_shapes=[
            pltpu.VMEM((DN_HEADS, DN_DIM, DN_DIM), F32),
            pltpu.VMEM((nu, C, C), BF16),
            pltpu.VMEM((nu, C, C), BF16),
            pltpu.VMEM((nu, C, 2 * DN_DIM), BF16),
            pltpu.VMEM((nu, C, DN_DIM), F32),
            pltpu.VMEM((nu, DN_DIM, DN_DIM), F32),
            pltpu.VMEM((nu, C + DN_DIM, DN_DIM), BF16),
            pltpu.VMEM((nu, C, C), BF16),
            pltpu.VMEM((nu, DN_DIM, C), BF16),
            pltpu.VMEM((nu, 8, LANES), F32),
            pltpu.VMEM((4, C, C), F32),
            pltpu.VMEM((C.bit_length() - 2, C, C), BF16),
        ],
        compiler_params=pltpu.CompilerParams(
            dimension_semantics=("arbitrary", "arbitrary"), vmem_limit_bytes=VMEM_LIMIT),
        name="gated_deltanet",
    )(gq, gk, gv, z, ab, alog, dtb, onw)


def _mlp_body(x_ref, ma_ref, mb_ref, woa_ref, wob_ref, n2_ref, wup_ref, wdn_ref, o_ref, *, ff_chunk):
    x1 = x_ref[...] + _dot(ma_ref[...], woa_ref[...]) + _dot(mb_ref[...], wob_ref[...])
    ms = jnp.mean(x1 * x1, axis=-1, keepdims=True)
    h = (x1 * lax.rsqrt(ms + EPS) * n2_ref[...]).astype(BF16)
    mlp = None
    for c0 in range(0, wup_ref.shape[1], ff_chunk):
        up = jnp.maximum(_dot(h, wup_ref[:, c0:c0 + ff_chunk]), 0.0)
        down = _dot((up * up).astype(BF16), wdn_ref[c0:c0 + ff_chunk, :])
        mlp = down if mlp is None else mlp + down
    o_ref[...] = x1 + mlp


def _mlp(x2, mix_a, mix_b, woa, wob, n2, wup, wdn, tm, ff_chunk):
    T, D = x2.shape
    const = lambda i: (0, 0)
    row = lambda i: (i, 0)
    return pl.pallas_call(
        functools.partial(_mlp_body, ff_chunk=ff_chunk),
        grid=(T // tm,),
        in_specs=[
            pl.BlockSpec((tm, D), row),
            pl.BlockSpec((tm, HEAD_COLS), row),
            pl.BlockSpec((tm, HEAD_COLS), row),
            pl.BlockSpec(woa.shape, const),
            pl.BlockSpec(wob.shape, const),
            pl.BlockSpec((1, D), const),
            pl.BlockSpec(wup.shape, const),
            pl.BlockSpec(wdn.shape, const),
        ],
        out_specs=pl.BlockSpec((tm, D), row),
        out_shape=jax.ShapeDtypeStruct((T, D), F32),
        compiler_params=pltpu.CompilerParams(
            dimension_semantics=("arbitrary",), vmem_limit_bytes=VMEM_LIMIT),
        name="outproj_mlp",
    )(x2, mix_a, mix_b, woa, wob, n2, wup, wdn)


def _layer(x2, B, S, l, norm1_w, w_in, lambda_q1, lambda_k1, lambda_q2, lambda_k2, q_norm_w, k_norm_w,
           da_out_norm_w, conv_w, A_log, dt_bias, dn_out_norm_w, w_out, norm2_w, w_up, w_down,
           tm, tq, tb, ff_chunk):
    D = x2.shape[1]
    qkv_a = 3 * HEAD_COLS
    dn_cols = 3 * HEAD_COLS
    s4 = qkv_a + dn_cols
    s5 = s4 + HEAD_COLS
    w = w_in[l].astype(BF16)
    wa, wdn, wz = w[:, :qkv_a], w[:, qkv_a:s4], w[:, s4:s5]
    wab = jnp.pad(w[:, s5:], ((0, 0), (0, LANES - 2 * DN_HEADS)))
    tile2 = lambda v: jnp.concatenate([v, v]).reshape(1, LANES).astype(F32)
    pad_lanes = lambda v: jnp.pad(v.astype(F32), (0, LANES - v.shape[0])).reshape(1, LANES)

    q, k, v, gq, gk, gv, z, ab = _inproj(
        x2, norm1_w[l].reshape(1, D).astype(F32), wa, wdn, wz, wab,
        tile2(q_norm_w[l]), tile2(k_norm_w[l]), conv_w[l].astype(F32), S, tm)

    lam_init = 0.8 - 0.6 * math.exp(-0.3 * l)
    vec = lambda p: p[l].reshape(1, DA_QK_DIM).astype(F32)
    bound = 8.0 * jnp.max(jnp.abs(q_norm_w[l].astype(F32))) * jnp.max(jnp.abs(k_norm_w[l].astype(F32)))
    attn_args = ((bound * LOG2E).reshape(1), q, k, v, vec(lambda_q1), vec(lambda_k1), vec(lambda_q2),
                 vec(lambda_k2), da_out_norm_w[l].reshape(1, LANES).astype(F32))
    mix_a = lax.cond(bound <= MAX_FIXED_SHIFT,
                     lambda a: _attention(*a, B, S, tq, lam_init, online=False),
                     lambda a: _attention(*a, B, S, tq, lam_init, online=True), attn_args)

    mix_b = _gdn(gq, gk, gv, z, ab, pad_lanes(A_log[l]), pad_lanes(dt_bias[l]),
                 dn_out_norm_w[l].reshape(1, LANES).astype(F32), B, S, tb)

    wo = w_out[l].astype(BF16)
    return _mlp(x2, mix_a, mix_b, wo[:HEAD_COLS], wo[HEAD_COLS:], norm2_w[l].reshape(1, D).astype(F32),
                w_up[l].astype(BF16), w_down[l].astype(BF16), tm, ff_chunk)


def kernel(x, norm1_w, w_in, lambda_q1, lambda_k1, lambda_q2, lambda_k2, q_norm_w, k_norm_w, da_out_norm_w,
           conv_w, A_log, dt_bias, dn_out_norm_w, w_out, norm2_w, w_up, w_down):
    B, S, D = x.shape
    x2 = x.reshape(B * S, D)
    tm = min(512, S)
    tq = min(512, S)
    tb = min(512, S)
    for l in range(w_in.shape[0]):
        x2 = _layer(x2, B, S, l, norm1_w, w_in, lambda_q1, lambda_k1, lambda_q2, lambda_k2, q_norm_w,
                    k_norm_w, da_out_norm_w, conv_w, A_log, dt_bias, dn_out_norm_w, w_out, norm2_w,
                    w_up, w_down, tm, tq, tb, 1024)
    return x2.reshape(B, S, D)
```

```python
import functools
import math

import jax
import jax.numpy as jnp
from jax import lax
from jax.experimental import pallas as pl
from jax.experimental.pallas import tpu as pltpu

F32 = jnp.float32
BF16 = jnp.bfloat16
EPS = 1e-6
LANES = 128
NEG_BIG = -1e30
LOG2E = math.log2(math.e)
MAX_FIXED_SHIFT = 40.0

DA_HEADS = 4
DA_QK_DIM = 64
DN_HEADS = 4
DN_DIM = 128
CONV_WIDTH = 4
HEAD_COLS = 512
GDN_CHUNK = 128
VMEM_LIMIT = 56 * 1024 * 1024


def _dot(a, b):
    return jnp.dot(a, b, preferred_element_type=F32)


def _dot_nt(a, b):
    return lax.dot_general(a, b, (((1,), (1,)), ((), ())), preferred_element_type=F32)


def _dot_tn(a, b):
    return lax.dot_general(a, b, (((0,), (0,)), ((), ())), preferred_element_type=F32)


def _silu(x):
    h = 0.5 * x
    return h + h * jnp.tanh(h)


def _split3(x):
    x1 = x.astype(BF16)
    r1 = x - x1.astype(F32)
    x2 = r1.astype(BF16)
    x3 = (r1 - x2.astype(F32)).astype(BF16)
    return x1, x2, x3


def _inproj_body(x_ref, n1_ref, wa_ref, wdn_ref, wz_ref, wab_ref, qnw_ref, knw_ref, cw_ref,
                 q_out, k_out, v_out, gq_out, gk_out, gv_out, z_out, ab_out, tail_ref,
                 *, tiles_per_seq):
    tm = x_ref.shape[0]
    piece = 2 * LANES
    x = x_ref[...]
    ms = jnp.mean(x * x, axis=-1, keepdims=True)
    h = (x * lax.rsqrt(ms + EPS) * n1_ref[...]).astype(BF16)

    @pl.when(pl.program_id(0) % tiles_per_seq == 0)
    def _():
        tail_ref[...] = jnp.zeros_like(tail_ref)

    def deltanet_piece(p0):
        d = _dot(h, wdn_ref[:, p0:p0 + piece])
        tail = tail_ref[:, p0:p0 + piece]
        tail_ref[:, p0:p0 + piece] = d[tm - 8:tm, :]
        for cc in range(0, piece, LANES):
            c0 = p0 + cc
            cb = c0 // LANES
            for r0 in range(0, tm, GDN_CHUNK):
                halo = tail[:, cc:cc + LANES] if r0 == 0 else d[r0 - 8:r0, cc:cc + LANES]
                a = jnp.concatenate([halo, d[r0:r0 + GDN_CHUNK, cc:cc + LANES]], axis=0)
                y = cw_ref[CONV_WIDTH - 1:CONV_WIDTH, c0:c0 + LANES] * a[8:]
                for w in range(CONV_WIDTH - 1):
                    shifted = pltpu.roll(a, CONV_WIDTH - 1 - w, axis=0)[8:]
                    y = y + cw_ref[w:w + 1, c0:c0 + LANES] * shifted
                y = y + y * jnp.tanh(y)
                if cb < DN_HEADS:
                    y = y * (lax.rsqrt(jnp.sum(y * y, axis=-1, keepdims=True) + EPS) * (DN_DIM ** -0.5))
                elif cb < 2 * DN_HEADS:
                    y = y * lax.rsqrt(jnp.sum(y * y, axis=-1, keepdims=True) + EPS)
                out = (gq_out, gk_out, gv_out)[cb // DN_HEADS]
                c1 = (cb % DN_HEADS) * LANES
                out[r0:r0 + GDN_CHUNK, c1:c1 + LANES] = y.astype(BF16)

    lo = lax.broadcasted_iota(jnp.int32, (tm, LANES), 1) < DA_QK_DIM

    def qk_norm(t, w):
        sq = t * t
        s_lo = jnp.sum(jnp.where(lo, sq, 0.0), axis=-1, keepdims=True)
        s_hi = jnp.sum(jnp.where(lo, 0.0, sq), axis=-1, keepdims=True)
        ms_ = jnp.where(lo, s_lo, s_hi) * (1.0 / DA_QK_DIM)
        return t * lax.rsqrt(ms_ + EPS) * w

    def attention_piece(p0):
        t = _dot(h, wa_ref[:, p0:p0 + piece])
        for cc in range(0, piece, LANES):
            th = t[:, cc:cc + LANES]
            c1 = p0 % HEAD_COLS + cc
            if p0 < HEAD_COLS:
                q_out[:, c1:c1 + LANES] = (qk_norm(th, qnw_ref[...]) * (DA_QK_DIM ** -0.5 * LOG2E)).astype(BF16)
            elif p0 < 2 * HEAD_COLS:
                k_out[:, c1:c1 + LANES] = qk_norm(th, knw_ref[...]).astype(BF16)
            else:
                v_out[:, c1:c1 + LANES] = th.astype(BF16)

    for p0 in range(0, 3 * HEAD_COLS, piece):
        deltanet_piece(p0)
        attention_piece(p0)
    for p0 in range(0, HEAD_COLS, piece):
        z_out[:, p0:p0 + piece] = _dot(h, wz_ref[:, p0:p0 + piece])
    ab_out[...] = _dot(h, wab_ref[...])


def _inproj(x2, n1, wa, wdn, wz, wab, qnw, knw, cw, S, tm):
    T, D = x2.shape
    W = wdn.shape[1]
    const = lambda i: (0, 0)
    row = lambda i: (i, 0)
    slab = pl.BlockSpec((tm, HEAD_COLS), row)
    slab_bf16 = jax.ShapeDtypeStruct((T, HEAD_COLS), BF16)
    return pl.pallas_call(
        functools.partial(_inproj_body, tiles_per_seq=S // tm),
        grid=(T // tm,),
        in_specs=[
            pl.BlockSpec((tm, D), row),
            pl.BlockSpec((1, D), const),
            pl.BlockSpec(wa.shape, const),
            pl.BlockSpec(wdn.shape, const),
            pl.BlockSpec(wz.shape, const),
            pl.BlockSpec(wab.shape, const),
            pl.BlockSpec((1, LANES), const),
            pl.BlockSpec((1, LANES), const),
            pl.BlockSpec((CONV_WIDTH, W), const),
        ],
        out_specs=[slab, slab, slab, slab, slab, slab, slab, pl.BlockSpec((tm, LANES), row)],
        out_shape=[slab_bf16, slab_bf16, slab_bf16, slab_bf16, slab_bf16, slab_bf16,
                   jax.ShapeDtypeStruct((T, HEAD_COLS), F32),
                   jax.ShapeDtypeStruct((T, LANES), F32)],
        scratch_shapes=[
            pltpu.VMEM((8, W), F32),
        ],
        compiler_params=pltpu.CompilerParams(
            dimension_semantics=("arbitrary",), vmem_limit_bytes=VMEM_LIMIT),
        name="inproj",
    )(x2, n1, wa, wdn, wz, wab, qnw, knw, cw)


def _stack_maps(q_ref, qz_ref, tq):
    q = q_ref[...]
    lane = lax.broadcasted_iota(jnp.int32, (tq, LANES), 1)
    zero = jnp.zeros_like(q)
    qz_ref[0:tq, :] = jnp.where(lane < DA_QK_DIM, q, zero)
    qz_ref[tq:2 * tq, :] = jnp.where(lane < DA_QK_DIM, zero, q)


def _causal_mask(tq):
    rows = lax.broadcasted_iota(jnp.int32, (2 * tq, tq), 0)
    cols = lax.broadcasted_iota(jnp.int32, (2 * tq, tq), 1)
    return jnp.where(rows >= tq, rows - tq, rows) >= cols


def _attn_finish(acc, l, lq1_ref, lk1_ref, lq2_ref, lk2_ref, onw_ref, o_ref, tq, lam_init):
    lam = (jnp.exp(jnp.sum(lq1_ref[...] * lk1_ref[...], axis=-1, keepdims=True))
           - jnp.exp(jnp.sum(lq2_ref[...] * lk2_ref[...], axis=-1, keepdims=True)) + lam_init)
    o = acc[:tq] / l[:tq] - lam * (acc[tq:] / l[tq:])
    ms = jnp.mean(o * o, axis=-1, keepdims=True)
    o = o * lax.rsqrt(ms + EPS) * onw_ref[...] * (1.0 - lam_init)
    o_ref[...] = o.astype(BF16)


def _attn_shift_body(shift_ref, q_ref, k_ref, v_ref, lq1_ref, lk1_ref, lq2_ref, lk2_ref, onw_ref, o_ref,
                     qz_ref, p_ref, l_ref, acc_ref, *, tq, lam_init, unroll, heads):
    i = pl.program_id(2)
    shift = shift_ref[0]
    cols = lambda hh: slice(hh * LANES, (hh + 1) * LANES)
    rows = lambda j: pl.ds(pl.multiple_of(j * tq, tq), tq)

    def probs(hh, j, diagonal):
        s = _dot_nt(qz_ref[hh], k_ref[rows(j), cols(hh)]) - shift
        if diagonal:
            s = jnp.where(_causal_mask(tq), s, NEG_BIG)
        p = jnp.exp2(s)
        psum = p[:, 0:LANES]
        for c0 in range(LANES, tq, LANES):
            psum = psum + p[:, c0:c0 + LANES]
        return p.astype(BF16), psum

    for hh in range(heads):
        _stack_maps(q_ref.at[:, cols(hh)], qz_ref.at[hh], tq)
        p, psum = probs(hh, i, True)
        p_ref[hh] = p
        l_ref[hh] = psum
        acc_ref[hh] = jnp.zeros((2 * tq, LANES), F32)

    def pv_prev(hh, j):
        return _dot(p_ref[hh], v_ref[rows(jnp.where(j == 0, i, j - 1)), cols(hh)])

    def steps(j0, n):
        acc = [pv_prev(hh, j0) for hh in range(heads)]
        psum = [None] * heads
        for u in range(n):
            for hh in range(heads):
                p, ps = probs(hh, j0 + u, False)
                psum[hh] = ps if psum[hh] is None else psum[hh] + ps
                if u + 1 < n:
                    acc[hh] = acc[hh] + _dot(p, v_ref[rows(j0 + u), cols(hh)])
                else:
                    p_ref[hh] = p
        for hh in range(heads):
            acc_ref[hh] += acc[hh]
            l_ref[hh] += psum[hh]

    start = 0
    n = unroll
    while n >= 1:
        trips = (i - start) // n

        def group(t, _, start=start, n=n):
            steps(start + t * n, n)
            return 0

        lax.fori_loop(0, trips, group, 0)
        start = start + trips * n
        n //= 2
    for hh in range(heads):
        acc = acc_ref[hh] + pv_prev(hh, i)
        l = jnp.sum(l_ref[hh], axis=-1, keepdims=True)
        _attn_finish(acc, l, lq1_ref, lk1_ref, lq2_ref, lk2_ref, onw_ref, o_ref.at[:, cols(hh)], tq, lam_init)


def _attn_online_body(shift_ref, q_ref, k_ref, v_ref, lq1_ref, lk1_ref, lq2_ref, lk2_ref, onw_ref, o_ref,
                      qz_ref, *, tq, lam_init):
    del shift_ref
    i = pl.program_id(2)
    _stack_maps(q_ref, qz_ref, tq)

    def step(j, carry, diagonal):
        m, l, acc = carry
        r0 = pl.multiple_of(j * tq, tq)
        s = _dot_nt(qz_ref[...], k_ref[pl.ds(r0, tq), :])
        if diagonal:
            s = jnp.where(_causal_mask(tq), s, NEG_BIG)
        m_new = jnp.maximum(m, jnp.max(s, axis=-1, keepdims=True))
        p = jnp.exp2(s - m_new)
        alpha = jnp.exp2(m - m_new)
        l = alpha * l + jnp.sum(p, axis=-1, keepdims=True)
        acc = alpha * acc + _dot(p.astype(BF16), v_ref[pl.ds(r0, tq), :])
        return m_new, l, acc

    init = (jnp.full((2 * tq, 1), NEG_BIG, F32), jnp.zeros((2 * tq, 1), F32),
            jnp.zeros((2 * tq, LANES), F32))
    carry = lax.fori_loop(0, i, lambda j, c: step(j, c, False), init)
    _, l, acc = step(i, carry, True)
    _attn_finish(acc, l, lq1_ref, lk1_ref, lq2_ref, lk2_ref, onw_ref, o_ref, tq, lam_init)


def _attention(shift, q, k, v, lq1, lk1, lq2, lk2, onw, B, S, tq, lam_init, online):
    nq = S // tq
    vec = pl.BlockSpec((1, DA_QK_DIM), lambda b, h, i: (0, 0))
    if online:
        heads = 1
        body = functools.partial(_attn_online_body, tq=tq, lam_init=lam_init)
        scratch = [pltpu.VMEM((2 * tq, LANES), BF16)]
    else:
        heads = DA_HEADS
        body = functools.partial(_attn_shift_body, tq=tq, lam_init=lam_init, unroll=4, heads=heads)
        scratch = [pltpu.VMEM((heads, 2 * tq, LANES), BF16),
                   pltpu.VMEM((heads, 2 * tq, tq), BF16),
                   pltpu.VMEM((heads, 2 * tq, LANES), F32),
                   pltpu.VMEM((heads, 2 * tq, LANES), F32)]
    width = heads * LANES
    return pl.pallas_call(
        body,
        grid=(B, DA_HEADS // heads, nq),
        in_specs=[
            pl.BlockSpec(memory_space=pltpu.SMEM),
            pl.BlockSpec((tq, width), lambda b, h, i: (b * nq + i, h)),
            pl.BlockSpec((S, width), lambda b, h, i: (b, h)),
            pl.BlockSpec((S, width), lambda b, h, i: (b, h)),
            vec, vec, vec, vec,
            pl.BlockSpec((1, LANES), lambda b, h, i: (0, 0)),
        ],
        out_specs=pl.BlockSpec((tq, width), lambda b, h, i: (b * nq + i, h)),
        out_shape=jax.ShapeDtypeStruct((B * S, HEAD_COLS), BF16),
        scratch_shapes=scratch,
        compiler_params=pltpu.CompilerParams(
            dimension_semantics=("arbitrary", "arbitrary", "arbitrary"), vmem_limit_bytes=VMEM_LIMIT),
        name="diff_attention_online" if online else "diff_attention",
    )(shift, q, k, v, lq1, lk1, lq2, lk2, onw)


def _merge_masks(row, col, n):
    masks = []
    b = 2
    while b < n:
        masks.append(((row // b) == (col // b) + 1) & ((row // (2 * b)) == (col // (2 * b))))
        b *= 2
    return masks


def _gdn_body(q_ref, k_ref, v_ref, z_ref, ab_ref, alog_ref, dtb_ref, onw_ref, o_ref,
              state_ref, l_ref, t_ref, rhs_ref, ou_ref, ku_ref, qa_ref, qk_ref, kdt_ref, egl_ref,
              fmask_ref, lmask_ref, *, tb, group):
    C = GDN_CHUNK
    nc = tb // C
    blk = pl.program_id(1)
    n_levels = lmask_ref.shape[0]

    @pl.when(blk == 0)
    def _():
        state_ref[...] = jnp.zeros_like(state_ref)
        row = lax.broadcasted_iota(jnp.int32, (C, C), 0)
        col = lax.broadcasted_iota(jnp.int32, (C, C), 1)
        fmask_ref[0] = (row >= col).astype(F32)
        fmask_ref[1] = (row > col).astype(F32)
        fmask_ref[2] = (row == col).astype(F32)
        fmask_ref[3] = ((row == col + 1) & (row % 2 == 1)).astype(F32)
        for lv, m in enumerate(_merge_masks(row, col, C)):
            lmask_ref[lv] = m.astype(BF16)

    ab = ab_ref[...]
    sp_in = ab + dtb_ref[...]
    softplus = jnp.maximum(sp_in, 0.0) + jnp.log(1.0 + jnp.exp(-jnp.abs(sp_in)))
    gtok = -jnp.exp(alog_ref[...]) * softplus
    gtok = jnp.where(lax.broadcasted_iota(jnp.int32, (tb, LANES), 1) < DN_HEADS, gtok, 0.0)
    beta_all = 1.0 / (1.0 + jnp.exp(-ab))

    tri = fmask_ref[0].astype(BF16)

    for c in range(nc):
        r0 = c * C
        g1, g2, g3 = _split3(gtok[r0:r0 + C])
        gc = _dot(tri, g1) + _dot(tri, g2) + _dot(tri, g3)
        gct = gc.T
        for h in range(DN_HEADS):
            un = c * DN_HEADS + h
            c0 = h * DN_DIM
            qh = q_ref[r0:r0 + C, c0:c0 + DN_DIM].astype(F32)
            kh_bf = k_ref[r0:r0 + C, c0:c0 + DN_DIM]
            kh = kh_bf.astype(F32)
            vh = v_ref[r0:r0 + C, c0:c0 + DN_DIM].astype(F32)
            gcol = gc[:, h:h + 1]
            grow = gct[h:h + 1, :]
            beta = beta_all[r0:r0 + C, DN_HEADS + h:DN_HEADS + h + 1]
            decay = jnp.exp(jnp.minimum(gcol - grow, 0.0))
            kb = kh * beta
            a = _dot_nt(jnp.concatenate([kb, qh], axis=0).astype(BF16), kh_bf)
            Lm = a[:C] * decay * fmask_ref[1]
            l_ref[un] = Lm.astype(BF16)
            t_ref[un] = (fmask_ref[2] - Lm * fmask_ref[3]).astype(BF16)
            qk_ref[un] = (a[C:] * decay * fmask_ref[0]).astype(BF16)
            eg = jnp.exp(gcol)
            rhs_ref[un] = jnp.concatenate([vh * beta, kb * eg], axis=1).astype(BF16)
            qa_ref[un, 0:C, :] = (qh * eg).astype(BF16)
            g_last = gcol[C - 1:C, :]
            kdt_ref[un] = (kh * jnp.exp(g_last - gcol)).T.astype(BF16)
            egl_ref[un] = jnp.broadcast_to(jnp.exp(g_last), (8, LANES))

    nu = nc * DN_HEADS
    for lv in range(n_levels):
        for g0 in range(0, nu, group):
            units = list(range(g0, min(g0 + group, nu)))
            ps = [_dot(l_ref[un] * lmask_ref[lv], t_ref[un]).astype(BF16) for un in units]
            for un, p in zip(units, ps):
                t_ref[un] = t_ref[un] - _dot(t_ref[un], p).astype(BF16)
    for un in range(nu):
        uw = _dot(t_ref[un], rhs_ref[un]).astype(BF16)
        r = _dot(jnp.concatenate([qk_ref[un], kdt_ref[un]], axis=0), uw)
        ou_ref[un] = r[:C, :DN_DIM]
        qa_ref[un, 0:C, :] = (qa_ref[un, 0:C, :].astype(F32) - r[:C, DN_DIM:]).astype(BF16)
        qa_ref[un, C:C + DN_DIM, :] = (-r[C:, DN_DIM:]).astype(BF16)
        ku_ref[un] = r[C:, :DN_DIM]

    for c in range(nc):
        r0 = c * C
        for h in range(DN_HEADS):
            un = c * DN_HEADS + h
            c0 = h * DN_DIM
            state = state_ref[h]
            r = _dot(qa_ref[un], state.astype(BF16))
            o = ou_ref[un] + r[:C]
            state_ref[h] = state * egl_ref[un][0:1, :] + r[C:] + ku_ref[un]
            zh = z_ref[r0:r0 + C, c0:c0 + DN_DIM]
            o = o * lax.rsqrt(jnp.mean(o * o, axis=-1, keepdims=True) + EPS) * onw_ref[...]
            o = o * _silu(zh)
            o_ref[r0:r0 + C, c0:c0 + DN_DIM] = o.astype(BF16)


def _gdn(gq, gk, gv, z, ab, alog, dtb, onw, B, S, tb):
    nb = S // tb
    C = GDN_CHUNK
    nu = (tb // C) * DN_HEADS
    const = lambda b, i: (0, 0)
    row = lambda b, i: (b * nb + i, 0)
    return pl.pallas_call(
        functools.partial(_gdn_body, tb=tb, group=8),
        grid=(B, nb),
        in_specs=[
            pl.BlockSpec((tb, HEAD_COLS), row),
            pl.BlockSpec((tb, HEAD_COLS), row),
            pl.BlockSpec((tb, HEAD_COLS), row),
            pl.BlockSpec((tb, HEAD_COLS), row),
            pl.BlockSpec((tb, LANES), row),
            pl.BlockSpec((1, LANES), const),
            pl.BlockSpec((1, LANES), const),
            pl.BlockSpec((1, LANES), const),
        ],
        out_specs=pl.BlockSpec((tb, HEAD_COLS), row),
        out_shape=jax.ShapeDtypeStruct((B * S, HEAD_COLS), BF16),
        scratch_shapes=[
            pltpu.VMEM((DN_HEADS, DN_DIM, DN_DIM), F32),
            pltpu.VMEM((nu, C, C), BF16),
            pltpu.VMEM((nu, C, C), BF16),
            pltpu.VMEM((nu, C, 2 * DN_DIM), BF16),
            pltpu.VMEM((nu, C, DN_DIM), F32),
            pltpu.VMEM((nu, DN_DIM, DN_DIM), F32),
            pltpu.VMEM((nu, C + DN_DIM, DN_DIM), BF16),
            pltpu.VMEM((nu, C, C), BF16),
            pltpu.VMEM((nu, DN_DIM, C), BF16),
            pltpu.VMEM((nu, 8, LANES), F32),
            pltpu.VMEM((4, C, C), F32),
            pltpu.VMEM((C.bit_length() - 2, C, C), BF16),
        ],
        compiler_params=pltpu.CompilerParams(
            dimension_semantics=("arbitrary", "arbitrary"), vmem_limit_bytes=VMEM_LIMIT),
        name="gated_deltanet",
    )(gq, gk, gv, z, ab, alog, dtb, onw)


def _mlp_body(x_ref, ma_ref, mb_ref, woa_ref, wob_ref, n2_ref, wup_ref, wdn_ref, o_ref, *, ff_chunk):
    x1 = x_ref[...] + _dot(ma_ref[...], woa_ref[...]) + _dot(mb_ref[...], wob_ref[...])
    ms = jnp.mean(x1 * x1, axis=-1, keepdims=True)
    h = (x1 * lax.rsqrt(ms + EPS) * n2_ref[...]).astype(BF16)
    mlp = None
    for c0 in range(0, wup_ref.shape[1], ff_chunk):
        up = jnp.maximum(_dot(h, wup_ref[:, c0:c0 + ff_chunk]), 0.0)
        down = _dot((up * up).astype(BF16), wdn_ref[c0:c0 + ff_chunk, :])
        mlp = down if mlp is None else mlp + down
    o_ref[...] = x1 + mlp


def _mlp(x2, mix_a, mix_b, woa, wob, n2, wup, wdn, tm, ff_chunk):
    T, D = x2.shape
    const = lambda i: (0, 0)
    row = lambda i: (i, 0)
    return pl.pallas_call(
        functools.partial(_mlp_body, ff_chunk=ff_chunk),
        grid=(T // tm,),
        in_specs=[
            pl.BlockSpec((tm, D), row),
            pl.BlockSpec((tm, HEAD_COLS), row),
            pl.BlockSpec((tm, HEAD_COLS), row),
            pl.BlockSpec(woa.shape, const),
            pl.BlockSpec(wob.shape, const),
            pl.BlockSpec((1, D), const),
            pl.BlockSpec(wup.shape, const),
            pl.BlockSpec(wdn.shape, const),
        ],
        out_specs=pl.BlockSpec((tm, D), row),
        out_shape=jax.ShapeDtypeStruct((T, D), F32),
        compiler_params=pltpu.CompilerParams(
            dimension_semantics=("arbitrary",), vmem_limit_bytes=VMEM_LIMIT),
        name="outproj_mlp",
    )(x2, mix_a, mix_b, woa, wob, n2, wup, wdn)


def _layer(x2, B, S, l, norm1_w, w_in, lambda_q1, lambda_k1, lambda_q2, lambda_k2, q_norm_w, k_norm_w,
           da_out_norm_w, conv_w, A_log, dt_bias, dn_out_norm_w, w_out, norm2_w, w_up, w_down,
           tm, tq, tb, ff_chunk):
    D = x2.shape[1]
    qkv_a = 3 * HEAD_COLS
    dn_cols = 3 * HEAD_COLS
    s4 = qkv_a + dn_cols
    s5 = s4 + HEAD_COLS
    w = w_in[l].astype(BF16)
    wa, wdn, wz = w[:, :qkv_a], w[:, qkv_a:s4], w[:, s4:s5]
    wab = jnp.pad(w[:, s5:], ((0, 0), (0, LANES - 2 * DN_HEADS)))
    tile2 = lambda v: jnp.concatenate([v, v]).reshape(1, LANES).astype(F32)
    pad_lanes = lambda v: jnp.pad(v.astype(F32), (0, LANES - v.shape[0])).reshape(1, LANES)

    q, k, v, gq, gk, gv, z, ab = _inproj(
        x2, norm1_w[l].reshape(1, D).astype(F32), wa, wdn, wz, wab,
        tile2(q_norm_w[l]), tile2(k_norm_w[l]), 0.5 * conv_w[l].astype(F32), S, tm)

    lam_init = 0.8 - 0.6 * math.exp(-0.3 * l)
    vec = lambda p: p[l].reshape(1, DA_QK_DIM).astype(F32)
    bound = 8.0 * jnp.max(jnp.abs(q_norm_w[l].astype(F32))) * jnp.max(jnp.abs(k_norm_w[l].astype(F32)))
    attn_args = ((bound * LOG2E).reshape(1), q, k, v, vec(lambda_q1), vec(lambda_k1), vec(lambda_q2),
                 vec(lambda_k2), da_out_norm_w[l].reshape(1, LANES).astype(F32))
    mix_a = lax.cond(bound <= MAX_FIXED_SHIFT,
                     lambda a: _attention(*a, B, S, tq, lam_init, online=False),
                     lambda a: _attention(*a, B, S, tq, lam_init, online=True), attn_args)

    mix_b = _gdn(gq, gk, gv, z, ab, pad_lanes(A_log[l]), pad_lanes(dt_bias[l]),
                 dn_out_norm_w[l].reshape(1, LANES).astype(F32), B, S, tb)

    wo = w_out[l].astype(BF16)
    return _mlp(x2, mix_a, mix_b, wo[:HEAD_COLS], wo[HEAD_COLS:], norm2_w[l].reshape(1, D).astype(F32),
                w_up[l].astype(BF16), w_down[l].astype(BF16), tm, ff_chunk)


def kernel(x, norm1_w, w_in, lambda_q1, lambda_k1, lambda_q2, lambda_k2, q_norm_w, k_norm_w, da_out_norm_w,
           conv_w, A_log, dt_bias, dn_out_norm_w, w_out, norm2_w, w_up, w_down):
    B, S, D = x.shape
    x2 = x.reshape(B * S, D)
    tm = min(512, S)
    tq = min(512, S)
    tb = min(512, S)
    for l in range(w_in.shape[0]):
        x2 = _layer(x2, B, S, l, norm1_w, w_in, lambda_q1, lambda_k1, lambda_q2, lambda_k2, q_norm_w,
                    k_norm_w, da_out_norm_w, conv_w, A_log, dt_bias, dn_out_norm_w, w_out, norm2_w,
                    w_up, w_down, tm, tq, tb, 1024)
    return x2.reshape(B, S, D)
```

```python
import functools
import math

import jax
import jax.numpy as jnp
from jax import lax
from jax.experimental import pallas as pl
from jax.experimental.pallas import tpu as pltpu

F32 = jnp.float32
BF16 = jnp.bfloat16
EPS = 1e-6
LANES = 128
NEG_BIG = -1e30
LOG2E = math.log2(math.e)
MAX_FIXED_SHIFT = 40.0

DA_HEADS = 4
DA_QK_DIM = 64
DN_HEADS = 4
DN_DIM = 128
CONV_WIDTH = 4
HEAD_COLS = 512
GDN_CHUNK = 128
VMEM_LIMIT = 56 * 1024 * 1024


def _dot(a, b):
    return jnp.dot(a, b, preferred_element_type=F32)


def _dot_nt(a, b):
    return lax.dot_general(a, b, (((1,), (1,)), ((), ())), preferred_element_type=F32)


def _dot_tn(a, b):
    return lax.dot_general(a, b, (((0,), (0,)), ((), ())), preferred_element_type=F32)


def _silu(x):
    h = 0.5 * x
    return h + h * jnp.tanh(h)


def _split3(x):
    x1 = x.astype(BF16)
    r1 = x - x1.astype(F32)
    x2 = r1.astype(BF16)
    x3 = (r1 - x2.astype(F32)).astype(BF16)
    return x1, x2, x3


def _inproj_body(x_ref, n1_ref, w_ref, qnw_ref, knw_ref, cw_ref,
                 q_out, k_out, v_out, gq_out, gk_out, gv_out, z_out, ab_out, tail_ref, wb_ref,
                 *, tiles_per_seq):
    tm = x_ref.shape[0]
    piece = 2 * LANES
    wa0, wdn0, wz0, wab0 = 0, 3 * HEAD_COLS, 6 * HEAD_COLS, 7 * HEAD_COLS
    n_in = w_ref.shape[1]

    @pl.when(pl.program_id(0) == 0)
    def _():
        for c0 in range(0, wab0, HEAD_COLS):
            wb_ref[:, c0:c0 + HEAD_COLS] = w_ref[:, c0:c0 + HEAD_COLS].astype(BF16)
        wb_ref[:, wab0:wab0 + LANES] = jnp.zeros((wb_ref.shape[0], LANES), BF16)
        wb_ref[:, wab0:n_in] = w_ref[:, wab0:n_in].astype(BF16)

    x = x_ref[...]
    ms = jnp.mean(x * x, axis=-1, keepdims=True)
    h = (x * lax.rsqrt(ms + EPS) * n1_ref[...]).astype(BF16)

    @pl.when(pl.program_id(0) % tiles_per_seq == 0)
    def _():
        tail_ref[...] = jnp.zeros_like(tail_ref)

    def deltanet_piece(p0):
        d = _dot(h, wb_ref[:, wdn0 + p0:wdn0 + p0 + piece])
        tail = tail_ref[:, p0:p0 + piece]
        tail_ref[:, p0:p0 + piece] = d[tm - 8:tm, :]
        for cc in range(0, piece, LANES):
            c0 = p0 + cc
            cb = c0 // LANES
            for r0 in range(0, tm, GDN_CHUNK):
                halo = tail[:, cc:cc + LANES] if r0 == 0 else d[r0 - 8:r0, cc:cc + LANES]
                a = jnp.concatenate([halo, d[r0:r0 + GDN_CHUNK, cc:cc + LANES]], axis=0)
                y = cw_ref[CONV_WIDTH - 1:CONV_WIDTH, c0:c0 + LANES] * a[8:]
                for w in range(CONV_WIDTH - 1):
                    shifted = pltpu.roll(a, CONV_WIDTH - 1 - w, axis=0)[8:]
                    y = y + cw_ref[w:w + 1, c0:c0 + LANES] * shifted
                y = y + y * jnp.tanh(y)
                if cb < DN_HEADS:
                    y = y * (lax.rsqrt(jnp.sum(y * y, axis=-1, keepdims=True) + EPS) * (DN_DIM ** -0.5))
                elif cb < 2 * DN_HEADS:
                    y = y * lax.rsqrt(jnp.sum(y * y, axis=-1, keepdims=True) + EPS)
                out = (gq_out, gk_out, gv_out)[cb // DN_HEADS]
                c1 = (cb % DN_HEADS) * LANES
                out[r0:r0 + GDN_CHUNK, c1:c1 + LANES] = y.astype(BF16)

    lo = lax.broadcasted_iota(jnp.int32, (tm, LANES), 1) < DA_QK_DIM

    def qk_norm(t, w):
        sq = t * t
        s_lo = jnp.sum(jnp.where(lo, sq, 0.0), axis=-1, keepdims=True)
        s_hi = jnp.sum(jnp.where(lo, 0.0, sq), axis=-1, keepdims=True)
        ms_ = jnp.where(lo, s_lo, s_hi) * (1.0 / DA_QK_DIM)
        return t * lax.rsqrt(ms_ + EPS) * w

    def attention_piece(p0):
        t = _dot(h, wb_ref[:, wa0 + p0:wa0 + p0 + piece])
        for cc in range(0, piece, LANES):
            th = t[:, cc:cc + LANES]
            c1 = p0 % HEAD_COLS + cc
            if p0 < HEAD_COLS:
                q_out[:, c1:c1 + LANES] = (qk_norm(th, qnw_ref[...]) * (DA_QK_DIM ** -0.5 * LOG2E)).astype(BF16)
            elif p0 < 2 * HEAD_COLS:
                k_out[:, c1:c1 + LANES] = qk_norm(th, knw_ref[...]).astype(BF16)
            else:
                v_out[:, c1:c1 + LANES] = th.astype(BF16)

    for p0 in range(0, 3 * HEAD_COLS, piece):
        deltanet_piece(p0)
        attention_piece(p0)
    for p0 in range(0, HEAD_COLS, piece):
        z_out[:, p0:p0 + piece] = _dot(h, wb_ref[:, wz0 + p0:wz0 + p0 + piece])
    ab_out[...] = _dot(h, wb_ref[:, wab0:wab0 + LANES])


def _inproj(x2, n1, w_in, l, qnw, knw, cw, S, tm):
    T, D = x2.shape
    W = cw.shape[1]
    n_in = w_in.shape[2]
    const = lambda i: (0, 0)
    row = lambda i: (i, 0)
    slab = pl.BlockSpec((tm, HEAD_COLS), row)
    slab_bf16 = jax.ShapeDtypeStruct((T, HEAD_COLS), BF16)
    return pl.pallas_call(
        functools.partial(_inproj_body, tiles_per_seq=S // tm),
        grid=(T // tm,),
        in_specs=[
            pl.BlockSpec((tm, D), row),
            pl.BlockSpec((1, D), const),
            pl.BlockSpec((None, D, n_in), lambda i: (l, 0, 0)),
            pl.BlockSpec((1, LANES), const),
            pl.BlockSpec((1, LANES), const),
            pl.BlockSpec((CONV_WIDTH, W), const),
        ],
        out_specs=[slab, slab, slab, slab, slab, slab, slab, pl.BlockSpec((tm, LANES), row)],
        out_shape=[slab_bf16, slab_bf16, slab_bf16, slab_bf16, slab_bf16, slab_bf16,
                   jax.ShapeDtypeStruct((T, HEAD_COLS), F32),
                   jax.ShapeDtypeStruct((T, LANES), F32)],
        scratch_shapes=[
            pltpu.VMEM((8, W), F32),
            pltpu.VMEM((D, 7 * HEAD_COLS + LANES), BF16),
        ],
        compiler_params=pltpu.CompilerParams(
            dimension_semantics=("arbitrary",), vmem_limit_bytes=VMEM_LIMIT),
        name="inproj",
    )(x2, n1, w_in, qnw, knw, cw)


def _stack_maps(q_ref, qz_ref, tq):
    q = q_ref[...]
    lane = lax.broadcasted_iota(jnp.int32, (tq, LANES), 1)
    zero = jnp.zeros_like(q)
    qz_ref[0:tq, :] = jnp.where(lane < DA_QK_DIM, q, zero)
    qz_ref[tq:2 * tq, :] = jnp.where(lane < DA_QK_DIM, zero, q)


def _causal_mask(tq):
    rows = lax.broadcasted_iota(jnp.int32, (2 * tq, tq), 0)
    cols = lax.broadcasted_iota(jnp.int32, (2 * tq, tq), 1)
    return jnp.where(rows >= tq, rows - tq, rows) >= cols


def _attn_finish(acc, l, lq1_ref, lk1_ref, lq2_ref, lk2_ref, onw_ref, o_ref, tq, lam_init):
    lam = (jnp.exp(jnp.sum(lq1_ref[...] * lk1_ref[...], axis=-1, keepdims=True))
           - jnp.exp(jnp.sum(lq2_ref[...] * lk2_ref[...], axis=-1, keepdims=True)) + lam_init)
    o = acc[:tq] / l[:tq] - lam * (acc[tq:] / l[tq:])
    ms = jnp.mean(o * o, axis=-1, keepdims=True)
    o = o * lax.rsqrt(ms + EPS) * onw_ref[...] * (1.0 - lam_init)
    o_ref[...] = o.astype(BF16)


def _attn_shift_body(shift_ref, q_ref, k_ref, v_ref, lq1_ref, lk1_ref, lq2_ref, lk2_ref, onw_ref, o_ref,
                     qz_ref, p_ref, l_ref, acc_ref, *, tq, lam_init, unroll, heads):
    i = pl.program_id(2)
    shift = shift_ref[0]
    cols = lambda hh: slice(hh * LANES, (hh + 1) * LANES)
    rows = lambda j: pl.ds(pl.multiple_of(j * tq, tq), tq)

    def probs(hh, j, diagonal):
        s = _dot_nt(qz_ref[hh], k_ref[rows(j), cols(hh)]) - shift
        if diagonal:
            s = jnp.where(_causal_mask(tq), s, NEG_BIG)
        p = jnp.exp2(s)
        psum = p[:, 0:LANES]
        for c0 in range(LANES, tq, LANES):
            psum = psum + p[:, c0:c0 + LANES]
        return p.astype(BF16), psum

    for hh in range(heads):
        _stack_maps(q_ref.at[:, cols(hh)], qz_ref.at[hh], tq)
        p, psum = probs(hh, i, True)
        p_ref[hh] = p
        l_ref[hh] = psum
        acc_ref[hh] = jnp.zeros((2 * tq, LANES), F32)

    def pv_prev(hh, j):
        return _dot(p_ref[hh], v_ref[rows(jnp.where(j == 0, i, j - 1)), cols(hh)])

    def steps(j0, n):
        acc = [pv_prev(hh, j0) for hh in range(heads)]
        psum = [None] * heads
        for u in range(n):
            for hh in range(heads):
                p, ps = probs(hh, j0 + u, False)
                psum[hh] = ps if psum[hh] is None else psum[hh] + ps
                if u + 1 < n:
                    acc[hh] = acc[hh] + _dot(p, v_ref[rows(j0 + u), cols(hh)])
                else:
                    p_ref[hh] = p
        for hh in range(heads):
            acc_ref[hh] += acc[hh]
            l_ref[hh] += psum[hh]

    start = 0
    n = unroll
    while n >= 1:
        trips = (i - start) // n

        def group(t, _, start=start, n=n):
            steps(start + t * n, n)
            return 0

        lax.fori_loop(0, trips, group, 0)
        start = start + trips * n
        n //= 2
    for hh in range(heads):
        acc = acc_ref[hh] + pv_prev(hh, i)
        l = jnp.sum(l_ref[hh], axis=-1, keepdims=True)
        _attn_finish(acc, l, lq1_ref, lk1_ref, lq2_ref, lk2_ref, onw_ref, o_ref.at[:, cols(hh)], tq, lam_init)


def _attn_online_body(shift_ref, q_ref, k_ref, v_ref, lq1_ref, lk1_ref, lq2_ref, lk2_ref, onw_ref, o_ref,
                      qz_ref, *, tq, lam_init):
    del shift_ref
    i = pl.program_id(2)
    _stack_maps(q_ref, qz_ref, tq)

    def step(j, carry, diagonal):
        m, l, acc = carry
        r0 = pl.multiple_of(j * tq, tq)
        s = _dot_nt(qz_ref[...], k_ref[pl.ds(r0, tq), :])
        if diagonal:
            s = jnp.where(_causal_mask(tq), s, NEG_BIG)
        m_new = jnp.maximum(m, jnp.max(s, axis=-1, keepdims=True))
        p = jnp.exp2(s - m_new)
        alpha = jnp.exp2(m - m_new)
        l = alpha * l + jnp.sum(p, axis=-1, keepdims=True)
        acc = alpha * acc + _dot(p.astype(BF16), v_ref[pl.ds(r0, tq), :])
        return m_new, l, acc

    init = (jnp.full((2 * tq, 1), NEG_BIG, F32), jnp.zeros((2 * tq, 1), F32),
            jnp.zeros((2 * tq, LANES), F32))
    carry = lax.fori_loop(0, i, lambda j, c: step(j, c, False), init)
    _, l, acc = step(i, carry, True)
    _attn_finish(acc, l, lq1_ref, lk1_ref, lq2_ref, lk2_ref, onw_ref, o_ref, tq, lam_init)


def _attention(shift, q, k, v, lq1, lk1, lq2, lk2, onw, B, S, tq, lam_init, online):
    nq = S // tq
    vec = pl.BlockSpec((1, DA_QK_DIM), lambda b, h, i: (0, 0))
    if online:
        heads = 1
        body = functools.partial(_attn_online_body, tq=tq, lam_init=lam_init)
        scratch = [pltpu.VMEM((2 * tq, LANES), BF16)]
    else:
        heads = DA_HEADS
        body = functools.partial(_attn_shift_body, tq=tq, lam_init=lam_init, unroll=4, heads=heads)
        scratch = [pltpu.VMEM((heads, 2 * tq, LANES), BF16),
                   pltpu.VMEM((heads, 2 * tq, tq), BF16),
                   pltpu.VMEM((heads, 2 * tq, LANES), F32),
                   pltpu.VMEM((heads, 2 * tq, LANES), F32)]
    width = heads * LANES
    return pl.pallas_call(
        body,
        grid=(B, DA_HEADS // heads, nq),
        in_specs=[
            pl.BlockSpec(memory_space=pltpu.SMEM),
            pl.BlockSpec((tq, width), lambda b, h, i: (b * nq + i, h)),
            pl.BlockSpec((S, width), lambda b, h, i: (b, h)),
            pl.BlockSpec((S, width), lambda b, h, i: (b, h)),
            vec, vec, vec, vec,
            pl.BlockSpec((1, LANES), lambda b, h, i: (0, 0)),
        ],
        out_specs=pl.BlockSpec((tq, width), lambda b, h, i: (b * nq + i, h)),
        out_shape=jax.ShapeDtypeStruct((B * S, HEAD_COLS), BF16),
        scratch_shapes=scratch,
        compiler_params=pltpu.CompilerParams(
            dimension_semantics=("arbitrary", "arbitrary", "arbitrary"), vmem_limit_bytes=VMEM_LIMIT),
        name="diff_attention_online" if online else "diff_attention",
    )(shift, q, k, v, lq1, lk1, lq2, lk2, onw)


def _merge_masks(row, col, n):
    masks = []
    b = 2
    while b < n:
        masks.append(((row // b) == (col // b) + 1) & ((row // (2 * b)) == (col // (2 * b))))
        b *= 2
    return masks


def _gdn_body(q_ref, k_ref, v_ref, z_ref, ab_ref, alog_ref, dtb_ref, onw_ref, o_ref,
              state_ref, l_ref, t_ref, rhs_ref, ou_ref, ku_ref, qa_ref, qk_ref, kdt_ref, egl_ref,
              fmask_ref, lmask_ref, *, tb, group):
    C = GDN_CHUNK
    nc = tb // C
    blk = pl.program_id(1)
    n_levels = lmask_ref.shape[0]

    @pl.when(blk == 0)
    def _():
        state_ref[...] = jnp.zeros_like(state_ref)
        row = lax.broadcasted_iota(jnp.int32, (C, C), 0)
        col = lax.broadcasted_iota(jnp.int32, (C, C), 1)
        fmask_ref[0] = (row >= col).astype(F32)
        fmask_ref[1] = (row > col).astype(F32)
        fmask_ref[2] = (row == col).astype(F32)
        fmask_ref[3] = ((row == col + 1) & (row % 2 == 1)).astype(F32)
        for lv, m in enumerate(_merge_masks(row, col, C)):
            lmask_ref[lv] = m.astype(BF16)

    ab = ab_ref[...]
    sp_in = ab + dtb_ref[...]
    softplus = jnp.maximum(sp_in, 0.0) + jnp.log(1.0 + jnp.exp(-jnp.abs(sp_in)))
    gtok = -jnp.exp(alog_ref[...]) * softplus
    gtok = jnp.where(lax.broadcasted_iota(jnp.int32, (tb, LANES), 1) < DN_HEADS, gtok, 0.0)
    beta_all = 1.0 / (1.0 + jnp.exp(-ab))

    tri = fmask_ref[0].astype(BF16)

    for c in range(nc):
        r0 = c * C
        g1, g2, g3 = _split3(gtok[r0:r0 + C])
        gc = _dot(tri, g1) + _dot(tri, g2) + _dot(tri, g3)
        gct = gc.T
        for h in range(DN_HEADS):
            un = c * DN_HEADS + h
            c0 = h * DN_DIM
            qh = q_ref[r0:r0 + C, c0:c0 + DN_DIM].astype(F32)
            kh_bf = k_ref[r0:r0 + C, c0:c0 + DN_DIM]
            kh = kh_bf.astype(F32)
            vh = v_ref[r0:r0 + C, c0:c0 + DN_DIM].astype(F32)
            gcol = gc[:, h:h + 1]
            grow = gct[h:h + 1, :]
            beta = beta_all[r0:r0 + C, DN_HEADS + h:DN_HEADS + h + 1]
            decay = jnp.exp(jnp.minimum(gcol - grow, 0.0))
            kb = kh * beta
            a = _dot_nt(jnp.concatenate([kb, qh], axis=0).astype(BF16), kh_bf)
            Lm = a[:C] * decay * fmask_ref[1]
            l_ref[un] = Lm.astype(BF16)
            t_ref[un] = (fmask_ref[2] - Lm * fmask_ref[3]).astype(BF16)
            qk_ref[un] = (a[C:] * decay * fmask_ref[0]).astype(BF16)
            eg = jnp.exp(gcol)
            rhs_ref[un] = jnp.concatenate([vh * beta, kb * eg], axis=1).astype(BF16)
            qa_ref[un, 0:C, :] = (qh * eg).astype(BF16)
            g_last = gcol[C - 1:C, :]
            kdt_ref[un] = (kh * jnp.exp(g_last - gcol)).T.astype(BF16)
            egl_ref[un] = jnp.broadcast_to(jnp.exp(g_last), (8, LANES))

    nu = nc * DN_HEADS
    for lv in range(n_levels):
        for g0 in range(0, nu, group):
            units = list(range(g0, min(g0 + group, nu)))
            ps = [_dot(l_ref[un] * lmask_ref[lv], t_ref[un]).astype(BF16) for un in units]
            for un, p in zip(units, ps):
                t_ref[un] = t_ref[un] - _dot(t_ref[un], p).astype(BF16)
    for un in range(nu):
        uw = _dot(t_ref[un], rhs_ref[un]).astype(BF16)
        r = _dot(jnp.concatenate([qk_ref[un], kdt_ref[un]], axis=0), uw)
        ou_ref[un] = r[:C, :DN_DIM]
        qa_ref[un, 0:C, :] = (qa_ref[un, 0:C, :].astype(F32) - r[:C, DN_DIM:]).astype(BF16)
        qa_ref[un, C:C + DN_DIM, :] = (-r[C:, DN_DIM:]).astype(BF16)
        ku_ref[un] = r[C:, :DN_DIM]

    for c in range(nc):
        r0 = c * C
        for h in range(DN_HEADS):
            un = c * DN_HEADS + h
            c0 = h * DN_DIM
            state = state_ref[h]
            r = _dot(qa_ref[un], state.astype(BF16))
            o = ou_ref[un] + r[:C]
            state_ref[h] = state * egl_ref[un][0:1, :] + r[C:] + ku_ref[un]
            zh = z_ref[r0:r0 + C, c0:c0 + DN_DIM]
            o = o * lax.rsqrt(jnp.mean(o * o, axis=-1, keepdims=True) + EPS) * onw_ref[...]
            o = o * _silu(zh)
            o_ref[r0:r0 + C, c0:c0 + DN_DIM] = o.astype(BF16)


def _gdn(gq, gk, gv, z, ab, alog, dtb, onw, B, S, tb):
    nb = S // tb
    C = GDN_CHUNK
    nu = (tb // C) * DN_HEADS
    const = lambda b, i: (0, 0)
    row = lambda b, i: (b * nb + i, 0)
    return pl.pallas_call(
        functools.partial(_gdn_body, tb=tb, group=8),
        grid=(B, nb),
        in_specs=[
            pl.BlockSpec((tb, HEAD_COLS), row),
            pl.BlockSpec((tb, HEAD_COLS), row),
            pl.BlockSpec((tb, HEAD_COLS), row),
            pl.BlockSpec((tb, HEAD_COLS), row),
            pl.BlockSpec((tb, LANES), row),
            pl.BlockSpec((1, LANES), const),
            pl.BlockSpec((1, LANES), const),
            pl.BlockSpec((1, LANES), const),
        ],
        out_specs=pl.BlockSpec((tb, HEAD_COLS), row),
        out_shape=jax.ShapeDtypeStruct((B * S, HEAD_COLS), BF16),
        scratch_shapes=[
            pltpu.VMEM((DN_HEADS, DN_DIM, DN_DIM), F32),
            pltpu.VMEM((nu, C, C), BF16),
            pltpu.VMEM((nu, C, C), BF16),
            pltpu.VMEM((nu, C, 2 * DN_DIM), BF16),
            pltpu.VMEM((nu, C, DN_DIM), F32),
            pltpu.VMEM((nu, DN_DIM, DN_DIM), F32),
            pltpu.VMEM((nu, C + DN_DIM, DN_DIM), BF16),
            pltpu.VMEM((nu, C, C), BF16),
            pltpu.VMEM((nu, DN_DIM, C), BF16),
            pltpu.VMEM((nu, 8, LANES), F32),
            pltpu.VMEM((4, C, C), F32),
            pltpu.VMEM((C.bit_length() - 2, C, C), BF16),
        ],
        compiler_params=pltpu.CompilerParams(
            dimension_semantics=("arbitrary", "arbitrary"), vmem_limit_bytes=VMEM_LIMIT),
        name="gated_deltanet",
    )(gq, gk, gv, z, ab, alog, dtb, onw)


def _mlp_body(x_ref, ma_ref, mb_ref, woa_ref, wob_ref, n2_ref, wup_ref, wdn_ref, o_ref, *, ff_chunk):
    x1 = x_ref[...] + _dot(ma_ref[...], woa_ref[...]) + _dot(mb_ref[...], wob_ref[...])
    ms = jnp.mean(x1 * x1, axis=-1, keepdims=True)
    h = (x1 * lax.rsqrt(ms + EPS) * n2_ref[...]).astype(BF16)
    mlp = None
    for c0 in range(0, wup_ref.shape[1], ff_chunk):
        up = jnp.maximum(_dot(h, wup_ref[:, c0:c0 + ff_chunk]), 0.0)
        down = _dot((up * up).astype(BF16), wdn_ref[c0:c0 + ff_chunk, :])
        mlp = down if mlp is None else mlp + down
    o_ref[...] = x1 + mlp


def _mlp(x2, mix_a, mix_b, woa, wob, n2, wup, wdn, tm, ff_chunk):
    T, D = x2.shape
    const = lambda i: (0, 0)
    row = lambda i: (i, 0)
    return pl.pallas_call(
        functools.partial(_mlp_body, ff_chunk=ff_chunk),
        grid=(T // tm,),
        in_specs=[
            pl.BlockSpec((tm, D), row),
            pl.BlockSpec((tm, HEAD_COLS), row),
            pl.BlockSpec((tm, HEAD_COLS), row),
            pl.BlockSpec(woa.shape, const),
            pl.BlockSpec(wob.shape, const),
            pl.BlockSpec((1, D), const),
            pl.BlockSpec(wup.shape, const),
            pl.BlockSpec(wdn.shape, const),
        ],
        out_specs=pl.BlockSpec((tm, D), row),
        out_shape=jax.ShapeDtypeStruct((T, D), F32),
        compiler_params=pltpu.CompilerParams(
            dimension_semantics=("arbitrary",), vmem_limit_bytes=VMEM_LIMIT),
        name="outproj_mlp",
    )(x2, mix_a, mix_b, woa, wob, n2, wup, wdn)


def _layer(x2, B, S, l, norm1_w, w_in, lambda_q1, lambda_k1, lambda_q2, lambda_k2, q_norm_w, k_norm_w,
           da_out_norm_w, conv_w, A_log, dt_bias, dn_out_norm_w, w_out, norm2_w, w_up, w_down,
           tm, tq, tb, ff_chunk):
    D = x2.shape[1]
    tile2 = lambda v: jnp.concatenate([v, v]).reshape(1, LANES).astype(F32)
    pad_lanes = lambda v: jnp.pad(v.astype(F32), (0, LANES - v.shape[0])).reshape(1, LANES)

    q, k, v, gq, gk, gv, z, ab = _inproj(
        x2, norm1_w[l].reshape(1, D).astype(F32), w_in.astype(F32), l,
        tile2(q_norm_w[l]), tile2(k_norm_w[l]), 0.5 * conv_w[l].astype(F32), S, tm)

    lam_init = 0.8 - 0.6 * math.exp(-0.3 * l)
    vec = lambda p: p[l].reshape(1, DA_QK_DIM).astype(F32)
    bound = 8.0 * jnp.max(jnp.abs(q_norm_w[l].astype(F32))) * jnp.max(jnp.abs(k_norm_w[l].astype(F32)))
    attn_args = ((bound * LOG2E).reshape(1), q, k, v, vec(lambda_q1), vec(lambda_k1), vec(lambda_q2),
                 vec(lambda_k2), da_out_norm_w[l].reshape(1, LANES).astype(F32))
    mix_a = lax.cond(bound <= MAX_FIXED_SHIFT,
                     lambda a: _attention(*a, B, S, tq, lam_init, online=False),
                     lambda a: _attention(*a, B, S, tq, lam_init, online=True), attn_args)

    mix_b = _gdn(gq, gk, gv, z, ab, pad_lanes(A_log[l]), pad_lanes(dt_bias[l]),
                 dn_out_norm_w[l].reshape(1, LANES).astype(F32), B, S, tb)

    wo = w_out[l].astype(BF16)
    return _mlp(x2, mix_a, mix_b, wo[:HEAD_COLS], wo[HEAD_COLS:], norm2_w[l].reshape(1, D).astype(F32),
                w_up[l].astype(BF16), w_down[l].astype(BF16), tm, ff_chunk)


def kernel(x, norm1_w, w_in, lambda_q1, lambda_k1, lambda_q2, lambda_k2, q_norm_w, k_norm_w, da_out_norm_w,
           conv_w, A_log, dt_bias, dn_out_norm_w, w_out, norm2_w, w_up, w_down):
    B, S, D = x.shape
    x2 = x.reshape(B * S, D)
    tm = min(512, S)
    tq = min(512, S)
    tb = min(512, S)
    for l in range(w_in.shape[0]):
        x2 = _layer(x2, B, S, l, norm1_w, w_in, lambda_q1, lambda_k1, lambda_q2, lambda_k2, q_norm_w,
                    k_norm_w, da_out_norm_w, conv_w, A_log, dt_bias, dn_out_norm_w, w_out, norm2_w,
                    w_up, w_down, tm, tq, tb, 1024)
    return x2.reshape(B, S, D)
```

```python
import functools
import math

import jax
import jax.numpy as jnp
from jax import lax
from jax.experimental import pallas as pl
from jax.experimental.pallas import tpu as pltpu

F32 = jnp.float32
BF16 = jnp.bfloat16
EPS = 1e-6
LANES = 128
NEG_BIG = -1e30
LOG2E = math.log2(math.e)
MAX_FIXED_SHIFT = 40.0

DA_HEADS = 4
DA_QK_DIM = 64
DN_HEADS = 4
DN_DIM = 128
CONV_WIDTH = 4
HEAD_COLS = 512
GDN_CHUNK = 128
VMEM_LIMIT = 56 * 1024 * 1024


def _dot(a, b):
    return jnp.dot(a, b, preferred_element_type=F32)


def _dot_nt(a, b):
    return lax.dot_general(a, b, (((1,), (1,)), ((), ())), preferred_element_type=F32)


def _dot_tn(a, b):
    return lax.dot_general(a, b, (((0,), (0,)), ((), ())), preferred_element_type=F32)


def _silu(x):
    h = 0.5 * x
    return h + h * jnp.tanh(h)


def _split3(x):
    x1 = x.astype(BF16)
    r1 = x - x1.astype(F32)
    x2 = r1.astype(BF16)
    x3 = (r1 - x2.astype(F32)).astype(BF16)
    return x1, x2, x3


def _inproj_body(x_ref, n1_ref, w_ref, qnw_ref, knw_ref, cw_ref,
                 q_out, k_out, v_out, gq_out, gk_out, gv_out, z_out, ab_out, tail_ref, wb_ref,
                 *, tiles_per_seq):
    tm = x_ref.shape[0]
    piece = 2 * LANES
    wa0, wdn0, wz0, wab0 = 0, 3 * HEAD_COLS, 6 * HEAD_COLS, 7 * HEAD_COLS
    n_in = w_ref.shape[1]

    @pl.when(pl.program_id(0) == 0)
    def _():
        for c0 in range(0, wab0, HEAD_COLS):
            wb_ref[:, c0:c0 + HEAD_COLS] = w_ref[:, c0:c0 + HEAD_COLS].astype(BF16)
        wb_ref[:, wab0:wab0 + LANES] = jnp.zeros((wb_ref.shape[0], LANES), BF16)
        wb_ref[:, wab0:n_in] = w_ref[:, wab0:n_in].astype(BF16)

    x = x_ref[...]
    ms = jnp.mean(x * x, axis=-1, keepdims=True)
    h = (x * lax.rsqrt(ms + EPS) * n1_ref[...]).astype(BF16)

    @pl.when(pl.program_id(0) % tiles_per_seq == 0)
    def _():
        tail_ref[...] = jnp.zeros_like(tail_ref)

    def deltanet_piece(p0):
        d = _dot(h, wb_ref[:, wdn0 + p0:wdn0 + p0 + piece])
        tail = tail_ref[:, p0:p0 + piece]
        tail_ref[:, p0:p0 + piece] = d[tm - 8:tm, :]
        for cc in range(0, piece, LANES):
            c0 = p0 + cc
            cb = c0 // LANES
            for r0 in range(0, tm, GDN_CHUNK):
                halo = tail[:, cc:cc + LANES] if r0 == 0 else d[r0 - 8:r0, cc:cc + LANES]
                a = jnp.concatenate([halo, d[r0:r0 + GDN_CHUNK, cc:cc + LANES]], axis=0)
                tap = lambda w: cw_ref[w:w + 1, c0:c0 + LANES]
                a1 = pltpu.roll(a, 1, axis=0)
                u = tap(1) * a + tap(0) * a1
                y = tap(3) * a[8:] + tap(2) * a1[8:] + pltpu.roll(u, 2, axis=0)[8:]
                y = y + y * jnp.tanh(y)
                if cb < DN_HEADS:
                    y = y * (lax.rsqrt(jnp.sum(y * y, axis=-1, keepdims=True) + EPS) * (DN_DIM ** -0.5))
                elif cb < 2 * DN_HEADS:
                    y = y * lax.rsqrt(jnp.sum(y * y, axis=-1, keepdims=True) + EPS)
                out = (gq_out, gk_out, gv_out)[cb // DN_HEADS]
                c1 = (cb % DN_HEADS) * LANES
                out[r0:r0 + GDN_CHUNK, c1:c1 + LANES] = y.astype(BF16)

    lo = lax.broadcasted_iota(jnp.int32, (tm, LANES), 1) < DA_QK_DIM

    def qk_norm(t, w8):
        sq = t * t
        s_lo = jnp.sum(jnp.where(lo, sq, 0.0), axis=-1, keepdims=True)
        s_hi = jnp.sum(jnp.where(lo, 0.0, sq), axis=-1, keepdims=True)
        return t * lax.rsqrt(jnp.where(lo, s_lo, s_hi) + DA_QK_DIM * EPS) * w8

    def attention_piece(p0):
        t = _dot(h, wb_ref[:, wa0 + p0:wa0 + p0 + piece])
        for cc in range(0, piece, LANES):
            th = t[:, cc:cc + LANES]
            c1 = p0 % HEAD_COLS + cc
            if p0 < HEAD_COLS:
                q_out[:, c1:c1 + LANES] = qk_norm(th, qnw_ref[...]).astype(BF16)
            elif p0 < 2 * HEAD_COLS:
                k_out[:, c1:c1 + LANES] = qk_norm(th, knw_ref[...]).astype(BF16)
            else:
                v_out[:, c1:c1 + LANES] = th.astype(BF16)

    for p0 in range(0, 3 * HEAD_COLS, piece):
        deltanet_piece(p0)
        attention_piece(p0)
    for p0 in range(0, HEAD_COLS, piece):
        z_out[:, p0:p0 + piece] = _dot(h, wb_ref[:, wz0 + p0:wz0 + p0 + piece])
    ab_out[...] = _dot(h, wb_ref[:, wab0:wab0 + LANES])


def _inproj(x2, n1, w_in, l, qnw, knw, cw, S, tm):
    T, D = x2.shape
    W = cw.shape[1]
    n_in = w_in.shape[2]
    const = lambda i: (0, 0)
    row = lambda i: (i, 0)
    slab = pl.BlockSpec((tm, HEAD_COLS), row)
    slab_bf16 = jax.ShapeDtypeStruct((T, HEAD_COLS), BF16)
    return pl.pallas_call(
        functools.partial(_inproj_body, tiles_per_seq=S // tm),
        grid=(T // tm,),
        in_specs=[
            pl.BlockSpec((tm, D), row),
            pl.BlockSpec((1, D), const),
            pl.BlockSpec((None, D, n_in), lambda i: (l, 0, 0)),
            pl.BlockSpec((1, LANES), const),
            pl.BlockSpec((1, LANES), const),
            pl.BlockSpec((CONV_WIDTH, W), const),
        ],
        out_specs=[slab, slab, slab, slab, slab, slab, slab, pl.BlockSpec((tm, LANES), row)],
        out_shape=[slab_bf16, slab_bf16, slab_bf16, slab_bf16, slab_bf16, slab_bf16,
                   jax.ShapeDtypeStruct((T, HEAD_COLS), F32),
                   jax.ShapeDtypeStruct((T, LANES), F32)],
        scratch_shapes=[
            pltpu.VMEM((8, W), F32),
            pltpu.VMEM((D, 7 * HEAD_COLS + LANES), BF16),
        ],
        compiler_params=pltpu.CompilerParams(
            dimension_semantics=("arbitrary",), vmem_limit_bytes=VMEM_LIMIT),
        name="inproj",
    )(x2, n1, w_in, qnw, knw, cw)


def _stack_maps(q_ref, qz_ref, tq):
    q = q_ref[...]
    lane = lax.broadcasted_iota(jnp.int32, (tq, LANES), 1)
    zero = jnp.zeros_like(q)
    qz_ref[0:tq, :] = jnp.where(lane < DA_QK_DIM, q, zero)
    qz_ref[tq:2 * tq, :] = jnp.where(lane < DA_QK_DIM, zero, q)


def _causal_mask(tq):
    rows = lax.broadcasted_iota(jnp.int32, (2 * tq, tq), 0)
    cols = lax.broadcasted_iota(jnp.int32, (2 * tq, tq), 1)
    return jnp.where(rows >= tq, rows - tq, rows) >= cols


def _attn_finish(acc, l, lq1_ref, lk1_ref, lq2_ref, lk2_ref, onw_ref, o_ref, tq, lam_init):
    lam = (jnp.exp(jnp.sum(lq1_ref[...] * lk1_ref[...], axis=-1, keepdims=True))
           - jnp.exp(jnp.sum(lq2_ref[...] * lk2_ref[...], axis=-1, keepdims=True)) + lam_init)
    o = acc[:tq] / l[:tq] - lam * (acc[tq:] / l[tq:])
    ms = jnp.mean(o * o, axis=-1, keepdims=True)
    o = o * lax.rsqrt(ms + EPS) * onw_ref[...] * (1.0 - lam_init)
    o_ref[...] = o.astype(BF16)


def _attn_shift_body(shift_ref, q_ref, k_ref, v_ref, lq1_ref, lk1_ref, lq2_ref, lk2_ref, onw_ref, o_ref,
                     qz_ref, p_ref, l_ref, acc_ref, *, tq, lam_init, unroll, heads):
    i = pl.program_id(2)
    shift = shift_ref[0]
    cols = lambda hh: slice(hh * LANES, (hh + 1) * LANES)
    rows = lambda j: pl.ds(pl.multiple_of(j * tq, tq), tq)

    def probs(hh, j, diagonal):
        s = _dot_nt(qz_ref[hh], k_ref[rows(j), cols(hh)]) - shift
        if diagonal:
            s = jnp.where(_causal_mask(tq), s, NEG_BIG)
        p = jnp.exp2(s)
        psum = p[:, 0:LANES]
        for c0 in range(LANES, tq, LANES):
            psum = psum + p[:, c0:c0 + LANES]
        return p.astype(BF16), psum

    for hh in range(heads):
        _stack_maps(q_ref.at[:, cols(hh)], qz_ref.at[hh], tq)
        p, psum = probs(hh, i, True)
        p_ref[hh] = p
        l_ref[hh] = psum
        acc_ref[hh] = jnp.zeros((2 * tq, LANES), F32)

    def pv_prev(hh, j):
        return _dot(p_ref[hh], v_ref[rows(jnp.where(j == 0, i, j - 1)), cols(hh)])

    def steps(j0, n):
        acc = [pv_prev(hh, j0) for hh in range(heads)]
        psum = [None] * heads
        for u in range(n):
            for hh in range(heads):
                p, ps = probs(hh, j0 + u, False)
                psum[hh] = ps if psum[hh] is None else psum[hh] + ps
                if u + 1 < n:
                    acc[hh] = acc[hh] + _dot(p, v_ref[rows(j0 + u), cols(hh)])
                else:
                    p_ref[hh] = p
        for hh in range(heads):
            acc_ref[hh] += acc[hh]
            l_ref[hh] += psum[hh]

    start = 0
    n = unroll
    while n >= 1:
        trips = (i - start) // n

        def group(t, _, start=start, n=n):
            steps(start + t * n, n)
            return 0

        lax.fori_loop(0, trips, group, 0)
        start = start + trips * n
        n //= 2
    for hh in range(heads):
        acc = acc_ref[hh] + pv_prev(hh, i)
        l = jnp.sum(l_ref[hh], axis=-1, keepdims=True)
        _attn_finish(acc, l, lq1_ref, lk1_ref, lq2_ref, lk2_ref, onw_ref, o_ref.at[:, cols(hh)], tq, lam_init)


def _attn_online_body(shift_ref, q_ref, k_ref, v_ref, lq1_ref, lk1_ref, lq2_ref, lk2_ref, onw_ref, o_ref,
                      qz_ref, *, tq, lam_init):
    del shift_ref
    i = pl.program_id(2)
    _stack_maps(q_ref, qz_ref, tq)

    def step(j, carry, diagonal):
        m, l, acc = carry
        r0 = pl.multiple_of(j * tq, tq)
        s = _dot_nt(qz_ref[...], k_ref[pl.ds(r0, tq), :])
        if diagonal:
            s = jnp.where(_causal_mask(tq), s, NEG_BIG)
        m_new = jnp.maximum(m, jnp.max(s, axis=-1, keepdims=True))
        p = jnp.exp2(s - m_new)
        alpha = jnp.exp2(m - m_new)
        l = alpha * l + jnp.sum(p, axis=-1, keepdims=True)
        acc = alpha * acc + _dot(p.astype(BF16), v_ref[pl.ds(r0, tq), :])
        return m_new, l, acc

    init = (jnp.full((2 * tq, 1), NEG_BIG, F32), jnp.zeros((2 * tq, 1), F32),
            jnp.zeros((2 * tq, LANES), F32))
    carry = lax.fori_loop(0, i, lambda j, c: step(j, c, False), init)
    _, l, acc = step(i, carry, True)
    _attn_finish(acc, l, lq1_ref, lk1_ref, lq2_ref, lk2_ref, onw_ref, o_ref, tq, lam_init)


def _attention(shift, q, k, v, lq1, lk1, lq2, lk2, onw, B, S, tq, lam_init, online):
    nq = S // tq
    vec = pl.BlockSpec((1, DA_QK_DIM), lambda b, h, i: (0, 0))
    if online:
        heads = 1
        body = functools.partial(_attn_online_body, tq=tq, lam_init=lam_init)
        scratch = [pltpu.VMEM((2 * tq, LANES), BF16)]
    else:
        heads = DA_HEADS
        body = functools.partial(_attn_shift_body, tq=tq, lam_init=lam_init, unroll=4, heads=heads)
        scratch = [pltpu.VMEM((heads, 2 * tq, LANES), BF16),
                   pltpu.VMEM((heads, 2 * tq, tq), BF16),
                   pltpu.VMEM((heads, 2 * tq, LANES), F32),
                   pltpu.VMEM((heads, 2 * tq, LANES), F32)]
    width = heads * LANES
    return pl.pallas_call(
        body,
        grid=(B, DA_HEADS // heads, nq),
        in_specs=[
            pl.BlockSpec(memory_space=pltpu.SMEM),
            pl.BlockSpec((tq, width), lambda b, h, i: (b * nq + i, h)),
            pl.BlockSpec((S, width), lambda b, h, i: (b, h)),
            pl.BlockSpec((S, width), lambda b, h, i: (b, h)),
            vec, vec, vec, vec,
            pl.BlockSpec((1, LANES), lambda b, h, i: (0, 0)),
        ],
        out_specs=pl.BlockSpec((tq, width), lambda b, h, i: (b * nq + i, h)),
        out_shape=jax.ShapeDtypeStruct((B * S, HEAD_COLS), BF16),
        scratch_shapes=scratch,
        compiler_params=pltpu.CompilerParams(
            dimension_semantics=("arbitrary", "arbitrary", "arbitrary"), vmem_limit_bytes=VMEM_LIMIT),
        name="diff_attention_online" if online else "diff_attention",
    )(shift, q, k, v, lq1, lk1, lq2, lk2, onw)


def _merge_masks(row, col, n):
    masks = []
    b = 2
    while b < n:
        masks.append(((row // b) == (col // b) + 1) & ((row // (2 * b)) == (col // (2 * b))))
        b *= 2
    return masks


def _gdn_body(q_ref, k_ref, v_ref, z_ref, ab_ref, alog_ref, dtb_ref, onw_ref, o_ref,
              state_ref, l_ref, t_ref, rhs_ref, ou_ref, ku_ref, qa_ref, qk_ref, kdt_ref, egl_ref,
              fmask_ref, lmask_ref, *, tb, group):
    C = GDN_CHUNK
    nc = tb // C
    blk = pl.program_id(1)
    n_levels = lmask_ref.shape[0]

    @pl.when(blk == 0)
    def _():
        state_ref[...] = jnp.zeros_like(state_ref)
        row = lax.broadcasted_iota(jnp.int32, (C, C), 0)
        col = lax.broadcasted_iota(jnp.int32, (C, C), 1)
        fmask_ref[0] = (row >= col).astype(F32)
        fmask_ref[1] = (row > col).astype(F32)
        fmask_ref[2] = (row == col).astype(F32)
        fmask_ref[3] = ((row == col + 1) & (row % 2 == 1)).astype(F32)
        for lv, m in enumerate(_merge_masks(row, col, C)):
            lmask_ref[lv] = m.astype(BF16)

    ab = ab_ref[...]
    sp_in = ab + dtb_ref[...]
    softplus = jnp.maximum(sp_in, 0.0) + jnp.log(1.0 + jnp.exp(-jnp.abs(sp_in)))
    gtok = -jnp.exp(alog_ref[...]) * softplus
    gtok = jnp.where(lax.broadcasted_iota(jnp.int32, (tb, LANES), 1) < DN_HEADS, gtok, 0.0)
    beta_all = 1.0 / (1.0 + jnp.exp(-ab))

    tri = fmask_ref[0].astype(BF16)

    for c in range(nc):
        r0 = c * C
        g1, g2, g3 = _split3(gtok[r0:r0 + C])
        gc = _dot(tri, g1) + _dot(tri, g2) + _dot(tri, g3)
        gct = gc.T
        for h in range(DN_HEADS):
            un = c * DN_HEADS + h
            c0 = h * DN_DIM
            qh = q_ref[r0:r0 + C, c0:c0 + DN_DIM].astype(F32)
            kh_bf = k_ref[r0:r0 + C, c0:c0 + DN_DIM]
            kh = kh_bf.astype(F32)
            vh = v_ref[r0:r0 + C, c0:c0 + DN_DIM].astype(F32)
            gcol = gc[:, h:h + 1]
            grow = gct[h:h + 1, :]
            beta = beta_all[r0:r0 + C, DN_HEADS + h:DN_HEADS + h + 1]
            decay = jnp.exp(jnp.minimum(gcol - grow, 0.0))
            kb = kh * beta
            a = _dot_nt(jnp.concatenate([kb, qh], axis=0).astype(BF16), kh_bf)
            Lm = a[:C] * decay * fmask_ref[1]
            l_ref[un] = Lm.astype(BF16)
            t_ref[un] = (fmask_ref[2] - Lm * fmask_ref[3]).astype(BF16)
            qk_ref[un] = (a[C:] * decay * fmask_ref[0]).astype(BF16)
            eg = jnp.exp(gcol)
            rhs_ref[un] = jnp.concatenate([vh * beta, kb * eg], axis=1).astype(BF16)
            qa_ref[un, 0:C, :] = (qh * eg).astype(BF16)
            g_last = gcol[C - 1:C, :]
            kdt_ref[un] = (kh * jnp.exp(g_last - gcol)).T.astype(BF16)
            egl_ref[un] = jnp.broadcast_to(jnp.exp(g_last), (8, LANES))

    nu = nc * DN_HEADS
    for lv in range(n_levels):
        for g0 in range(0, nu, group):
            units = list(range(g0, min(g0 + group, nu)))
            ps = [_dot(l_ref[un] * lmask_ref[lv], t_ref[un]).astype(BF16) for un in units]
            for un, p in zip(units, ps):
                t_ref[un] = t_ref[un] - _dot(t_ref[un], p).astype(BF16)
    for un in range(nu):
        uw = _dot(t_ref[un], rhs_ref[un]).astype(BF16)
        r = _dot(jnp.concatenate([qk_ref[un], kdt_ref[un]], axis=0), uw)
        ou_ref[un] = r[:C, :DN_DIM]
        qa_ref[un, 0:C, :] = (qa_ref[un, 0:C, :].astype(F32) - r[:C, DN_DIM:]).astype(BF16)
        qa_ref[un, C:C + DN_DIM, :] = (-r[C:, DN_DIM:]).astype(BF16)
        ku_ref[un] = r[C:, :DN_DIM]

    for c in range(nc):
        r0 = c * C
        for h in range(DN_HEADS):
            un = c * DN_HEADS + h
            c0 = h * DN_DIM
            state = state_ref[h]
            r = _dot(qa_ref[un], state.astype(BF16))
            o = ou_ref[un] + r[:C]
            state_ref[h] = state * egl_ref[un][0:1, :] + r[C:] + ku_ref[un]
            zh = z_ref[r0:r0 + C, c0:c0 + DN_DIM]
            o = o * lax.rsqrt(jnp.mean(o * o, axis=-1, keepdims=True) + EPS) * onw_ref[...]
            o = o * _silu(zh)
            o_ref[r0:r0 + C, c0:c0 + DN_DIM] = o.astype(BF16)


def _gdn(gq, gk, gv, z, ab, alog, dtb, onw, B, S, tb):
    nb = S // tb
    C = GDN_CHUNK
    nu = (tb // C) * DN_HEADS
    const = lambda b, i: (0, 0)
    row = lambda b, i: (b * nb + i, 0)
    return pl.pallas_call(
        functools.partial(_gdn_body, tb=tb, group=8),
        grid=(B, nb),
        in_specs=[
            pl.BlockSpec((tb, HEAD_COLS), row),
            pl.BlockSpec((tb, HEAD_COLS), row),
            pl.BlockSpec((tb, HEAD_COLS), row),
            pl.BlockSpec((tb, HEAD_COLS), row),
            pl.BlockSpec((tb, LANES), row),
            pl.BlockSpec((1, LANES), const),
            pl.BlockSpec((1, LANES), const),
            pl.BlockSpec((1, LANES), const),
        ],
        out_specs=pl.BlockSpec((tb, HEAD_COLS), row),
        out_shape=jax.ShapeDtypeStruct((B * S, HEAD_COLS), BF16),
        scratch_shapes=[
            pltpu.VMEM((DN_HEADS, DN_DIM, DN_DIM), F32),
            pltpu.VMEM((nu, C, C), BF16),
            pltpu.VMEM((nu, C, C), BF16),
            pltpu.VMEM((nu, C, 2 * DN_DIM), BF16),
            pltpu.VMEM((nu, C, DN_DIM), F32),
            pltpu.VMEM((nu, DN_DIM, DN_DIM), F32),
            pltpu.VMEM((nu, C + DN_DIM, DN_DIM), BF16),
            pltpu.VMEM((nu, C, C), BF16),
            pltpu.VMEM((nu, DN_DIM, C), BF16),
            pltpu.VMEM((nu, 8, LANES), F32),
            pltpu.VMEM((4, C, C), F32),
            pltpu.VMEM((C.bit_length() - 2, C, C), BF16),
        ],
        compiler_params=pltpu.CompilerParams(
            dimension_semantics=("arbitrary", "arbitrary"), vmem_limit_bytes=VMEM_LIMIT),
        name="gated_deltanet",
    )(gq, gk, gv, z, ab, alog, dtb, onw)


def _mlp_body(x_ref, ma_ref, mb_ref, woa_ref, wob_ref, n2_ref, wup_ref, wdn_ref, o_ref, *, ff_chunk):
    x1 = x_ref[...] + _dot(ma_ref[...], woa_ref[...]) + _dot(mb_ref[...], wob_ref[...])
    ms = jnp.mean(x1 * x1, axis=-1, keepdims=True)
    h = (x1 * lax.rsqrt(ms + EPS) * n2_ref[...]).astype(BF16)
    mlp = None
    for c0 in range(0, wup_ref.shape[1], ff_chunk):
        up = jnp.maximum(_dot(h, wup_ref[:, c0:c0 + ff_chunk]), 0.0)
        down = _dot((up * up).astype(BF16), wdn_ref[c0:c0 + ff_chunk, :])
        mlp = down if mlp is None else mlp + down
    o_ref[...] = x1 + mlp


def _mlp(x2, mix_a, mix_b, woa, wob, n2, wup, wdn, tm, ff_chunk):
    T, D = x2.shape
    const = lambda i: (0, 0)
    row = lambda i: (i, 0)
    return pl.pallas_call(
        functools.partial(_mlp_body, ff_chunk=ff_chunk),
        grid=(T // tm,),
        in_specs=[
            pl.BlockSpec((tm, D), row),
            pl.BlockSpec((tm, HEAD_COLS), row),
            pl.BlockSpec((tm, HEAD_COLS), row),
            pl.BlockSpec(woa.shape, const),
            pl.BlockSpec(wob.shape, const),
            pl.BlockSpec((1, D), const),
            pl.BlockSpec(wup.shape, const),
            pl.BlockSpec(wdn.shape, const),
        ],
        out_specs=pl.BlockSpec((tm, D), row),
        out_shape=jax.ShapeDtypeStruct((T, D), F32),
        compiler_params=pltpu.CompilerParams(
            dimension_semantics=("arbitrary",), vmem_limit_bytes=VMEM_LIMIT),
        name="outproj_mlp",
    )(x2, mix_a, mix_b, woa, wob, n2, wup, wdn)


def _layer(x2, B, S, l, norm1_w, w_in, lambda_q1, lambda_k1, lambda_q2, lambda_k2, q_norm_w, k_norm_w,
           da_out_norm_w, conv_w, A_log, dt_bias, dn_out_norm_w, w_out, norm2_w, w_up, w_down,
           tm, tq, tb, ff_chunk):
    D = x2.shape[1]
    tile2 = lambda v: jnp.concatenate([v, v]).reshape(1, LANES).astype(F32)
    pad_lanes = lambda v: jnp.pad(v.astype(F32), (0, LANES - v.shape[0])).reshape(1, LANES)

    q, k, v, gq, gk, gv, z, ab = _inproj(
        x2, norm1_w[l].reshape(1, D).astype(F32), w_in.astype(F32), l,
        tile2(q_norm_w[l]) * (DA_QK_DIM ** 0.5 * DA_QK_DIM ** -0.5 * LOG2E), tile2(k_norm_w[l]) * DA_QK_DIM ** 0.5,
        0.5 * conv_w[l].astype(F32), S, tm)

    lam_init = 0.8 - 0.6 * math.exp(-0.3 * l)
    vec = lambda p: p[l].reshape(1, DA_QK_DIM).astype(F32)
    bound = 8.0 * jnp.max(jnp.abs(q_norm_w[l].astype(F32))) * jnp.max(jnp.abs(k_norm_w[l].astype(F32)))
    attn_args = ((bound * LOG2E).reshape(1), q, k, v, vec(lambda_q1), vec(lambda_k1), vec(lambda_q2),
                 vec(lambda_k2), da_out_norm_w[l].reshape(1, LANES).astype(F32))
    mix_a = lax.cond(bound <= MAX_FIXED_SHIFT,
                     lambda a: _attention(*a, B, S, tq, lam_init, online=False),
                     lambda a: _attention(*a, B, S, tq, lam_init, online=True), attn_args)

    mix_b = _gdn(gq, gk, gv, z, ab, pad_lanes(A_log[l]), pad_lanes(dt_bias[l]),
                 dn_out_norm_w[l].reshape(1, LANES).astype(F32), B, S, tb)

    wo = w_out[l].astype(BF16)
    return _mlp(x2, mix_a, mix_b, wo[:HEAD_COLS], wo[HEAD_COLS:], norm2_w[l].reshape(1, D).astype(F32),
                w_up[l].astype(BF16), w_down[l].astype(BF16), tm, ff_chunk)


def kernel(x, norm1_w, w_in, lambda_q1, lambda_k1, lambda_q2, lambda_k2, q_norm_w, k_norm_w, da_out_norm_w,
           conv_w, A_log, dt_bias, dn_out_norm_w, w_out, norm2_w, w_up, w_down):
    B, S, D = x.shape
    x2 = x.reshape(B * S, D)
    tm = min(512, S)
    tq = min(512, S)
    tb = min(512, S)
    for l in range(w_in.shape[0]):
        x2 = _layer(x2, B, S, l, norm1_w, w_in, lambda_q1, lambda_k1, lambda_q2, lambda_k2, q_norm_w,
                    k_norm_w, da_out_norm_w, conv_w, A_log, dt_bias, dn_out_norm_w, w_out, norm2_w,
                    w_up, w_down, tm, tq, tb, 1024)
    return x2.reshape(B, S, D)
```

```python
import functools
import math

import jax
import jax.numpy as jnp
from jax import lax
from jax.experimental import pallas as pl
from jax.experimental.pallas import tpu as pltpu

F32 = jnp.float32
BF16 = jnp.bfloat16
EPS = 1e-6
LANES = 128
NEG_BIG = -1e30
LOG2E = math.log2(math.e)
MAX_FIXED_SHIFT = 40.0

DA_HEADS = 4
DA_QK_DIM = 64
DN_HEADS = 4
DN_DIM = 128
CONV_WIDTH = 4
HEAD_COLS = 512
GDN_CHUNK = 128
VMEM_LIMIT = 56 * 1024 * 1024


def _dot(a, b):
    return jnp.dot(a, b, preferred_element_type=F32)


def _dot_nt(a, b):
    return lax.dot_general(a, b, (((1,), (1,)), ((), ())), preferred_element_type=F32)


def _dot_tn(a, b):
    return lax.dot_general(a, b, (((0,), (0,)), ((), ())), preferred_element_type=F32)


def _silu(x):
    h = 0.5 * x
    return h + h * jnp.tanh(h)


def _split3(x):
    x1 = x.astype(BF16)
    r1 = x - x1.astype(F32)
    x2 = r1.astype(BF16)
    x3 = (r1 - x2.astype(F32)).astype(BF16)
    return x1, x2, x3


def _inproj_body(x_ref, n1_ref, w_ref, qnw_ref, knw_ref, cw_ref,
                 q_out, k_out, v_out, gq_out, gk_out, gv_out, z_out, ab_out, tail_ref, wb_ref,
                 *, tiles_per_seq):
    tm = x_ref.shape[0]
    piece = 2 * LANES
    wa0, wdn0, wz0, wab0 = 0, 3 * HEAD_COLS, 6 * HEAD_COLS, 7 * HEAD_COLS
    n_in = w_ref.shape[1]

    @pl.when(pl.program_id(0) == 0)
    def _():
        for c0 in range(0, wab0, HEAD_COLS):
            wb_ref[:, c0:c0 + HEAD_COLS] = w_ref[:, c0:c0 + HEAD_COLS].astype(BF16)
        wb_ref[:, wab0:wab0 + LANES] = jnp.zeros((wb_ref.shape[0], LANES), BF16)
        wb_ref[:, wab0:n_in] = w_ref[:, wab0:n_in].astype(BF16)

    x = x_ref[...]
    ms = jnp.mean(x * x, axis=-1, keepdims=True)
    h = (x * lax.rsqrt(ms + EPS) * n1_ref[...]).astype(BF16)

    @pl.when(pl.program_id(0) % tiles_per_seq == 0)
    def _():
        tail_ref[...] = jnp.zeros_like(tail_ref)

    def deltanet_piece(p0):
        d = _dot(h, wb_ref[:, wdn0 + p0:wdn0 + p0 + piece])
        tail = tail_ref[:, p0:p0 + piece]
        tail_ref[:, p0:p0 + piece] = d[tm - 8:tm, :]
        for cc in range(0, piece, LANES):
            c0 = p0 + cc
            cb = c0 // LANES
            for r0 in range(0, tm, GDN_CHUNK):
                halo = tail[:, cc:cc + LANES] if r0 == 0 else d[r0 - 8:r0, cc:cc + LANES]
                a = jnp.concatenate([halo, d[r0:r0 + GDN_CHUNK, cc:cc + LANES]], axis=0)
                tap = lambda w: cw_ref[w:w + 1, c0:c0 + LANES]
                a1 = pltpu.roll(a, 1, axis=0)
                u = tap(1) * a + tap(0) * a1
                y = tap(3) * a[8:] + tap(2) * a1[8:] + pltpu.roll(u, 2, axis=0)[8:]
                y = y + y * jnp.tanh(y)
                if cb < DN_HEADS:
                    y = y * (lax.rsqrt(jnp.sum(y * y, axis=-1, keepdims=True) + EPS) * (DN_DIM ** -0.5))
                elif cb < 2 * DN_HEADS:
                    y = y * lax.rsqrt(jnp.sum(y * y, axis=-1, keepdims=True) + EPS)
                out = (gq_out, gk_out, gv_out)[cb // DN_HEADS]
                c1 = (cb % DN_HEADS) * LANES
                out[r0:r0 + GDN_CHUNK, c1:c1 + LANES] = y.astype(BF16)

    lo = lax.broadcasted_iota(jnp.int32, (tm, LANES), 1) < DA_QK_DIM

    def qk_norm(t, w8):
        sq = t * t
        s_lo = jnp.sum(jnp.where(lo, sq, 0.0), axis=-1, keepdims=True)
        s_hi = jnp.sum(jnp.where(lo, 0.0, sq), axis=-1, keepdims=True)
        return t * lax.rsqrt(jnp.where(lo, s_lo, s_hi) + DA_QK_DIM * EPS) * w8

    def attention_piece(p0):
        t = _dot(h, wb_ref[:, wa0 + p0:wa0 + p0 + piece])
        for cc in range(0, piece, LANES):
            th = t[:, cc:cc + LANES]
            c1 = p0 % HEAD_COLS + cc
            if p0 < HEAD_COLS:
                q_out[:, c1:c1 + LANES] = qk_norm(th, qnw_ref[...]).astype(BF16)
            elif p0 < 2 * HEAD_COLS:
                k_out[:, c1:c1 + LANES] = qk_norm(th, knw_ref[...]).astype(BF16)
            else:
                v_out[:, c1:c1 + LANES] = th.astype(BF16)

    for p0 in range(0, 3 * HEAD_COLS, piece):
        deltanet_piece(p0)
        attention_piece(p0)
    for p0 in range(0, HEAD_COLS, piece):
        z_out[:, p0:p0 + piece] = _dot(h, wb_ref[:, wz0 + p0:wz0 + p0 + piece])
    ab_out[...] = _dot(h, wb_ref[:, wab0:wab0 + LANES])


def _inproj(x2, n1, w_in, l, qnw, knw, cw, S, tm):
    T, D = x2.shape
    W = cw.shape[1]
    n_in = w_in.shape[2]
    const = lambda i: (0, 0)
    row = lambda i: (i, 0)
    slab = pl.BlockSpec((tm, HEAD_COLS), row)
    slab_bf16 = jax.ShapeDtypeStruct((T, HEAD_COLS), BF16)
    return pl.pallas_call(
        functools.partial(_inproj_body, tiles_per_seq=S // tm),
        grid=(T // tm,),
        in_specs=[
            pl.BlockSpec((tm, D), row),
            pl.BlockSpec((1, D), const),
            pl.BlockSpec((None, D, n_in), lambda i: (l, 0, 0)),
            pl.BlockSpec((1, LANES), const),
            pl.BlockSpec((1, LANES), const),
            pl.BlockSpec((CONV_WIDTH, W), const),
        ],
        out_specs=[slab, slab, slab, slab, slab, slab, slab, pl.BlockSpec((tm, LANES), row)],
        out_shape=[slab_bf16, slab_bf16, slab_bf16, slab_bf16, slab_bf16, slab_bf16,
                   jax.ShapeDtypeStruct((T, HEAD_COLS), F32),
                   jax.ShapeDtypeStruct((T, LANES), F32)],
        scratch_shapes=[
            pltpu.VMEM((8, W), F32),
            pltpu.VMEM((D, 7 * HEAD_COLS + LANES), BF16),
        ],
        compiler_params=pltpu.CompilerParams(
            dimension_semantics=("arbitrary",), vmem_limit_bytes=VMEM_LIMIT),
        name="inproj",
    )(x2, n1, w_in, qnw, knw, cw)


def _stack_maps(q_ref, qz_ref, tq):
    q = q_ref[...]
    lane = lax.broadcasted_iota(jnp.int32, (tq, LANES), 1)
    zero = jnp.zeros_like(q)
    qz_ref[0:tq, :] = jnp.where(lane < DA_QK_DIM, q, zero)
    qz_ref[tq:2 * tq, :] = jnp.where(lane < DA_QK_DIM, zero, q)


def _causal_mask(tq):
    rows = lax.broadcasted_iota(jnp.int32, (2 * tq, tq), 0)
    cols = lax.broadcasted_iota(jnp.int32, (2 * tq, tq), 1)
    return jnp.where(rows >= tq, rows - tq, rows) >= cols


def _attn_finish(acc, l, lq1_ref, lk1_ref, lq2_ref, lk2_ref, onw_ref, o_ref, tq, lam_init):
    lam = (jnp.exp(jnp.sum(lq1_ref[...] * lk1_ref[...], axis=-1, keepdims=True))
           - jnp.exp(jnp.sum(lq2_ref[...] * lk2_ref[...], axis=-1, keepdims=True)) + lam_init)
    o = acc[:tq] / l[:tq] - lam * (acc[tq:] / l[tq:])
    ms = jnp.mean(o * o, axis=-1, keepdims=True)
    o = o * lax.rsqrt(ms + EPS) * onw_ref[...] * (1.0 - lam_init)
    o_ref[...] = o.astype(BF16)


def _attn_shift_body(shift_ref, q_ref, k_ref, v_ref, lq1_ref, lk1_ref, lq2_ref, lk2_ref, onw_ref, o_ref,
                     qz_ref, p_ref, l_ref, acc_ref, *, tq, lam_init, unroll, heads):
    i = pl.program_id(2)
    shift = shift_ref[0]
    cols = lambda hh: slice(hh * LANES, (hh + 1) * LANES)
    rows = lambda j: pl.ds(pl.multiple_of(j * tq, tq), tq)

    def probs(hh, j, diagonal):
        s = _dot_nt(qz_ref[hh], k_ref[rows(j), cols(hh)]) - shift
        if diagonal:
            s = jnp.where(_causal_mask(tq), s, NEG_BIG)
        p = jnp.exp2(s)
        psum = p[:, 0:LANES]
        for c0 in range(LANES, tq, LANES):
            psum = psum + p[:, c0:c0 + LANES]
        return p.astype(BF16), psum

    for hh in range(heads):
        _stack_maps(q_ref.at[:, cols(hh)], qz_ref.at[hh], tq)
        p, psum = probs(hh, i, True)
        p_ref[hh] = p
        l_ref[hh] = psum
        acc_ref[hh] = jnp.zeros((2 * tq, LANES), F32)

    def pv_prev(hh, j):
        return _dot(p_ref[hh], v_ref[rows(jnp.where(j == 0, i, j - 1)), cols(hh)])

    def steps(j0, n):
        acc = [pv_prev(hh, j0) for hh in range(heads)]
        psum = [None] * heads
        for u in range(n):
            for hh in range(heads):
                p, ps = probs(hh, j0 + u, False)
                psum[hh] = ps if psum[hh] is None else psum[hh] + ps
                if u + 1 < n:
                    acc[hh] = acc[hh] + _dot(p, v_ref[rows(j0 + u), cols(hh)])
                else:
                    p_ref[hh] = p
        for hh in range(heads):
            acc_ref[hh] += acc[hh]
            l_ref[hh] += psum[hh]

    start = 0
    n = unroll
    while n >= 1:
        trips = (i - start) // n

        def group(t, _, start=start, n=n):
            steps(start + t * n, n)
            return 0

        lax.fori_loop(0, trips, group, 0)
        start = start + trips * n
        n //= 2
    for hh in range(heads):
        acc = acc_ref[hh] + pv_prev(hh, i)
        l = jnp.sum(l_ref[hh], axis=-1, keepdims=True)
        _attn_finish(acc, l, lq1_ref, lk1_ref, lq2_ref, lk2_ref, onw_ref, o_ref.at[:, cols(hh)], tq, lam_init)


def _attn_online_body(shift_ref, q_ref, k_ref, v_ref, lq1_ref, lk1_ref, lq2_ref, lk2_ref, onw_ref, o_ref,
                      qz_ref, *, tq, lam_init):
    del shift_ref
    i = pl.program_id(2)
    _stack_maps(q_ref, qz_ref, tq)

    def step(j, carry, diagonal):
        m, l, acc = carry
        r0 = pl.multiple_of(j * tq, tq)
        s = _dot_nt(qz_ref[...], k_ref[pl.ds(r0, tq), :])
        if diagonal:
            s = jnp.where(_causal_mask(tq), s, NEG_BIG)
        m_new = jnp.maximum(m, jnp.max(s, axis=-1, keepdims=True))
        p = jnp.exp2(s - m_new)
        alpha = jnp.exp2(m - m_new)
        l = alpha * l + jnp.sum(p, axis=-1, keepdims=True)
        acc = alpha * acc + _dot(p.astype(BF16), v_ref[pl.ds(r0, tq), :])
        return m_new, l, acc

    init = (jnp.full((2 * tq, 1), NEG_BIG, F32), jnp.zeros((2 * tq, 1), F32),
            jnp.zeros((2 * tq, LANES), F32))
    carry = lax.fori_loop(0, i, lambda j, c: step(j, c, False), init)
    _, l, acc = step(i, carry, True)
    _attn_finish(acc, l, lq1_ref, lk1_ref, lq2_ref, lk2_ref, onw_ref, o_ref, tq, lam_init)


def _attention(shift, q, k, v, lq1, lk1, lq2, lk2, onw, B, S, tq, lam_init, online):
    nq = S // tq
    vec = pl.BlockSpec((1, DA_QK_DIM), lambda b, h, i: (0, 0))
    if online:
        heads = 1
        body = functools.partial(_attn_online_body, tq=tq, lam_init=lam_init)
        scratch = [pltpu.VMEM((2 * tq, LANES), BF16)]
    else:
        heads = DA_HEADS
        body = functools.partial(_attn_shift_body, tq=tq, lam_init=lam_init, unroll=4, heads=heads)
        scratch = [pltpu.VMEM((heads, 2 * tq, LANES), BF16),
                   pltpu.VMEM((heads, 2 * tq, tq), BF16),
                   pltpu.VMEM((heads, 2 * tq, LANES), F32),
                   pltpu.VMEM((heads, 2 * tq, LANES), F32)]
    width = heads * LANES
    return pl.pallas_call(
        body,
        grid=(B, DA_HEADS // heads, nq),
        in_specs=[
            pl.BlockSpec(memory_space=pltpu.SMEM),
            pl.BlockSpec((tq, width), lambda b, h, i: (b * nq + i, h)),
            pl.BlockSpec((S, width), lambda b, h, i: (b, h)),
            pl.BlockSpec((S, width), lambda b, h, i: (b, h)),
            vec, vec, vec, vec,
            pl.BlockSpec((1, LANES), lambda b, h, i: (0, 0)),
        ],
        out_specs=pl.BlockSpec((tq, width), lambda b, h, i: (b * nq + i, h)),
        out_shape=jax.ShapeDtypeStruct((B * S, HEAD_COLS), BF16),
        scratch_shapes=scratch,
        compiler_params=pltpu.CompilerParams(
            dimension_semantics=("arbitrary", "arbitrary", "arbitrary"), vmem_limit_bytes=VMEM_LIMIT),
        name="diff_attention_online" if online else "diff_attention",
    )(shift, q, k, v, lq1, lk1, lq2, lk2, onw)


def _merge_masks(row, col, n):
    masks = []
    b = 2
    while b < n:
        masks.append(((row // b) == (col // b) + 1) & ((row // (2 * b)) == (col // (2 * b))))
        b *= 2
    return masks


def _gdn_body(q_ref, k_ref, v_ref, z_ref, ab_ref, alog_ref, dtb_ref, onw_ref, o_ref,
              state_ref, l_ref, t_ref, rhs_ref, ou_ref, ku_ref, qa_ref, qk_ref, kdt_ref, egl_ref,
              fmask_ref, lmask_ref, *, tb, group):
    C = GDN_CHUNK
    nc = tb // C
    blk = pl.program_id(1)
    n_levels = lmask_ref.shape[0]

    @pl.when(blk == 0)
    def _():
        state_ref[...] = jnp.zeros_like(state_ref)
        row = lax.broadcasted_iota(jnp.int32, (C, C), 0)
        col = lax.broadcasted_iota(jnp.int32, (C, C), 1)
        fmask_ref[0] = (row >= col).astype(F32)
        fmask_ref[1] = (row > col).astype(F32)
        fmask_ref[2] = (row == col).astype(F32)
        fmask_ref[3] = ((row == col + 1) & (row % 2 == 1)).astype(F32)
        for lv, m in enumerate(_merge_masks(row, col, C)):
            lmask_ref[lv] = m.astype(BF16)

    ab = ab_ref[...]
    sp_in = ab + dtb_ref[...]
    softplus = jnp.maximum(sp_in, 0.0) + jnp.log(1.0 + jnp.exp(-jnp.abs(sp_in)))
    gtok = -jnp.exp(alog_ref[...]) * softplus
    gtok = jnp.where(lax.broadcasted_iota(jnp.int32, (tb, LANES), 1) < DN_HEADS, gtok, 0.0)
    beta_all = 1.0 / (1.0 + jnp.exp(-ab))

    tri = fmask_ref[0].astype(BF16)

    for c in range(nc):
        r0 = c * C
        g1, g2, g3 = _split3(gtok[r0:r0 + C])
        gc = _dot(tri, g1) + _dot(tri, g2) + _dot(tri, g3)
        gct = gc.T
        for h in range(DN_HEADS):
            un = c * DN_HEADS + h
            c0 = h * DN_DIM
            qh = q_ref[r0:r0 + C, c0:c0 + DN_DIM].astype(F32)
            kh_bf = k_ref[r0:r0 + C, c0:c0 + DN_DIM]
            kh = kh_bf.astype(F32)
            vh = v_ref[r0:r0 + C, c0:c0 + DN_DIM].astype(F32)
            gcol = gc[:, h:h + 1]
            grow = gct[h:h + 1, :]
            beta = beta_all[r0:r0 + C, DN_HEADS + h:DN_HEADS + h + 1]
            decay = jnp.exp(jnp.minimum(gcol - grow, 0.0))
            kb = kh * beta
            a = _dot_nt(jnp.concatenate([kb, qh], axis=0).astype(BF16), kh_bf)
            Lm = a[:C] * decay * fmask_ref[1]
            l_ref[un] = Lm.astype(BF16)
            t_ref[un] = (fmask_ref[2] - Lm * fmask_ref[3]).astype(BF16)
            qk_ref[un] = (a[C:] * decay * fmask_ref[0]).astype(BF16)
            eg = jnp.exp(gcol)
            rhs_ref[un] = jnp.concatenate([vh * beta, kb * eg], axis=1).astype(BF16)
            qa_ref[un, 0:C, :] = (qh * eg).astype(BF16)
            g_last = gcol[C - 1:C, :]
            kdt_ref[un] = (kh * jnp.exp(g_last - gcol)).T.astype(BF16)
            egl_ref[un] = jnp.broadcast_to(jnp.exp(g_last), (8, LANES))

    nu = nc * DN_HEADS
    for lv in range(n_levels):
        for g0 in range(0, nu, group):
            units = list(range(g0, min(g0 + group, nu)))
            ps = [_dot(l_ref[un] * lmask_ref[lv], t_ref[un]).astype(BF16) for un in units]
            for un, p in zip(units, ps):
                t_ref[un] = t_ref[un] - _dot(t_ref[un], p).astype(BF16)
    for un in range(nu):
        uw = _dot(t_ref[un], rhs_ref[un]).astype(BF16)
        r = _dot(jnp.concatenate([qk_ref[un], kdt_ref[un]], axis=0), uw)
        ou_ref[un] = r[:C, :DN_DIM]
        qa_ref[un, 0:C, :] = (qa_ref[un, 0:C, :].astype(F32) - r[:C, DN_DIM:]).astype(BF16)
        qa_ref[un, C:C + DN_DIM, :] = (-r[C:, DN_DIM:]).astype(BF16)
        ku_ref[un] = r[C:, :DN_DIM]

    for c in range(nc):
        r0 = c * C
        for h in range(DN_HEADS):
            un = c * DN_HEADS + h
            c0 = h * DN_DIM
            state = state_ref[h]
            r = _dot(qa_ref[un], state.astype(BF16))
            o = ou_ref[un] + r[:C]
            state_ref[h] = state * egl_ref[un][0:1, :] + r[C:] + ku_ref[un]
            zh = z_ref[r0:r0 + C, c0:c0 + DN_DIM]
            o = o * lax.rsqrt(jnp.mean(o * o, axis=-1, keepdims=True) + EPS) * onw_ref[...]
            o = o * _silu(zh)
            o_ref[r0:r0 + C, c0:c0 + DN_DIM] = o.astype(BF16)


def _gdn(gq, gk, gv, z, ab, alog, dtb, onw, B, S, tb):
    nb = S // tb
    C = GDN_CHUNK
    nu = (tb // C) * DN_HEADS
    const = lambda b, i: (0, 0)
    row = lambda b, i: (b * nb + i, 0)
    return pl.pallas_call(
        functools.partial(_gdn_body, tb=tb, group=16),
        grid=(B, nb),
        in_specs=[
            pl.BlockSpec((tb, HEAD_COLS), row),
            pl.BlockSpec((tb, HEAD_COLS), row),
            pl.BlockSpec((tb, HEAD_COLS), row),
            pl.BlockSpec((tb, HEAD_COLS), row),
            pl.BlockSpec((tb, LANES), row),
            pl.BlockSpec((1, LANES), const),
            pl.BlockSpec((1, LANES), const),
            pl.BlockSpec((1, LANES), const),
        ],
        out_specs=pl.BlockSpec((tb, HEAD_COLS), row),
        out_shape=jax.ShapeDtypeStruct((B * S, HEAD_COLS), BF16),
        scratch_shapes=[
            pltpu.VMEM((DN_HEADS, DN_DIM, DN_DIM), F32),
            pltpu.VMEM((nu, C, C), BF16),
            pltpu.VMEM((nu, C, C), BF16),
            pltpu.VMEM((nu, C, 2 * DN_DIM), BF16),
            pltpu.VMEM((nu, C, DN_DIM), F32),
            pltpu.VMEM((nu, DN_DIM, DN_DIM), F32),
            pltpu.VMEM((nu, C + DN_DIM, DN_DIM), BF16),
            pltpu.VMEM((nu, C, C), BF16),
            pltpu.VMEM((nu, DN_DIM, C), BF16),
            pltpu.VMEM((nu, 8, LANES), F32),
            pltpu.VMEM((4, C, C), F32),
            pltpu.VMEM((C.bit_length() - 2, C, C), BF16),
        ],
        compiler_params=pltpu.CompilerParams(
            dimension_semantics=("arbitrary", "arbitrary"), vmem_limit_bytes=VMEM_LIMIT),
        name="gated_deltanet",
    )(gq, gk, gv, z, ab, alog, dtb, onw)


def _mlp_body(x_ref, ma_ref, mb_ref, woa_ref, wob_ref, n2_ref, wup_ref, wdn_ref, o_ref, *, ff_chunk):
    x1 = x_ref[...] + _dot(ma_ref[...], woa_ref[...]) + _dot(mb_ref[...], wob_ref[...])
    ms = jnp.mean(x1 * x1, axis=-1, keepdims=True)
    h = (x1 * lax.rsqrt(ms + EPS) * n2_ref[...]).astype(BF16)
    mlp = None
    for c0 in range(0, wup_ref.shape[1], ff_chunk):
        up = jnp.maximum(_dot(h, wup_ref[:, c0:c0 + ff_chunk]), 0.0)
        down = _dot((up * up).astype(BF16), wdn_ref[c0:c0 + ff_chunk, :])
        mlp = down if mlp is None else mlp + down
    o_ref[...] = x1 + mlp


def _mlp(x2, mix_a, mix_b, woa, wob, n2, wup, wdn, tm, ff_chunk):
    T, D = x2.shape
    const = lambda i: (0, 0)
    row = lambda i: (i, 0)
    return pl.pallas_call(
        functools.partial(_mlp_body, ff_chunk=ff_chunk),
        grid=(T // tm,),
        in_specs=[
            pl.BlockSpec((tm, D), row),
            pl.BlockSpec((tm, HEAD_COLS), row),
            pl.BlockSpec((tm, HEAD_COLS), row),
            pl.BlockSpec(woa.shape, const),
            pl.BlockSpec(wob.shape, const),
            pl.BlockSpec((1, D), const),
            pl.BlockSpec(wup.shape, const),
            pl.BlockSpec(wdn.shape, const),
        ],
        out_specs=pl.BlockSpec((tm, D), row),
        out_shape=jax.ShapeDtypeStruct((T, D), F32),
        compiler_params=pltpu.CompilerParams(
            dimension_semantics=("arbitrary",), vmem_limit_bytes=VMEM_LIMIT),
        name="outproj_mlp",
    )(x2, mix_a, mix_b, woa, wob, n2, wup, wdn)


def _layer(x2, B, S, l, norm1_w, w_in, lambda_q1, lambda_k1, lambda_q2, lambda_k2, q_norm_w, k_norm_w,
           da_out_norm_w, conv_w, A_log, dt_bias, dn_out_norm_w, w_out, norm2_w, w_up, w_down,
           tm, tq, tb, ff_chunk):
    D = x2.shape[1]
    tile2 = lambda v: jnp.concatenate([v, v]).reshape(1, LANES).astype(F32)
    pad_lanes = lambda v: jnp.pad(v.astype(F32), (0, LANES - v.shape[0])).reshape(1, LANES)

    q, k, v, gq, gk, gv, z, ab = _inproj(
        x2, norm1_w[l].reshape(1, D).astype(F32), w_in.astype(F32), l,
        tile2(q_norm_w[l]) * (DA_QK_DIM ** 0.5 * DA_QK_DIM ** -0.5 * LOG2E), tile2(k_norm_w[l]) * DA_QK_DIM ** 0.5,
        0.5 * conv_w[l].astype(F32), S, tm)

    lam_init = 0.8 - 0.6 * math.exp(-0.3 * l)
    vec = lambda p: p[l].reshape(1, DA_QK_DIM).astype(F32)
    bound = 8.0 * jnp.max(jnp.abs(q_norm_w[l].astype(F32))) * jnp.max(jnp.abs(k_norm_w[l].astype(F32)))
    attn_args = ((bound * LOG2E).reshape(1), q, k, v, vec(lambda_q1), vec(lambda_k1), vec(lambda_q2),
                 vec(lambda_k2), da_out_norm_w[l].reshape(1, LANES).astype(F32))
    mix_a = lax.cond(bound <= MAX_FIXED_SHIFT,
                     lambda a: _attention(*a, B, S, tq, lam_init, online=False),
                     lambda a: _attention(*a, B, S, tq, lam_init, online=True), attn_args)

    mix_b = _gdn(gq, gk, gv, z, ab, pad_lanes(A_log[l]), pad_lanes(dt_bias[l]),
                 dn_out_norm_w[l].reshape(1, LANES).astype(F32), B, S, tb)

    wo = w_out[l].astype(BF16)
    return _mlp(x2, mix_a, mix_b, wo[:HEAD_COLS], wo[HEAD_COLS:], norm2_w[l].reshape(1, D).astype(F32),
                w_up[l].astype(BF16), w_down[l].astype(BF16), tm, ff_chunk)


def kernel(x, norm1_w, w_in, lambda_q1, lambda_k1, lambda_q2, lambda_k2, q_norm_w, k_norm_w, da_out_norm_w,
           conv_w, A_log, dt_bias, dn_out_norm_w, w_out, norm2_w, w_up, w_down):
    B, S, D = x.shape
    x2 = x.reshape(B * S, D)
    tm = min(512, S)
    tq = min(512, S)
    tb = min(1024, S)
    for l in range(w_in.shape[0]):
        x2 = _layer(x2, B, S, l, norm1_w, w_in, lambda_q1, lambda_k1, lambda_q2, lambda_k2, q_norm_w,
                    k_norm_w, da_out_norm_w, conv_w, A_log, dt_bias, dn_out_norm_w, w_out, norm2_w,
                    w_up, w_down, tm, tq, tb, 1024)
    return x2.reshape(B, S, D)
```

```python
import functools
import math

import jax
import jax.numpy as jnp
from jax import lax
from jax.experimental import pallas as pl
from jax.experimental.pallas import tpu as pltpu

F32 = jnp.float32
BF16 = jnp.bfloat16
EPS = 1e-6
LANES = 128
SUBLANES = 8
NEG_BIG = -1e30
LOG2E = math.log2(math.e)
MAX_FIXED_SHIFT = 40.0

DA_HEADS = 4
DA_QK_DIM = 64
DN_HEADS = 4
DN_DIM = 128
CONV_WIDTH = 4
HEAD_COLS = 512
GDN_CHUNK = 128
VMEM_LIMIT = 56 * 1024 * 1024


def _dot(a, b):
    return jnp.dot(a, b, preferred_element_type=F32)


def _dot_nt(a, b):
    return lax.dot_general(a, b, (((1,), (1,)), ((), ())), preferred_element_type=F32)


def _silu(x):
    h = 0.5 * x
    return h + h * jnp.tanh(h)


def _split3(x):
    x1 = x.astype(BF16)
    r1 = x - x1.astype(F32)
    x2 = r1.astype(BF16)
    x3 = (r1 - x2.astype(F32)).astype(BF16)
    return x1, x2, x3


def _inproj_body(x_ref, n1_ref, w_ref, qnw_ref, knw_ref, cw_ref,
                 q_out, k_out, v_out, gq_out, gk_out, gv_out, z_out, ab_out, tail_ref, wb_ref,
                 *, tiles_per_seq):
    tm = x_ref.shape[0]
    piece = 2 * LANES
    wa0, wdn0, wz0, wab0 = 0, 3 * HEAD_COLS, 6 * HEAD_COLS, 7 * HEAD_COLS
    n_in = w_ref.shape[1]

    @pl.when(pl.program_id(0) == 0)
    def _():
        for c0 in range(0, wab0, HEAD_COLS):
            wb_ref[:, c0:c0 + HEAD_COLS] = w_ref[:, c0:c0 + HEAD_COLS].astype(BF16)
        wb_ref[:, wab0:wab0 + LANES] = jnp.zeros((wb_ref.shape[0], LANES), BF16)
        wb_ref[:, wab0:n_in] = w_ref[:, wab0:n_in].astype(BF16)

    x = x_ref[...]
    ms = jnp.mean(x * x, axis=-1, keepdims=True)
    h = (x * lax.rsqrt(ms + EPS) * n1_ref[...]).astype(BF16)

    @pl.when(pl.program_id(0) % tiles_per_seq == 0)
    def _():
        tail_ref[...] = jnp.zeros_like(tail_ref)

    def deltanet_piece(p0):
        d = _dot(h, wb_ref[:, wdn0 + p0:wdn0 + p0 + piece])
        tail = tail_ref[:, p0:p0 + piece]
        tail_ref[:, p0:p0 + piece] = d[tm - SUBLANES:tm, :]
        for cc in range(0, piece, LANES):
            c0 = p0 + cc
            cb = c0 // LANES
            for r0 in range(0, tm, GDN_CHUNK):
                halo = tail[:, cc:cc + LANES] if r0 == 0 else d[r0 - SUBLANES:r0, cc:cc + LANES]
                a = jnp.concatenate([halo, d[r0:r0 + GDN_CHUNK, cc:cc + LANES]], axis=0)
                tap = lambda w: cw_ref[w:w + 1, c0:c0 + LANES]
                a1 = pltpu.roll(a, 1, axis=0)
                u = tap(1) * a + tap(0) * a1
                y = tap(3) * a[SUBLANES:] + tap(2) * a1[SUBLANES:] + pltpu.roll(u, 2, axis=0)[SUBLANES:]
                y = y + y * jnp.tanh(y)
                if cb < DN_HEADS:
                    y = y * (lax.rsqrt(jnp.sum(y * y, axis=-1, keepdims=True) + EPS) * (DN_DIM ** -0.5))
                elif cb < 2 * DN_HEADS:
                    y = y * lax.rsqrt(jnp.sum(y * y, axis=-1, keepdims=True) + EPS)
                out = (gq_out, gk_out, gv_out)[cb // DN_HEADS]
                c1 = (cb % DN_HEADS) * LANES
                out[r0:r0 + GDN_CHUNK, c1:c1 + LANES] = y.astype(BF16)

    lo = lax.broadcasted_iota(jnp.int32, (tm, LANES), 1) < DA_QK_DIM

    def qk_norm(t, w8):
        sq = t * t
        s_lo = jnp.sum(jnp.where(lo, sq, 0.0), axis=-1, keepdims=True)
        s_hi = jnp.sum(jnp.where(lo, 0.0, sq), axis=-1, keepdims=True)
        return t * lax.rsqrt(jnp.where(lo, s_lo, s_hi) + DA_QK_DIM * EPS) * w8

    def attention_piece(p0):
        t = _dot(h, wb_ref[:, wa0 + p0:wa0 + p0 + piece])
        for cc in range(0, piece, LANES):
            th = t[:, cc:cc + LANES]
            c1 = p0 % HEAD_COLS + cc
            if p0 < HEAD_COLS:
                q_out[:, c1:c1 + LANES] = qk_norm(th, qnw_ref[...]).astype(BF16)
            elif p0 < 2 * HEAD_COLS:
                k_out[:, c1:c1 + LANES] = qk_norm(th, knw_ref[...]).astype(BF16)
            else:
                v_out[:, c1:c1 + LANES] = th.astype(BF16)

    for p0 in range(0, 3 * HEAD_COLS, piece):
        deltanet_piece(p0)
        attention_piece(p0)
    for p0 in range(0, HEAD_COLS, piece):
        z_out[:, p0:p0 + piece] = _dot(h, wb_ref[:, wz0 + p0:wz0 + p0 + piece])
    ab_out[...] = _dot(h, wb_ref[:, wab0:wab0 + LANES])


def _inproj(x2, n1, w_in, l, qnw, knw, cw, S, tm):
    T, D = x2.shape
    W = cw.shape[1]
    n_in = w_in.shape[2]
    const = lambda i: (0, 0)
    row = lambda i: (i, 0)
    slab = pl.BlockSpec((tm, HEAD_COLS), row)
    slab_bf16 = jax.ShapeDtypeStruct((T, HEAD_COLS), BF16)
    return pl.pallas_call(
        functools.partial(_inproj_body, tiles_per_seq=S // tm),
        grid=(T // tm,),
        in_specs=[
            pl.BlockSpec((tm, D), row),
            pl.BlockSpec((1, D), const),
            pl.BlockSpec((None, D, n_in), lambda i: (l, 0, 0)),
            pl.BlockSpec((1, LANES), const),
            pl.BlockSpec((1, LANES), const),
            pl.BlockSpec((CONV_WIDTH, W), const),
        ],
        out_specs=[slab, slab, slab, slab, slab, slab, slab, pl.BlockSpec((tm, LANES), row)],
        out_shape=[slab_bf16, slab_bf16, slab_bf16, slab_bf16, slab_bf16, slab_bf16,
                   jax.ShapeDtypeStruct((T, HEAD_COLS), F32),
                   jax.ShapeDtypeStruct((T, LANES), F32)],
        scratch_shapes=[
            pltpu.VMEM((SUBLANES, W), F32),
            pltpu.VMEM((D, 7 * HEAD_COLS + LANES), BF16),
        ],
        compiler_params=pltpu.CompilerParams(
            dimension_semantics=("arbitrary",), vmem_limit_bytes=VMEM_LIMIT),
        name="inproj",
    )(x2, n1, w_in, qnw, knw, cw)


def _stack_maps(q_ref, qz_ref, tq):
    q = q_ref[...]
    lane = lax.broadcasted_iota(jnp.int32, (tq, LANES), 1)
    zero = jnp.zeros_like(q)
    qz_ref[0:tq, :] = jnp.where(lane < DA_QK_DIM, q, zero)
    qz_ref[tq:2 * tq, :] = jnp.where(lane < DA_QK_DIM, zero, q)


def _causal_mask(tq):
    rows = lax.broadcasted_iota(jnp.int32, (2 * tq, tq), 0)
    cols = lax.broadcasted_iota(jnp.int32, (2 * tq, tq), 1)
    return jnp.where(rows >= tq, rows - tq, rows) >= cols


def _attn_finish(acc, l, lq1_ref, lk1_ref, lq2_ref, lk2_ref, onw_ref, o_ref, tq, lam_init):
    lam = (jnp.exp(jnp.sum(lq1_ref[...] * lk1_ref[...], axis=-1, keepdims=True))
           - jnp.exp(jnp.sum(lq2_ref[...] * lk2_ref[...], axis=-1, keepdims=True)) + lam_init)
    o = acc[:tq] / l[:tq] - lam * (acc[tq:] / l[tq:])
    ms = jnp.mean(o * o, axis=-1, keepdims=True)
    o = o * lax.rsqrt(ms + EPS) * onw_ref[...] * (1.0 - lam_init)
    o_ref[...] = o.astype(BF16)


def _attn_shift_body(shift_ref, q_ref, k_ref, v_ref, lq1_ref, lk1_ref, lq2_ref, lk2_ref, onw_ref, o_ref,
                     qz_ref, p_ref, l_ref, acc_ref, *, tq, lam_init, unroll, heads):
    i = pl.program_id(2)
    shift = shift_ref[0]
    cols = lambda hh: slice(hh * LANES, (hh + 1) * LANES)
    rows = lambda j: pl.ds(pl.multiple_of(j * tq, tq), tq)

    def probs(hh, j, diagonal):
        s = _dot_nt(qz_ref[hh], k_ref[rows(j), cols(hh)]) - shift
        if diagonal:
            s = jnp.where(_causal_mask(tq), s, NEG_BIG)
        p = jnp.exp2(s)
        psum = p[:, 0:LANES]
        for c0 in range(LANES, tq, LANES):
            psum = psum + p[:, c0:c0 + LANES]
        return p.astype(BF16), psum

    for hh in range(heads):
        _stack_maps(q_ref.at[:, cols(hh)], qz_ref.at[hh], tq)
        p, psum = probs(hh, i, True)
        p_ref[hh] = p
        l_ref[hh] = psum
        acc_ref[hh] = jnp.zeros((2 * tq, LANES), F32)

    def pv_prev(hh, j):
        return _dot(p_ref[hh], v_ref[rows(jnp.where(j == 0, i, j - 1)), cols(hh)])

    def steps(j0, n):
        acc = [pv_prev(hh, j0) for hh in range(heads)]
        psum = [None] * heads
        for u in range(n):
            for hh in range(heads):
                p, ps = probs(hh, j0 + u, False)
                psum[hh] = ps if psum[hh] is None else psum[hh] + ps
                if u + 1 < n:
                    acc[hh] = acc[hh] + _dot(p, v_ref[rows(j0 + u), cols(hh)])
                else:
                    p_ref[hh] = p
        for hh in range(heads):
            acc_ref[hh] += acc[hh]
            l_ref[hh] += psum[hh]

    start = 0
    n = unroll
    while n >= 1:
        trips = (i - start) // n

        def group(t, _, start=start, n=n):
            steps(start + t * n, n)
            return 0

        lax.fori_loop(0, trips, group, 0)
        start = start + trips * n
        n //= 2
    for hh in range(heads):
        acc = acc_ref[hh] + pv_prev(hh, i)
        l = jnp.sum(l_ref[hh], axis=-1, keepdims=True)
        _attn_finish(acc, l, lq1_ref, lk1_ref, lq2_ref, lk2_ref, onw_ref, o_ref.at[:, cols(hh)], tq, lam_init)


def _attn_online_body(shift_ref, q_ref, k_ref, v_ref, lq1_ref, lk1_ref, lq2_ref, lk2_ref, onw_ref, o_ref,
                      qz_ref, *, tq, lam_init):
    del shift_ref
    i = pl.program_id(2)
    _stack_maps(q_ref, qz_ref, tq)

    def step(j, carry, diagonal):
        m, l, acc = carry
        r0 = pl.multiple_of(j * tq, tq)
        s = _dot_nt(qz_ref[...], k_ref[pl.ds(r0, tq), :])
        if diagonal:
            s = jnp.where(_causal_mask(tq), s, NEG_BIG)
        m_new = jnp.maximum(m, jnp.max(s, axis=-1, keepdims=True))
        p = jnp.exp2(s - m_new)
        alpha = jnp.exp2(m - m_new)
        l = alpha * l + jnp.sum(p, axis=-1, keepdims=True)
        acc = alpha * acc + _dot(p.astype(BF16), v_ref[pl.ds(r0, tq), :])
        return m_new, l, acc

    init = (jnp.full((2 * tq, 1), NEG_BIG, F32), jnp.zeros((2 * tq, 1), F32),
            jnp.zeros((2 * tq, LANES), F32))
    carry = lax.fori_loop(0, i, lambda j, c: step(j, c, False), init)
    _, l, acc = step(i, carry, True)
    _attn_finish(acc, l, lq1_ref, lk1_ref, lq2_ref, lk2_ref, onw_ref, o_ref, tq, lam_init)


def _attention(shift, q, k, v, lq1, lk1, lq2, lk2, onw, B, S, tq, lam_init, online):
    nq = S // tq
    vec = pl.BlockSpec((1, DA_QK_DIM), lambda b, h, i: (0, 0))
    if online:
        heads = 1
        body = functools.partial(_attn_online_body, tq=tq, lam_init=lam_init)
        scratch = [pltpu.VMEM((2 * tq, LANES), BF16)]
    else:
        heads = DA_HEADS
        body = functools.partial(_attn_shift_body, tq=tq, lam_init=lam_init, unroll=4, heads=heads)
        scratch = [pltpu.VMEM((heads, 2 * tq, LANES), BF16),
                   pltpu.VMEM((heads, 2 * tq, tq), BF16),
                   pltpu.VMEM((heads, 2 * tq, LANES), F32),
                   pltpu.VMEM((heads, 2 * tq, LANES), F32)]
    width = heads * LANES
    return pl.pallas_call(
        body,
        grid=(B, DA_HEADS // heads, nq),
        in_specs=[
            pl.BlockSpec(memory_space=pltpu.SMEM),
            pl.BlockSpec((tq, width), lambda b, h, i: (b * nq + i, h)),
            pl.BlockSpec((S, width), lambda b, h, i: (b, h)),
            pl.BlockSpec((S, width), lambda b, h, i: (b, h)),
            vec, vec, vec, vec,
            pl.BlockSpec((1, LANES), lambda b, h, i: (0, 0)),
        ],
        out_specs=pl.BlockSpec((tq, width), lambda b, h, i: (b * nq + i, h)),
        out_shape=jax.ShapeDtypeStruct((B * S, HEAD_COLS), BF16),
        scratch_shapes=scratch,
        compiler_params=pltpu.CompilerParams(
            dimension_semantics=("arbitrary", "arbitrary", "arbitrary"), vmem_limit_bytes=VMEM_LIMIT),
        name="diff_attention_online" if online else "diff_attention",
    )(shift, q, k, v, lq1, lk1, lq2, lk2, onw)


def _merge_masks(row, col, n):
    masks = []
    b = 2
    while b < n:
        masks.append(((row // b) == (col // b) + 1) & ((row // (2 * b)) == (col // (2 * b))))
        b *= 2
    return masks


def _gdn_body(q_ref, k_ref, v_ref, z_ref, ab_ref, alog_ref, dtb_ref, onw_ref, o_ref,
              state_ref, l_ref, t_ref, rhs_ref, ou_ref, ku_ref, qa_ref, qk_ref, kdt_ref, egl_ref,
              fmask_ref, bmask_ref, lmask_ref, *, tb, group):
    C = GDN_CHUNK
    nc = tb // C
    blk = pl.program_id(1)
    n_levels = lmask_ref.shape[0]

    @pl.when(blk == 0)
    def _():
        state_ref[...] = jnp.zeros_like(state_ref)
        row = lax.broadcasted_iota(jnp.int32, (C, C), 0)
        col = lax.broadcasted_iota(jnp.int32, (C, C), 1)
        fmask_ref[0] = (row >= col).astype(F32)
        fmask_ref[1] = (row > col).astype(F32)
        bmask_ref[0] = (row == col).astype(BF16)
        bmask_ref[1] = ((row == col + 1) & (row % 2 == 1)).astype(BF16)
        for lv, m in enumerate(_merge_masks(row, col, C)):
            lmask_ref[lv] = m.astype(BF16)

    ab = ab_ref[...]
    sp_in = ab + dtb_ref[...]
    softplus = jnp.maximum(sp_in, 0.0) + jnp.log(1.0 + jnp.exp(-jnp.abs(sp_in)))
    gtok = (-LOG2E) * jnp.exp(alog_ref[...]) * softplus
    gtok = jnp.where(lax.broadcasted_iota(jnp.int32, (tb, LANES), 1) < DN_HEADS, gtok, 0.0)
    beta_all = 1.0 / (1.0 + jnp.exp(-ab))

    tri = fmask_ref[0].astype(BF16)

    for c in range(nc):
        r0 = c * C
        g1, g2, g3 = _split3(gtok[r0:r0 + C])
        gc = _dot(tri, g1) + _dot(tri, g2) + _dot(tri, g3)
        gct = gc.T
        for h in range(DN_HEADS):
            un = c * DN_HEADS + h
            c0 = h * DN_DIM
            qh_bf = q_ref[r0:r0 + C, c0:c0 + DN_DIM]
            qh = qh_bf.astype(F32)
            kh_bf = k_ref[r0:r0 + C, c0:c0 + DN_DIM]
            kh = kh_bf.astype(F32)
            vh = v_ref[r0:r0 + C, c0:c0 + DN_DIM].astype(F32)
            gcol = gc[:, h:h + 1]
            grow = gct[h:h + 1, :]
            beta = beta_all[r0:r0 + C, DN_HEADS + h:DN_HEADS + h + 1]
            decay = jnp.exp2(jnp.minimum(gcol - grow, 0.0))
            kb = kh * beta
            a = _dot_nt(jnp.concatenate([kb.astype(BF16), qh_bf], axis=0), kh_bf)
            Lm = (a[:C] * decay * fmask_ref[1]).astype(BF16)
            l_ref[un] = Lm
            t_ref[un] = bmask_ref[0] - Lm * bmask_ref[1]
            qk_ref[un] = (a[C:] * decay * fmask_ref[0]).astype(BF16)
            eg = jnp.exp2(gcol)
            rhs_ref[un] = jnp.concatenate([vh * beta, kb * eg], axis=1).astype(BF16)
            qa_ref[un, 0:C, :] = (qh * eg).astype(BF16)
            g_last = grow[:, C - 1:C]
            kdt_ref[un] = (kh.T * jnp.exp2(g_last - grow)).astype(BF16)
            egl_ref[un] = jnp.broadcast_to(jnp.exp2(g_last), (SUBLANES, LANES))

    nu = nc * DN_HEADS
    for lv in range(n_levels):
        for g0 in range(0, nu, group):
            units = list(range(g0, min(g0 + group, nu)))
            ps = [_dot(l_ref[un] * lmask_ref[lv], t_ref[un]).astype(BF16) for un in units]
            for un, p in zip(units, ps):
                t_ref[un] = t_ref[un] - _dot(t_ref[un], p).astype(BF16)
    for g0 in range(0, nu, group):
        units = list(range(g0, min(g0 + group, nu)))
        uws = [_dot(t_ref[un], rhs_ref[un]).astype(BF16) for un in units]
        for un, uw in zip(units, uws):
            r = _dot(jnp.concatenate([qk_ref[un], kdt_ref[un]], axis=0), uw)
            ou_ref[un] = r[:C, :DN_DIM]
            qa_ref[un, 0:C, :] = (qa_ref[un, 0:C, :].astype(F32) - r[:C, DN_DIM:]).astype(BF16)
            qa_ref[un, C:C + DN_DIM, :] = (-r[C:, DN_DIM:]).astype(BF16)
            ku_ref[un] = r[C:, :DN_DIM]

    for c in range(nc):
        r0 = c * C
        for h in range(DN_HEADS):
            un = c * DN_HEADS + h
            c0 = h * DN_DIM
            state = state_ref[h]
            r = _dot(qa_ref[un], state.astype(BF16))
            o = ou_ref[un] + r[:C]
            state_ref[h] = state * egl_ref[un][0:1, :] + r[C:] + ku_ref[un]
            zh = z_ref[r0:r0 + C, c0:c0 + DN_DIM]
            o = o * lax.rsqrt(jnp.mean(o * o, axis=-1, keepdims=True) + EPS) * onw_ref[...]
            o = o * _silu(zh)
            o_ref[r0:r0 + C, c0:c0 + DN_DIM] = o.astype(BF16)


def _gdn(gq, gk, gv, z, ab, alog, dtb, onw, B, S, tb):
    nb = S // tb
    C = GDN_CHUNK
    nu = (tb // C) * DN_HEADS
    const = lambda b, i: (0, 0)
    row = lambda b, i: (b * nb + i, 0)
    return pl.pallas_call(
        functools.partial(_gdn_body, tb=tb, group=16),
        grid=(B, nb),
        in_specs=[
            pl.BlockSpec((tb, HEAD_COLS), row),
            pl.BlockSpec((tb, HEAD_COLS), row),
            pl.BlockSpec((tb, HEAD_COLS), row),
            pl.BlockSpec((tb, HEAD_COLS), row),
            pl.BlockSpec((tb, LANES), row),
            pl.BlockSpec((1, LANES), const),
            pl.BlockSpec((1, LANES), const),
            pl.BlockSpec((1, LANES), const),
        ],
        out_specs=pl.BlockSpec((tb, HEAD_COLS), row),
        out_shape=jax.ShapeDtypeStruct((B * S, HEAD_COLS), BF16),
        scratch_shapes=[
            pltpu.VMEM((DN_HEADS, DN_DIM, DN_DIM), F32),
            pltpu.VMEM((nu, C, C), BF16),
            pltpu.VMEM((nu, C, C), BF16),
            pltpu.VMEM((nu, C, 2 * DN_DIM), BF16),
            pltpu.VMEM((nu, C, DN_DIM), F32),
            pltpu.VMEM((nu, DN_DIM, DN_DIM), F32),
            pltpu.VMEM((nu, C + DN_DIM, DN_DIM), BF16),
            pltpu.VMEM((nu, C, C), BF16),
            pltpu.VMEM((nu, DN_DIM, C), BF16),
            pltpu.VMEM((nu, SUBLANES, LANES), F32),
            pltpu.VMEM((2, C, C), F32),
            pltpu.VMEM((2, C, C), BF16),
            pltpu.VMEM((C.bit_length() - 2, C, C), BF16),
        ],
        compiler_params=pltpu.CompilerParams(
            dimension_semantics=("arbitrary", "arbitrary"), vmem_limit_bytes=VMEM_LIMIT),
        name="gated_deltanet",
    )(gq, gk, gv, z, ab, alog, dtb, onw)


def _mlp_body(x_ref, ma_ref, mb_ref, woa_ref, wob_ref, n2_ref, wup_ref, wdn_ref, o_ref, *, ff_chunk):
    x1 = x_ref[...] + _dot(ma_ref[...], woa_ref[...]) + _dot(mb_ref[...], wob_ref[...])
    ms = jnp.mean(x1 * x1, axis=-1, keepdims=True)
    h = (x1 * lax.rsqrt(ms + EPS) * n2_ref[...]).astype(BF16)
    mlp = None
    for c0 in range(0, wup_ref.shape[1], ff_chunk):
        up = jnp.maximum(_dot(h, wup_ref[:, c0:c0 + ff_chunk]), 0.0)
        down = _dot((up * up).astype(BF16), wdn_ref[c0:c0 + ff_chunk, :])
        mlp = down if mlp is None else mlp + down
    o_ref[...] = x1 + mlp


def _mlp(x2, mix_a, mix_b, woa, wob, n2, wup, wdn, tm, ff_chunk):
    T, D = x2.shape
    const = lambda i: (0, 0)
    row = lambda i: (i, 0)
    return pl.pallas_call(
        functools.partial(_mlp_body, ff_chunk=ff_chunk),
        grid=(T // tm,),
        in_specs=[
            pl.BlockSpec((tm, D), row),
            pl.BlockSpec((tm, HEAD_COLS), row),
            pl.BlockSpec((tm, HEAD_COLS), row),
            pl.BlockSpec(woa.shape, const),
            pl.BlockSpec(wob.shape, const),
            pl.BlockSpec((1, D), const),
            pl.BlockSpec(wup.shape, const),
            pl.BlockSpec(wdn.shape, const),
        ],
        out_specs=pl.BlockSpec((tm, D), row),
        out_shape=jax.ShapeDtypeStruct((T, D), F32),
        compiler_params=pltpu.CompilerParams(
            dimension_semantics=("arbitrary",), vmem_limit_bytes=VMEM_LIMIT),
        name="outproj_mlp",
    )(x2, mix_a, mix_b, woa, wob, n2, wup, wdn)


def _layer(x2, B, S, l, norm1_w, w_in, lambda_q1, lambda_k1, lambda_q2, lambda_k2, q_norm_w, k_norm_w,
           da_out_norm_w, conv_w, A_log, dt_bias, dn_out_norm_w, w_out, norm2_w, w_up, w_down,
           tm, tq, tb, ff_chunk):
    D = x2.shape[1]
    tile2 = lambda v: jnp.concatenate([v, v]).reshape(1, LANES).astype(F32)
    pad_lanes = lambda v: jnp.pad(v.astype(F32), (0, LANES - v.shape[0])).reshape(1, LANES)

    q, k, v, gq, gk, gv, z, ab = _inproj(
        x2, norm1_w[l].reshape(1, D).astype(F32), w_in.astype(F32), l,
        tile2(q_norm_w[l]) * LOG2E,
        tile2(k_norm_w[l]) * DA_QK_DIM ** 0.5,
        0.5 * conv_w[l].astype(F32), S, tm)

    lam_init = 0.8 - 0.6 * math.exp(-0.3 * l)
    vec = lambda p: p[l].reshape(1, DA_QK_DIM).astype(F32)
    bound = 8.0 * jnp.max(jnp.abs(q_norm_w[l].astype(F32))) * jnp.max(jnp.abs(k_norm_w[l].astype(F32)))
    attn_args = ((bound * LOG2E).reshape(1), q, k, v, vec(lambda_q1), vec(lambda_k1), vec(lambda_q2),
                 vec(lambda_k2), da_out_norm_w[l].reshape(1, LANES).astype(F32))
    mix_a = lax.cond(bound <= MAX_FIXED_SHIFT,
                     lambda a: _attention(*a, B, S, tq, lam_init, online=False),
                     lambda a: _attention(*a, B, S, tq, lam_init, online=True), attn_args)

    mix_b = _gdn(gq, gk, gv, z, ab, pad_lanes(A_log[l]), pad_lanes(dt_bias[l]),
                 dn_out_norm_w[l].reshape(1, LANES).astype(F32), B, S, tb)

    wo = w_out[l].astype(BF16)
    return _mlp(x2, mix_a, mix_b, wo[:HEAD_COLS], wo[HEAD_COLS:], norm2_w[l].reshape(1, D).astype(F32),
                w_up[l].astype(BF16), w_down[l].astype(BF16), tm, ff_chunk)


def kernel(x, norm1_w, w_in, lambda_q1, lambda_k1, lambda_q2, lambda_k2, q_norm_w, k_norm_w, da_out_norm_w,
           conv_w, A_log, dt_bias, dn_out_norm_w, w_out, norm2_w, w_up, w_down):
    B, S, D = x.shape
    x2 = x.reshape(B * S, D)
    tm = min(512, S)
    tq = min(512, S)
    tb = min(1024, S)
    for l in range(w_in.shape[0]):
        x2 = _layer(x2, B, S, l, norm1_w, w_in, lambda_q1, lambda_k1, lambda_q2, lambda_k2, q_norm_w,
                    k_norm_w, da_out_norm_w, conv_w, A_log, dt_bias, dn_out_norm_w, w_out, norm2_w,
                    w_up, w_down, tm, tq, tb, 1024)
    return x2.reshape(B, S, D)
```

```python
import functools
import math

import jax
import jax.numpy as jnp
from jax import lax
from jax.experimental import pallas as pl
from jax.experimental.pallas import tpu as pltpu

F32 = jnp.float32
BF16 = jnp.bfloat16
EPS = 1e-6
LANES = 128
SUBLANES = 8
NEG_BIG = -1e30
LOG2E = math.log2(math.e)
MAX_FIXED_SHIFT = 40.0

DA_HEADS = 4
DA_QK_DIM = 64
DN_HEADS = 4
DN_DIM = 128
CONV_WIDTH = 4
HEAD_COLS = 512
GDN_CHUNK = 128
VMEM_LIMIT = 56 * 1024 * 1024


def _dot(a, b):
    return jnp.dot(a, b, preferred_element_type=F32)


def _dot_nt(a, b):
    return lax.dot_general(a, b, (((1,), (1,)), ((), ())), preferred_element_type=F32)


def _silu(x):
    h = 0.5 * x
    return h + h * jnp.tanh(h)


def _split3(x):
    x1 = x.astype(BF16)
    r1 = x - x1.astype(F32)
    x2 = r1.astype(BF16)
    x3 = (r1 - x2.astype(F32)).astype(BF16)
    return x1, x2, x3


def _inproj_body(x_ref, n1_ref, w_ref, qnw_ref, knw_ref, cw_ref,
                 q_out, k_out, v_out, gq_out, gk_out, gv_out, z_out, ab_out, tail_ref, wb_ref,
                 *, tiles_per_seq):
    tm = x_ref.shape[0]
    piece = 2 * LANES
    wa0, wdn0, wz0, wab0 = 0, 3 * HEAD_COLS, 6 * HEAD_COLS, 7 * HEAD_COLS
    n_in = w_ref.shape[1]

    @pl.when(pl.program_id(0) == 0)
    def _():
        for c0 in range(0, wab0, HEAD_COLS):
            wb_ref[:, c0:c0 + HEAD_COLS] = w_ref[:, c0:c0 + HEAD_COLS].astype(BF16)
        wb_ref[:, wab0:wab0 + LANES] = jnp.zeros((wb_ref.shape[0], LANES), BF16)
        wb_ref[:, wab0:n_in] = w_ref[:, wab0:n_in].astype(BF16)

    x = x_ref[...]
    ms = jnp.mean(x * x, axis=-1, keepdims=True)
    h = (x * lax.rsqrt(ms + EPS) * n1_ref[...]).astype(BF16)

    @pl.when(pl.program_id(0) % tiles_per_seq == 0)
    def _():
        tail_ref[...] = jnp.zeros_like(tail_ref)

    def deltanet_piece(p0):
        d = _dot(h, wb_ref[:, wdn0 + p0:wdn0 + p0 + piece])
        tail = tail_ref[:, p0:p0 + piece]
        tail_ref[:, p0:p0 + piece] = d[tm - SUBLANES:tm, :]
        for cc in range(0, piece, LANES):
            c0 = p0 + cc
            cb = c0 // LANES
            for r0 in range(0, tm, GDN_CHUNK):
                halo = tail[:, cc:cc + LANES] if r0 == 0 else d[r0 - SUBLANES:r0, cc:cc + LANES]
                a = jnp.concatenate([halo, d[r0:r0 + GDN_CHUNK, cc:cc + LANES]], axis=0)
                tap = lambda w: cw_ref[w:w + 1, c0:c0 + LANES]
                a1 = pltpu.roll(a, 1, axis=0)
                u = tap(1) * a + tap(0) * a1
                y = tap(3) * a[SUBLANES:] + tap(2) * a1[SUBLANES:] + pltpu.roll(u, 2, axis=0)[SUBLANES:]
                y = y + y * jnp.tanh(y)
                if cb < DN_HEADS:
                    y = y * (lax.rsqrt(jnp.sum(y * y, axis=-1, keepdims=True) + EPS) * (DN_DIM ** -0.5))
                elif cb < 2 * DN_HEADS:
                    y = y * lax.rsqrt(jnp.sum(y * y, axis=-1, keepdims=True) + EPS)
                out = (gq_out, gk_out, gv_out)[cb // DN_HEADS]
                c1 = (cb % DN_HEADS) * LANES
                out[r0:r0 + GDN_CHUNK, c1:c1 + LANES] = y.astype(BF16)

    lo = lax.broadcasted_iota(jnp.int32, (tm, LANES), 1) < DA_QK_DIM

    def qk_norm(t, w8):
        sq = t * t
        s_lo = jnp.sum(jnp.where(lo, sq, 0.0), axis=-1, keepdims=True)
        s_hi = jnp.sum(jnp.where(lo, 0.0, sq), axis=-1, keepdims=True)
        return t * lax.rsqrt(jnp.where(lo, s_lo, s_hi) + DA_QK_DIM * EPS) * w8

    def attention_piece(p0):
        t = _dot(h, wb_ref[:, wa0 + p0:wa0 + p0 + piece])
        for cc in range(0, piece, LANES):
            th = t[:, cc:cc + LANES]
            c1 = p0 % HEAD_COLS + cc
            if p0 < HEAD_COLS:
                q_out[:, c1:c1 + LANES] = qk_norm(th, qnw_ref[...]).astype(BF16)
            elif p0 < 2 * HEAD_COLS:
                k_out[:, c1:c1 + LANES] = qk_norm(th, knw_ref[...]).astype(BF16)
            else:
                v_out[:, c1:c1 + LANES] = th.astype(BF16)

    for p0 in range(0, 3 * HEAD_COLS, piece):
        deltanet_piece(p0)
        attention_piece(p0)
    for p0 in range(0, HEAD_COLS, piece):
        z_out[:, p0:p0 + piece] = _dot(h, wb_ref[:, wz0 + p0:wz0 + p0 + piece])
    ab_out[...] = _dot(h, wb_ref[:, wab0:wab0 + LANES])


def _inproj(x2, n1, w_in, l, qnw, knw, cw, S, tm):
    T, D = x2.shape
    W = cw.shape[1]
    n_in = w_in.shape[2]
    const = lambda i: (0, 0)
    row = lambda i: (i, 0)
    slab = pl.BlockSpec((tm, HEAD_COLS), row)
    slab_bf16 = jax.ShapeDtypeStruct((T, HEAD_COLS), BF16)
    return pl.pallas_call(
        functools.partial(_inproj_body, tiles_per_seq=S // tm),
        grid=(T // tm,),
        in_specs=[
            pl.BlockSpec((tm, D), row),
            pl.BlockSpec((1, D), const),
            pl.BlockSpec((None, D, n_in), lambda i: (l, 0, 0)),
            pl.BlockSpec((1, LANES), const),
            pl.BlockSpec((1, LANES), const),
            pl.BlockSpec((CONV_WIDTH, W), const),
        ],
        out_specs=[slab, slab, slab, slab, slab, slab, slab, pl.BlockSpec((tm, LANES), row)],
        out_shape=[slab_bf16, slab_bf16, slab_bf16, slab_bf16, slab_bf16, slab_bf16,
                   jax.ShapeDtypeStruct((T, HEAD_COLS), F32),
                   jax.ShapeDtypeStruct((T, LANES), F32)],
        scratch_shapes=[
            pltpu.VMEM((SUBLANES, W), F32),
            pltpu.VMEM((D, 7 * HEAD_COLS + LANES), BF16),
        ],
        compiler_params=pltpu.CompilerParams(
            dimension_semantics=("arbitrary",), vmem_limit_bytes=VMEM_LIMIT),
        name="inproj",
    )(x2, n1, w_in, qnw, knw, cw)


def _stack_maps(q_ref, qz_ref, tq):
    q = q_ref[...]
    lane = lax.broadcasted_iota(jnp.int32, (tq, LANES), 1)
    zero = jnp.zeros_like(q)
    qz_ref[0:tq, :] = jnp.where(lane < DA_QK_DIM, q, zero)
    qz_ref[tq:2 * tq, :] = jnp.where(lane < DA_QK_DIM, zero, q)


def _causal_mask(tq):
    rows = lax.broadcasted_iota(jnp.int32, (2 * tq, tq), 0)
    cols = lax.broadcasted_iota(jnp.int32, (2 * tq, tq), 1)
    return jnp.where(rows >= tq, rows - tq, rows) >= cols


def _attn_finish(acc, l, lq1_ref, lk1_ref, lq2_ref, lk2_ref, onw_ref, o_ref, tq, lam_init):
    lam = (jnp.exp(jnp.sum(lq1_ref[...] * lk1_ref[...], axis=-1, keepdims=True))
           - jnp.exp(jnp.sum(lq2_ref[...] * lk2_ref[...], axis=-1, keepdims=True)) + lam_init)
    o = acc[:tq] / l[:tq] - lam * (acc[tq:] / l[tq:])
    ms = jnp.mean(o * o, axis=-1, keepdims=True)
    o = o * lax.rsqrt(ms + EPS) * onw_ref[...] * (1.0 - lam_init)
    o_ref[...] = o.astype(BF16)


def _attn_shift_body(shift_ref, q_ref, k_ref, v_ref, lq1_ref, lk1_ref, lq2_ref, lk2_ref, onw_ref, o_ref,
                     qz_ref, p_ref, l_ref, acc_ref, *, tq, lam_init, unroll, heads):
    i = pl.program_id(2)
    shift = shift_ref[0]
    cols = lambda hh: slice(hh * LANES, (hh + 1) * LANES)
    rows = lambda j: pl.ds(pl.multiple_of(j * tq, tq), tq)

    def probs(hh, j, diagonal):
        s = _dot_nt(qz_ref[hh], k_ref[rows(j), cols(hh)]) - shift
        if diagonal:
            s = jnp.where(_causal_mask(tq), s, NEG_BIG)
        p = jnp.exp2(s)
        psum = p[:, 0:LANES]
        for c0 in range(LANES, tq, LANES):
            psum = psum + p[:, c0:c0 + LANES]
        return p.astype(BF16), psum

    for hh in range(heads):
        _stack_maps(q_ref.at[:, cols(hh)], qz_ref.at[hh], tq)
        p, psum = probs(hh, i, True)
        p_ref[hh] = p
        l_ref[hh] = psum
        acc_ref[hh] = jnp.zeros((2 * tq, LANES), F32)

    def pv_prev(hh, j):
        return _dot(p_ref[hh], v_ref[rows(jnp.where(j == 0, i, j - 1)), cols(hh)])

    def steps(j0, n):
        acc = [pv_prev(hh, j0) for hh in range(heads)]
        psum = [None] * heads
        for u in range(n):
            for hh in range(heads):
                p, ps = probs(hh, j0 + u, False)
                psum[hh] = ps if psum[hh] is None else psum[hh] + ps
                if u + 1 < n:
                    acc[hh] = acc[hh] + _dot(p, v_ref[rows(j0 + u), cols(hh)])
                else:
                    p_ref[hh] = p
        for hh in range(heads):
            acc_ref[hh] += acc[hh]
            l_ref[hh] += psum[hh]

    start = 0
    n = unroll
    while n >= 1:
        trips = (i - start) // n

        def group(t, _, start=start, n=n):
            steps(start + t * n, n)
            return 0

        lax.fori_loop(0, trips, group, 0)
        start = start + trips * n
        n //= 2
    for hh in range(heads):
        acc = acc_ref[hh] + pv_prev(hh, i)
        l = jnp.sum(l_ref[hh], axis=-1, keepdims=True)
        _attn_finish(acc, l, lq1_ref, lk1_ref, lq2_ref, lk2_ref, onw_ref, o_ref.at[:, cols(hh)], tq, lam_init)


def _attn_online_body(shift_ref, q_ref, k_ref, v_ref, lq1_ref, lk1_ref, lq2_ref, lk2_ref, onw_ref, o_ref,
                      qz_ref, *, tq, lam_init):
    del shift_ref
    i = pl.program_id(2)
    _stack_maps(q_ref, qz_ref, tq)

    def step(j, carry, diagonal):
        m, l, acc = carry
        r0 = pl.multiple_of(j * tq, tq)
        s = _dot_nt(qz_ref[...], k_ref[pl.ds(r0, tq), :])
        if diagonal:
            s = jnp.where(_causal_mask(tq), s, NEG_BIG)
        m_new = jnp.maximum(m, jnp.max(s, axis=-1, keepdims=True))
        p = jnp.exp2(s - m_new)
        alpha = jnp.exp2(m - m_new)
        l = alpha * l + jnp.sum(p, axis=-1, keepdims=True)
        acc = alpha * acc + _dot(p.astype(BF16), v_ref[pl.ds(r0, tq), :])
        return m_new, l, acc

    init = (jnp.full((2 * tq, 1), NEG_BIG, F32), jnp.zeros((2 * tq, 1), F32),
            jnp.zeros((2 * tq, LANES), F32))
    carry = lax.fori_loop(0, i, lambda j, c: step(j, c, False), init)
    _, l, acc = step(i, carry, True)
    _attn_finish(acc, l, lq1_ref, lk1_ref, lq2_ref, lk2_ref, onw_ref, o_ref, tq, lam_init)


def _attention(shift, q, k, v, lq1, lk1, lq2, lk2, onw, B, S, tq, lam_init, online):
    nq = S // tq
    vec = pl.BlockSpec((1, DA_QK_DIM), lambda b, h, i: (0, 0))
    if online:
        heads = 1
        body = functools.partial(_attn_online_body, tq=tq, lam_init=lam_init)
        scratch = [pltpu.VMEM((2 * tq, LANES), BF16)]
    else:
        heads = DA_HEADS
        body = functools.partial(_attn_shift_body, tq=tq, lam_init=lam_init, unroll=4, heads=heads)
        scratch = [pltpu.VMEM((heads, 2 * tq, LANES), BF16),
                   pltpu.VMEM((heads, 2 * tq, tq), BF16),
                   pltpu.VMEM((heads, 2 * tq, LANES), F32),
                   pltpu.VMEM((heads, 2 * tq, LANES), F32)]
    width = heads * LANES
    return pl.pallas_call(
        body,
        grid=(B, DA_HEADS // heads, nq),
        in_specs=[
            pl.BlockSpec(memory_space=pltpu.SMEM),
            pl.BlockSpec((tq, width), lambda b, h, i: (b * nq + i, h)),
            pl.BlockSpec((S, width), lambda b, h, i: (b, h)),
            pl.BlockSpec((S, width), lambda b, h, i: (b, h)),
            vec, vec, vec, vec,
            pl.BlockSpec((1, LANES), lambda b, h, i: (0, 0)),
        ],
        out_specs=pl.BlockSpec((tq, width), lambda b, h, i: (b * nq + i, h)),
        out_shape=jax.ShapeDtypeStruct((B * S, HEAD_COLS), BF16),
        scratch_shapes=scratch,
        compiler_params=pltpu.CompilerParams(
            dimension_semantics=("arbitrary", "arbitrary", "arbitrary"), vmem_limit_bytes=VMEM_LIMIT),
        name="diff_attention_online" if online else "diff_attention",
    )(shift, q, k, v, lq1, lk1, lq2, lk2, onw)


def _merge_masks(row, col, n):
    masks = []
    b = 2
    while b < n:
        masks.append(((row // b) == (col // b) + 1) & ((row // (2 * b)) == (col // (2 * b))))
        b *= 2
    return masks


def _gdn_body(q_ref, k_ref, v_ref, z_ref, ab_ref, alog_ref, dtb_ref, onw_ref, o_ref,
              state_ref, l_ref, t_ref, rhs_ref, ou_ref, ku_ref, qa_ref, qk_ref, kdt_ref, egl_ref,
              fmask_ref, bmask_ref, lmask_ref, *, tb, group):
    C = GDN_CHUNK
    nc = tb // C
    blk = pl.program_id(1)
    n_levels = lmask_ref.shape[0]

    @pl.when(blk == 0)
    def _():
        state_ref[...] = jnp.zeros_like(state_ref)
        row = lax.broadcasted_iota(jnp.int32, (C, C), 0)
        col = lax.broadcasted_iota(jnp.int32, (C, C), 1)
        fmask_ref[0] = (row >= col).astype(F32)
        fmask_ref[1] = (row > col).astype(F32)
        bmask_ref[0] = (row == col).astype(BF16)
        bmask_ref[1] = ((row == col + 1) & (row % 2 == 1)).astype(BF16)
        for lv, m in enumerate(_merge_masks(row, col, C)):
            lmask_ref[lv] = m.astype(BF16)

    ab = ab_ref[...]
    sp_in = ab + dtb_ref[...]
    softplus = jnp.maximum(sp_in, 0.0) + jnp.log(1.0 + jnp.exp(-jnp.abs(sp_in)))
    gtok = (-LOG2E) * jnp.exp(alog_ref[...]) * softplus
    gtok = jnp.where(lax.broadcasted_iota(jnp.int32, (tb, LANES), 1) < DN_HEADS, gtok, 0.0)
    beta_all = 1.0 / (1.0 + jnp.exp(-ab))

    tri = fmask_ref[0].astype(BF16)

    for c in range(nc):
        r0 = c * C
        g1, g2, g3 = _split3(gtok[r0:r0 + C])
        gc = _dot(tri, g1) + _dot(tri, g2) + _dot(tri, g3)
        gct = gc.T
        for h in range(DN_HEADS):
            un = c * DN_HEADS + h
            c0 = h * DN_DIM
            qh_bf = q_ref[r0:r0 + C, c0:c0 + DN_DIM]
            qh = qh_bf.astype(F32)
            kh_bf = k_ref[r0:r0 + C, c0:c0 + DN_DIM]
            kh = kh_bf.astype(F32)
            vh = v_ref[r0:r0 + C, c0:c0 + DN_DIM].astype(F32)
            gcol = gc[:, h:h + 1]
            grow = gct[h:h + 1, :]
            beta = beta_all[r0:r0 + C, DN_HEADS + h:DN_HEADS + h + 1]
            decay = jnp.exp2(jnp.minimum(gcol - grow, 0.0))
            kb = kh * beta
            a = _dot_nt(jnp.concatenate([kb.astype(BF16), qh_bf], axis=0), kh_bf)
            Lm = (a[:C] * decay * fmask_ref[1]).astype(BF16)
            l_ref[un] = Lm
            t_ref[un] = bmask_ref[0] - Lm * bmask_ref[1]
            qk_ref[un] = (a[C:] * decay * fmask_ref[0]).astype(BF16)
            eg = jnp.exp2(gcol)
            rhs_ref[un] = jnp.concatenate([vh * beta, kb * eg], axis=1).astype(BF16)
            qa_ref[un, 0:C, :] = (qh * eg).astype(BF16)
            g_last = grow[:, C - 1:C]
            kdt_ref[un] = (kh.T * jnp.exp2(g_last - grow)).astype(BF16)
            egl_ref[un] = jnp.broadcast_to(jnp.exp2(g_last), (SUBLANES, LANES))

    nu = nc * DN_HEADS
    for lv in range(n_levels):
        for g0 in range(0, nu, group):
            units = list(range(g0, min(g0 + group, nu)))
            ps = [_dot(l_ref[un] * lmask_ref[lv], t_ref[un]).astype(BF16) for un in units]
            for un, p in zip(units, ps):
                t_ref[un] = t_ref[un] - _dot(t_ref[un], p).astype(BF16)
    for g0 in range(0, nu, group):
        units = list(range(g0, min(g0 + group, nu)))
        uws = [_dot(t_ref[un], rhs_ref[un]).astype(BF16) for un in units]
        for un, uw in zip(units, uws):
            r = _dot(jnp.concatenate([qk_ref[un], kdt_ref[un]], axis=0), uw)
            ou_ref[un] = r[:C, :DN_DIM]
            qa_ref[un, 0:C, :] = (qa_ref[un, 0:C, :].astype(F32) - r[:C, DN_DIM:]).astype(BF16)
            qa_ref[un, C:C + DN_DIM, :] = (-r[C:, DN_DIM:]).astype(BF16)
            ku_ref[un] = r[C:, :DN_DIM]

    for c in range(nc):
        r0 = c * C
        for h in range(DN_HEADS):
            un = c * DN_HEADS + h
            c0 = h * DN_DIM
            state = state_ref[h]
            r = _dot(qa_ref[un], state.astype(BF16))
            o = ou_ref[un] + r[:C]
            state_ref[h] = state * egl_ref[un][0:1, :] + r[C:] + ku_ref[un]
            zh = z_ref[r0:r0 + C, c0:c0 + DN_DIM]
            o = o * lax.rsqrt(jnp.mean(o * o, axis=-1, keepdims=True) + EPS) * onw_ref[...]
            o = o * _silu(zh)
            o_ref[r0:r0 + C, c0:c0 + DN_DIM] = o.astype(BF16)


def _gdn(gq, gk, gv, z, ab, alog, dtb, onw, B, S, tb):
    nb = S // tb
    C = GDN_CHUNK
    nu = (tb // C) * DN_HEADS
    const = lambda b, i: (0, 0)
    row = lambda b, i: (b * nb + i, 0)
    return pl.pallas_call(
        functools.partial(_gdn_body, tb=tb, group=16),
        grid=(B, nb),
        in_specs=[
            pl.BlockSpec((tb, HEAD_COLS), row),
            pl.BlockSpec((tb, HEAD_COLS), row),
            pl.BlockSpec((tb, HEAD_COLS), row),
            pl.BlockSpec((tb, HEAD_COLS), row),
            pl.BlockSpec((tb, LANES), row),
            pl.BlockSpec((1, LANES), const),
            pl.BlockSpec((1, LANES), const),
            pl.BlockSpec((1, LANES), const),
        ],
        out_specs=pl.BlockSpec((tb, HEAD_COLS), row),
        out_shape=jax.ShapeDtypeStruct((B * S, HEAD_COLS), BF16),
        scratch_shapes=[
            pltpu.VMEM((DN_HEADS, DN_DIM, DN_DIM), F32),
            pltpu.VMEM((nu, C, C), BF16),
            pltpu.VMEM((nu, C, C), BF16),
            pltpu.VMEM((nu, C, 2 * DN_DIM), BF16),
            pltpu.VMEM((nu, C, DN_DIM), F32),
            pltpu.VMEM((nu, DN_DIM, DN_DIM), F32),
            pltpu.VMEM((nu, C + DN_DIM, DN_DIM), BF16),
            pltpu.VMEM((nu, C, C), BF16),
            pltpu.VMEM((nu, DN_DIM, C), BF16),
            pltpu.VMEM((nu, SUBLANES, LANES), F32),
            pltpu.VMEM((2, C, C), F32),
            pltpu.VMEM((2, C, C), BF16),
            pltpu.VMEM((C.bit_length() - 2, C, C), BF16),
        ],
        compiler_params=pltpu.CompilerParams(
            dimension_semantics=("arbitrary", "arbitrary"), vmem_limit_bytes=VMEM_LIMIT),
        name="gated_deltanet",
    )(gq, gk, gv, z, ab, alog, dtb, onw)


def _mlp_body(x_ref, ma_ref, mb_ref, woa_ref, wob_ref, n2_ref, wup_ref, wdn_ref, o_ref, *, ff_chunk):
    x1 = x_ref[...] + _dot(ma_ref[...], woa_ref[...]) + _dot(mb_ref[...], wob_ref[...])
    ms = jnp.mean(x1 * x1, axis=-1, keepdims=True)
    h = (x1 * lax.rsqrt(ms + EPS) * n2_ref[...]).astype(BF16)
    mlp = None
    for c0 in range(0, wup_ref.shape[1], ff_chunk):
        up = jnp.maximum(_dot(h, wup_ref[:, c0:c0 + ff_chunk]), 0.0)
        down = _dot((up * up).astype(BF16), wdn_ref[c0:c0 + ff_chunk, :])
        mlp = down if mlp is None else mlp + down
    o_ref[...] = x1 + mlp


def _mlp(x2, mix_a, mix_b, woa, wob, n2, wup, wdn, tm, ff_chunk):
    T, D = x2.shape
    const = lambda i: (0, 0)
    row = lambda i: (i, 0)
    return pl.pallas_call(
        functools.partial(_mlp_body, ff_chunk=ff_chunk),
        grid=(T // tm,),
        in_specs=[
            pl.BlockSpec((tm, D), row),
            pl.BlockSpec((tm, HEAD_COLS), row),
            pl.BlockSpec((tm, HEAD_COLS), row),
            pl.BlockSpec(woa.shape, const),
            pl.BlockSpec(wob.shape, const),
            pl.BlockSpec((1, D), const),
            pl.BlockSpec(wup.shape, const),
            pl.BlockSpec(wdn.shape, const),
        ],
        out_specs=pl.BlockSpec((tm, D), row),
        out_shape=jax.ShapeDtypeStruct((T, D), F32),
        compiler_params=pltpu.CompilerParams(
            dimension_semantics=("arbitrary",), vmem_limit_bytes=VMEM_LIMIT),
        name="outproj_mlp",
    )(x2, mix_a, mix_b, woa, wob, n2, wup, wdn)


def _layer(x2, B, S, l, norm1_w, w_in, lambda_q1, lambda_k1, lambda_q2, lambda_k2, q_norm_w, k_norm_w,
           da_out_norm_w, conv_w, A_log, dt_bias, dn_out_norm_w, w_out, norm2_w, w_up, w_down,
           tm, tq, tb, ff_chunk):
    D = x2.shape[1]
    tile2 = lambda v: jnp.concatenate([v, v]).reshape(1, LANES).astype(F32)
    pad_lanes = lambda v: jnp.pad(v.astype(F32), (0, LANES - v.shape[0])).reshape(1, LANES)

    q, k, v, gq, gk, gv, z, ab = _inproj(
        x2, norm1_w[l].reshape(1, D).astype(F32), w_in.astype(F32), l,
        tile2(q_norm_w[l]) * LOG2E,
        tile2(k_norm_w[l]) * DA_QK_DIM ** 0.5,
        0.5 * conv_w[l].astype(F32), S, tm)

    lam_init = 0.8 - 0.6 * math.exp(-0.3 * l)
    vec = lambda p: p[l].reshape(1, DA_QK_DIM).astype(F32)
    bound = 8.0 * jnp.max(jnp.abs(q_norm_w[l].astype(F32))) * jnp.max(jnp.abs(k_norm_w[l].astype(F32)))
    attn_args = ((bound * LOG2E).reshape(1), q, k, v, vec(lambda_q1), vec(lambda_k1), vec(lambda_q2),
                 vec(lambda_k2), da_out_norm_w[l].reshape(1, LANES).astype(F32))
    mix_a = lax.cond(bound <= MAX_FIXED_SHIFT,
                     lambda a: _attention(*a, B, S, tq, lam_init, online=False),
                     lambda a: _attention(*a, B, S, tq, lam_init, online=True), attn_args)

    mix_b = _gdn(gq, gk, gv, z, ab, pad_lanes(A_log[l]), pad_lanes(dt_bias[l]),
                 dn_out_norm_w[l].reshape(1, LANES).astype(F32), B, S, tb)

    wo = w_out[l].astype(BF16)
    return _mlp(x2, mix_a, mix_b, wo[:HEAD_COLS], wo[HEAD_COLS:], norm2_w[l].reshape(1, D).astype(F32),
                w_up[l].astype(BF16), w_down[l].astype(BF16), tm, ff_chunk)


def kernel(x, norm1_w, w_in, lambda_q1, lambda_k1, lambda_q2, lambda_k2, q_norm_w, k_norm_w, da_out_norm_w,
           conv_w, A_log, dt_bias, dn_out_norm_w, w_out, norm2_w, w_up, w_down):
    B, S, D = x.shape
    x2 = x.reshape(B * S, D)
    tm = min(1024, S)
    tq = min(512, S)
    tb = min(1024, S)
    for l in range(w_in.shape[0]):
        x2 = _layer(x2, B, S, l, norm1_w, w_in, lambda_q1, lambda_k1, lambda_q2, lambda_k2, q_norm_w,
                    k_norm_w, da_out_norm_w, conv_w, A_log, dt_bias, dn_out_norm_w, w_out, norm2_w,
                    w_up, w_down, tm, tq, tb, 1024)
    return x2.reshape(B, S, D)
```

```python
import functools
import math

import jax
import jax.numpy as jnp
from jax import lax
from jax.experimental import pallas as pl
from jax.experimental.pallas import tpu as pltpu

F32 = jnp.float32
BF16 = jnp.bfloat16
EPS = 1e-6
LANES = 128
SUBLANES = 8
NEG_BIG = -1e30
LOG2E = math.log2(math.e)
MAX_FIXED_SHIFT = 40.0

DA_HEADS = 4
DA_QK_DIM = 64
DN_HEADS = 4
DN_DIM = 128
CONV_WIDTH = 4
HEAD_COLS = 512
GDN_CHUNK = 128
VMEM_LIMIT = 56 * 1024 * 1024


def _dot(a, b):
    return jnp.dot(a, b, preferred_element_type=F32)


def _dot_nt(a, b):
    return lax.dot_general(a, b, (((1,), (1,)), ((), ())), preferred_element_type=F32)


def _silu(x):
    h = 0.5 * x
    return h + h * jnp.tanh(h)


def _split3(x):
    x1 = x.astype(BF16)
    r1 = x - x1.astype(F32)
    x2 = r1.astype(BF16)
    x3 = (r1 - x2.astype(F32)).astype(BF16)
    return x1, x2, x3


def _inproj_body(x_ref, n1_ref, w_ref, qnw_ref, knw_ref, cw_ref,
                 q_out, k_out, v_out, gq_out, gk_out, gv_out, z_out, ab_out, tail_ref, wb_ref,
                 *, tiles_per_seq):
    tm = x_ref.shape[0]
    piece = 2 * LANES
    wa0, wdn0, wz0, wab0 = 0, 3 * HEAD_COLS, 6 * HEAD_COLS, 7 * HEAD_COLS
    n_in = w_ref.shape[1]

    @pl.when(pl.program_id(0) == 0)
    def _():
        for c0 in range(0, wab0, HEAD_COLS):
            wb_ref[:, c0:c0 + HEAD_COLS] = w_ref[:, c0:c0 + HEAD_COLS].astype(BF16)
        wb_ref[:, wab0:wab0 + LANES] = jnp.zeros((wb_ref.shape[0], LANES), BF16)
        wb_ref[:, wab0:n_in] = w_ref[:, wab0:n_in].astype(BF16)

    x = x_ref[...]
    ms = jnp.mean(x * x, axis=-1, keepdims=True)
    h = (x * lax.rsqrt(ms + EPS) * n1_ref[...]).astype(BF16)

    @pl.when(pl.program_id(0) % tiles_per_seq == 0)
    def _():
        tail_ref[...] = jnp.zeros_like(tail_ref)

    def deltanet_piece(p0):
        d = _dot(h, wb_ref[:, wdn0 + p0:wdn0 + p0 + piece])
        tail = tail_ref[:, p0:p0 + piece]
        tail_ref[:, p0:p0 + piece] = d[tm - SUBLANES:tm, :]
        for cc in range(0, piece, LANES):
            c0 = p0 + cc
            cb = c0 // LANES
            for r0 in range(0, tm, GDN_CHUNK):
                halo = tail[:, cc:cc + LANES] if r0 == 0 else d[r0 - SUBLANES:r0, cc:cc + LANES]
                a = jnp.concatenate([halo, d[r0:r0 + GDN_CHUNK, cc:cc + LANES]], axis=0)
                tap = lambda w: cw_ref[w:w + 1, c0:c0 + LANES]
                a1 = pltpu.roll(a, 1, axis=0)
                u = tap(1) * a + tap(0) * a1
                y = tap(3) * a[SUBLANES:] + tap(2) * a1[SUBLANES:] + pltpu.roll(u, 2, axis=0)[SUBLANES:]
                y = y + y * jnp.tanh(y)
                if cb < DN_HEADS:
                    y = y * (lax.rsqrt(jnp.sum(y * y, axis=-1, keepdims=True) + EPS) * (DN_DIM ** -0.5))
                elif cb < 2 * DN_HEADS:
                    y = y * lax.rsqrt(jnp.sum(y * y, axis=-1, keepdims=True) + EPS)
                out = (gq_out, gk_out, gv_out)[cb // DN_HEADS]
                c1 = (cb % DN_HEADS) * LANES
                out[r0:r0 + GDN_CHUNK, c1:c1 + LANES] = y.astype(BF16)

    lo = lax.broadcasted_iota(jnp.int32, (tm, LANES), 1) < DA_QK_DIM

    def qk_norm(t, w8):
        sq = t * t
        s_lo = jnp.sum(jnp.where(lo, sq, 0.0), axis=-1, keepdims=True)
        s_hi = jnp.sum(jnp.where(lo, 0.0, sq), axis=-1, keepdims=True)
        return t * lax.rsqrt(jnp.where(lo, s_lo, s_hi) + DA_QK_DIM * EPS) * w8

    def attention_piece(p0):
        t = _dot(h, wb_ref[:, wa0 + p0:wa0 + p0 + piece])
        for cc in range(0, piece, LANES):
            th = t[:, cc:cc + LANES]
            c1 = p0 % HEAD_COLS + cc
            if p0 < HEAD_COLS:
                q_out[:, c1:c1 + LANES] = qk_norm(th, qnw_ref[...]).astype(BF16)
            elif p0 < 2 * HEAD_COLS:
                k_out[:, c1:c1 + LANES] = qk_norm(th, knw_ref[...]).astype(BF16)
            else:
                v_out[:, c1:c1 + LANES] = th.astype(BF16)

    for p0 in range(0, 3 * HEAD_COLS, piece):
        deltanet_piece(p0)
        attention_piece(p0)
    for p0 in range(0, HEAD_COLS, piece):
        z_out[:, p0:p0 + piece] = _dot(h, wb_ref[:, wz0 + p0:wz0 + p0 + piece])
    ab_out[...] = _dot(h, wb_ref[:, wab0:wab0 + LANES])


def _inproj(x2, n1, w_in, l, qnw, knw, cw, S, tm):
    T, D = x2.shape
    W = cw.shape[1]
    n_in = w_in.shape[2]
    const = lambda i: (0, 0)
    row = lambda i: (i, 0)
    slab = pl.BlockSpec((tm, HEAD_COLS), row)
    slab_bf16 = jax.ShapeDtypeStruct((T, HEAD_COLS), BF16)
    return pl.pallas_call(
        functools.partial(_inproj_body, tiles_per_seq=S // tm),
        grid=(T // tm,),
        in_specs=[
            pl.BlockSpec((tm, D), row),
            pl.BlockSpec((1, D), const),
            pl.BlockSpec((None, D, n_in), lambda i: (l, 0, 0)),
            pl.BlockSpec((1, LANES), const),
            pl.BlockSpec((1, LANES), const),
            pl.BlockSpec((CONV_WIDTH, W), const),
        ],
        out_specs=[slab, slab, slab, slab, slab, slab, slab, pl.BlockSpec((tm, LANES), row)],
        out_shape=[slab_bf16, slab_bf16, slab_bf16, slab_bf16, slab_bf16, slab_bf16,
                   jax.ShapeDtypeStruct((T, HEAD_COLS), F32),
                   jax.ShapeDtypeStruct((T, LANES), F32)],
        scratch_shapes=[
            pltpu.VMEM((SUBLANES, W), F32),
            pltpu.VMEM((D, 7 * HEAD_COLS + LANES), BF16),
        ],
        compiler_params=pltpu.CompilerParams(
            dimension_semantics=("arbitrary",), vmem_limit_bytes=VMEM_LIMIT),
        name="inproj",
    )(x2, n1, w_in, qnw, knw, cw)


def _stack_maps(q_ref, qz_ref, tq):
    q = q_ref[...]
    lane = lax.broadcasted_iota(jnp.int32, (tq, LANES), 1)
    zero = jnp.zeros_like(q)
    qz_ref[0:tq, :] = jnp.where(lane < DA_QK_DIM, q, zero)
    qz_ref[tq:2 * tq, :] = jnp.where(lane < DA_QK_DIM, zero, q)


def _causal_mask(tq):
    rows = lax.broadcasted_iota(jnp.int32, (2 * tq, tq), 0)
    cols = lax.broadcasted_iota(jnp.int32, (2 * tq, tq), 1)
    return jnp.where(rows >= tq, rows - tq, rows) >= cols


def _attn_finish(acc, l, lq1_ref, lk1_ref, lq2_ref, lk2_ref, onw_ref, o_ref, tq, lam_init):
    lam = (jnp.exp(jnp.sum(lq1_ref[...] * lk1_ref[...], axis=-1, keepdims=True))
           - jnp.exp(jnp.sum(lq2_ref[...] * lk2_ref[...], axis=-1, keepdims=True)) + lam_init)
    o = acc[:tq] / l[:tq] - lam * (acc[tq:] / l[tq:])
    ms = jnp.mean(o * o, axis=-1, keepdims=True)
    o = o * lax.rsqrt(ms + EPS) * onw_ref[...] * (1.0 - lam_init)
    o_ref[...] = o.astype(BF16)


def _attn_shift_body(shift_ref, q_ref, k_ref, v_ref, lq1_ref, lk1_ref, lq2_ref, lk2_ref, onw_ref, o_ref,
                     qz_ref, p_ref, l_ref, acc_ref, *, tq, lam_init, unroll, heads):
    i = pl.program_id(2)
    shift = shift_ref[0]
    cols = lambda hh: slice(hh * LANES, (hh + 1) * LANES)
    rows = lambda j: pl.ds(pl.multiple_of(j * tq, tq), tq)

    def probs(hh, j, diagonal):
        s = _dot_nt(qz_ref[hh], k_ref[rows(j), cols(hh)]) - shift
        if diagonal:
            s = jnp.where(_causal_mask(tq), s, NEG_BIG)
        p = jnp.exp2(s)
        psum = p[:, 0:LANES]
        for c0 in range(LANES, tq, LANES):
            psum = psum + p[:, c0:c0 + LANES]
        return p.astype(BF16), psum

    for hh in range(heads):
        _stack_maps(q_ref.at[:, cols(hh)], qz_ref.at[hh], tq)
        p, psum = probs(hh, i, True)
        p_ref[hh] = p
        l_ref[hh] = psum
        acc_ref[hh] = jnp.zeros((2 * tq, LANES), F32)

    def pv_prev(hh, j):
        return _dot(p_ref[hh], v_ref[rows(jnp.where(j == 0, i, j - 1)), cols(hh)])

    def steps(j0, n):
        acc = [pv_prev(hh, j0) for hh in range(heads)]
        psum = [None] * heads
        for u in range(n):
            for hh in range(heads):
                p, ps = probs(hh, j0 + u, False)
                psum[hh] = ps if psum[hh] is None else psum[hh] + ps
                if u + 1 < n:
                    acc[hh] = acc[hh] + _dot(p, v_ref[rows(j0 + u), cols(hh)])
                else:
                    p_ref[hh] = p
        for hh in range(heads):
            acc_ref[hh] += acc[hh]
            l_ref[hh] += psum[hh]

    start = 0
    n = unroll
    while n >= 1:
        trips = (i - start) // n

        def group(t, _, start=start, n=n):
            steps(start + t * n, n)
            return 0

        lax.fori_loop(0, trips, group, 0)
        start = start + trips * n
        n //= 2
    for hh in range(heads):
        acc = acc_ref[hh] + pv_prev(hh, i)
        l = jnp.sum(l_ref[hh], axis=-1, keepdims=True)
        _attn_finish(acc, l, lq1_ref, lk1_ref, lq2_ref, lk2_ref, onw_ref, o_ref.at[:, cols(hh)], tq, lam_init)


def _attn_online_body(shift_ref, q_ref, k_ref, v_ref, lq1_ref, lk1_ref, lq2_ref, lk2_ref, onw_ref, o_ref,
                      qz_ref, *, tq, lam_init):
    del shift_ref
    i = pl.program_id(2)
    _stack_maps(q_ref, qz_ref, tq)

    def step(j, carry, diagonal):
        m, l, acc = carry
        r0 = pl.multiple_of(j * tq, tq)
        s = _dot_nt(qz_ref[...], k_ref[pl.ds(r0, tq), :])
        if diagonal:
            s = jnp.where(_causal_mask(tq), s, NEG_BIG)
        m_new = jnp.maximum(m, jnp.max(s, axis=-1, keepdims=True))
        p = jnp.exp2(s - m_new)
        alpha = jnp.exp2(m - m_new)
        l = alpha * l + jnp.sum(p, axis=-1, keepdims=True)
        acc = alpha * acc + _dot(p.astype(BF16), v_ref[pl.ds(r0, tq), :])
        return m_new, l, acc

    init = (jnp.full((2 * tq, 1), NEG_BIG, F32), jnp.zeros((2 * tq, 1), F32),
            jnp.zeros((2 * tq, LANES), F32))
    carry = lax.fori_loop(0, i, lambda j, c: step(j, c, False), init)
    _, l, acc = step(i, carry, True)
    _attn_finish(acc, l, lq1_ref, lk1_ref, lq2_ref, lk2_ref, onw_ref, o_ref, tq, lam_init)


def _attention(shift, q, k, v, lq1, lk1, lq2, lk2, onw, B, S, tq, lam_init, online):
    nq = S // tq
    vec = pl.BlockSpec((1, DA_QK_DIM), lambda b, h, i: (0, 0))
    if online:
        heads = 1
        body = functools.partial(_attn_online_body, tq=tq, lam_init=lam_init)
        scratch = [pltpu.VMEM((2 * tq, LANES), BF16)]
    else:
        heads = DA_HEADS
        body = functools.partial(_attn_shift_body, tq=tq, lam_init=lam_init, unroll=4, heads=heads)
        scratch = [pltpu.VMEM((heads, 2 * tq, LANES), BF16),
                   pltpu.VMEM((heads, 2 * tq, tq), BF16),
                   pltpu.VMEM((heads, 2 * tq, LANES), F32),
                   pltpu.VMEM((heads, 2 * tq, LANES), F32)]
    width = heads * LANES
    return pl.pallas_call(
        body,
        grid=(B, DA_HEADS // heads, nq),
        in_specs=[
            pl.BlockSpec(memory_space=pltpu.SMEM),
            pl.BlockSpec((tq, width), lambda b, h, i: (b * nq + i, h)),
            pl.BlockSpec((S, width), lambda b, h, i: (b, h)),
            pl.BlockSpec((S, width), lambda b, h, i: (b, h)),
            vec, vec, vec, vec,
            pl.BlockSpec((1, LANES), lambda b, h, i: (0, 0)),
        ],
        out_specs=pl.BlockSpec((tq, width), lambda b, h, i: (b * nq + i, h)),
        out_shape=jax.ShapeDtypeStruct((B * S, HEAD_COLS), BF16),
        scratch_shapes=scratch,
        compiler_params=pltpu.CompilerParams(
            dimension_semantics=("arbitrary", "arbitrary", "arbitrary"), vmem_limit_bytes=VMEM_LIMIT),
        name="diff_attention_online" if online else "diff_attention",
    )(shift, q, k, v, lq1, lk1, lq2, lk2, onw)


def _merge_masks(row, col, n):
    masks = []
    b = 2
    while b < n:
        masks.append(((row // b) == (col // b) + 1) & ((row // (2 * b)) == (col // (2 * b))))
        b *= 2
    return masks


def _gdn_body(q_ref, k_ref, v_ref, z_ref, ab_ref, alog_ref, dtb_ref, onw_ref, o_ref,
              state_ref, l_ref, t_ref, rhs_ref, ou_ref, ku_ref, qa_ref, qk_ref, kdt_ref, egl_ref,
              fmask_ref, bmask_ref, lmask_ref, *, tb, group):
    C = GDN_CHUNK
    nc = tb // C
    blk = pl.program_id(1)
    n_levels = lmask_ref.shape[0]

    @pl.when(blk == 0)
    def _():
        state_ref[...] = jnp.zeros_like(state_ref)
        row = lax.broadcasted_iota(jnp.int32, (C, C), 0)
        col = lax.broadcasted_iota(jnp.int32, (C, C), 1)
        fmask_ref[0] = (row >= col).astype(F32)
        fmask_ref[1] = (row > col).astype(F32)
        bmask_ref[0] = (row == col).astype(BF16)
        bmask_ref[1] = ((row == col + 1) & (row % 2 == 1)).astype(BF16)
        for lv, m in enumerate(_merge_masks(row, col, C)):
            lmask_ref[lv] = m.astype(BF16)

    ab = ab_ref[...]
    sp_in = ab + dtb_ref[...]
    softplus = jnp.maximum(sp_in, 0.0) + jnp.log(1.0 + jnp.exp(-jnp.abs(sp_in)))
    gtok = (-LOG2E) * jnp.exp(alog_ref[...]) * softplus
    gtok = jnp.where(lax.broadcasted_iota(jnp.int32, (tb, LANES), 1) < DN_HEADS, gtok, 0.0)
    beta_all = 1.0 / (1.0 + jnp.exp(-ab))

    tri = fmask_ref[0].astype(BF16)

    for c in range(nc):
        r0 = c * C
        g1, g2, g3 = _split3(gtok[r0:r0 + C])
        gc = _dot(tri, g1) + _dot(tri, g2) + _dot(tri, g3)
        gct = gc.T
        for h in range(DN_HEADS):
            un = c * DN_HEADS + h
            c0 = h * DN_DIM
            qh_bf = q_ref[r0:r0 + C, c0:c0 + DN_DIM]
            qh = qh_bf.astype(F32)
            kh_bf = k_ref[r0:r0 + C, c0:c0 + DN_DIM]
            kh = kh_bf.astype(F32)
            vh = v_ref[r0:r0 + C, c0:c0 + DN_DIM].astype(F32)
            gcol = gc[:, h:h + 1]
            grow = gct[h:h + 1, :]
            beta = beta_all[r0:r0 + C, DN_HEADS + h:DN_HEADS + h + 1]
            decay = jnp.exp2(jnp.minimum(gcol - grow, 0.0))
            kb = kh * beta
            a = _dot_nt(jnp.concatenate([kb.astype(BF16), qh_bf], axis=0), kh_bf)
            Lm = (a[:C] * decay * fmask_ref[1]).astype(BF16)
            l_ref[un] = Lm
            t_ref[un] = bmask_ref[0] - Lm * bmask_ref[1]
            qk_ref[un] = (a[C:] * decay * fmask_ref[0]).astype(BF16)
            eg = jnp.exp2(gcol)
            rhs_ref[un] = jnp.concatenate([vh * beta, kb * eg], axis=1).astype(BF16)
            qa_ref[un, 0:C, :] = (qh * eg).astype(BF16)
            g_last = grow[:, C - 1:C]
            kdt_ref[un] = (kh.T * jnp.exp2(g_last - grow)).astype(BF16)
            egl_ref[un] = jnp.broadcast_to(jnp.exp2(g_last), (SUBLANES, LANES))

    nu = nc * DN_HEADS
    for lv in range(n_levels):
        for g0 in range(0, nu, group):
            units = list(range(g0, min(g0 + group, nu)))
            ps = [_dot(l_ref[un] * lmask_ref[lv], t_ref[un]).astype(BF16) for un in units]
            for un, p in zip(units, ps):
                t_ref[un] = t_ref[un] - _dot(t_ref[un], p).astype(BF16)
    for g0 in range(0, nu, group):
        units = list(range(g0, min(g0 + group, nu)))
        uws = [_dot(t_ref[un], rhs_ref[un]).astype(BF16) for un in units]
        for un, uw in zip(units, uws):
            r = _dot(jnp.concatenate([qk_ref[un], kdt_ref[un]], axis=0), uw)
            ou_ref[un] = r[:C, :DN_DIM]
            qa_ref[un, 0:C, :] = (qa_ref[un, 0:C, :].astype(F32) - r[:C, DN_DIM:]).astype(BF16)
            qa_ref[un, C:C + DN_DIM, :] = (-r[C:, DN_DIM:]).astype(BF16)
            ku_ref[un] = r[C:, :DN_DIM]

    for c in range(nc):
        r0 = c * C
        for h in range(DN_HEADS):
            un = c * DN_HEADS + h
            c0 = h * DN_DIM
            state = state_ref[h]
            r = _dot(qa_ref[un], state.astype(BF16))
            o = ou_ref[un] + r[:C]
            state_ref[h] = state * egl_ref[un][0:1, :] + r[C:] + ku_ref[un]
            zh = z_ref[r0:r0 + C, c0:c0 + DN_DIM]
            o = o * lax.rsqrt(jnp.mean(o * o, axis=-1, keepdims=True) + EPS) * onw_ref[...]
            o = o * _silu(zh)
            o_ref[r0:r0 + C, c0:c0 + DN_DIM] = o.astype(BF16)


def _gdn(gq, gk, gv, z, ab, alog, dtb, onw, B, S, tb):
    nb = S // tb
    C = GDN_CHUNK
    nu = (tb // C) * DN_HEADS
    const = lambda b, i: (0, 0)
    row = lambda b, i: (b * nb + i, 0)
    return pl.pallas_call(
        functools.partial(_gdn_body, tb=tb, group=16),
        grid=(B, nb),
        in_specs=[
            pl.BlockSpec((tb, HEAD_COLS), row),
            pl.BlockSpec((tb, HEAD_COLS), row),
            pl.BlockSpec((tb, HEAD_COLS), row),
            pl.BlockSpec((tb, HEAD_COLS), row),
            pl.BlockSpec((tb, LANES), row),
            pl.BlockSpec((1, LANES), const),
            pl.BlockSpec((1, LANES), const),
            pl.BlockSpec((1, LANES), const),
        ],
        out_specs=pl.BlockSpec((tb, HEAD_COLS), row),
        out_shape=jax.ShapeDtypeStruct((B * S, HEAD_COLS), BF16),
        scratch_shapes=[
            pltpu.VMEM((DN_HEADS, DN_DIM, DN_DIM), F32),
            pltpu.VMEM((nu, C, C), BF16),
            pltpu.VMEM((nu, C, C), BF16),
            pltpu.VMEM((nu, C, 2 * DN_DIM), BF16),
            pltpu.VMEM((nu, C, DN_DIM), F32),
            pltpu.VMEM((nu, DN_DIM, DN_DIM), F32),
            pltpu.VMEM((nu, C + DN_DIM, DN_DIM), BF16),
            pltpu.VMEM((nu, C, C), BF16),
            pltpu.VMEM((nu, DN_DIM, C), BF16),
            pltpu.VMEM((nu, SUBLANES, LANES), F32),
            pltpu.VMEM((2, C, C), F32),
            pltpu.VMEM((2, C, C), BF16),
            pltpu.VMEM((C.bit_length() - 2, C, C), BF16),
        ],
        compiler_params=pltpu.CompilerParams(
            dimension_semantics=("arbitrary", "arbitrary"), vmem_limit_bytes=VMEM_LIMIT),
        name="gated_deltanet",
    )(gq, gk, gv, z, ab, alog, dtb, onw)


def _mlp_body(x_ref, ma_ref, mb_ref, woa_ref, wob_ref, n2_ref, wup_ref, wdn_ref, o_ref, *, ff_chunk):
    x1 = x_ref[...] + _dot(ma_ref[...], woa_ref[...]) + _dot(mb_ref[...], wob_ref[...])
    ms = jnp.mean(x1 * x1, axis=-1, keepdims=True)
    h = (x1 * lax.rsqrt(ms + EPS) * n2_ref[...]).astype(BF16)
    mlp = None
    for c0 in range(0, wup_ref.shape[1], ff_chunk):
        up = jnp.maximum(_dot(h, wup_ref[:, c0:c0 + ff_chunk]), 0.0)
        down = _dot((up * up).astype(BF16), wdn_ref[c0:c0 + ff_chunk, :])
        mlp = down if mlp is None else mlp + down
    o_ref[...] = x1 + mlp


def _mlp(x2, mix_a, mix_b, woa, wob, n2, wup, wdn, tm, ff_chunk):
    T, D = x2.shape
    const = lambda i: (0, 0)
    row = lambda i: (i, 0)
    return pl.pallas_call(
        functools.partial(_mlp_body, ff_chunk=ff_chunk),
        grid=(T // tm,),
        in_specs=[
            pl.BlockSpec((tm, D), row),
            pl.BlockSpec((tm, HEAD_COLS), row),
            pl.BlockSpec((tm, HEAD_COLS), row),
            pl.BlockSpec(woa.shape, const),
            pl.BlockSpec(wob.shape, const),
            pl.BlockSpec((1, D), const),
            pl.BlockSpec(wup.shape, const),
            pl.BlockSpec(wdn.shape, const),
        ],
        out_specs=pl.BlockSpec((tm, D), row),
        out_shape=jax.ShapeDtypeStruct((T, D), F32),
        compiler_params=pltpu.CompilerParams(
            dimension_semantics=("arbitrary",), vmem_limit_bytes=VMEM_LIMIT),
        name="outproj_mlp",
    )(x2, mix_a, mix_b, woa, wob, n2, wup, wdn)


def _layer(x2, B, S, l, norm1_w, w_in, lambda_q1, lambda_k1, lambda_q2, lambda_k2, q_norm_w, k_norm_w,
           da_out_norm_w, conv_w, A_log, dt_bias, dn_out_norm_w, w_out, norm2_w, w_up, w_down,
           tm, tq, tb, tm_mlp, ff_chunk):
    D = x2.shape[1]
    tile2 = lambda v: jnp.concatenate([v, v]).reshape(1, LANES).astype(F32)
    pad_lanes = lambda v: jnp.pad(v.astype(F32), (0, LANES - v.shape[0])).reshape(1, LANES)

    q, k, v, gq, gk, gv, z, ab = _inproj(
        x2, norm1_w[l].reshape(1, D).astype(F32), w_in.astype(F32), l,
        tile2(q_norm_w[l]) * LOG2E,
        tile2(k_norm_w[l]) * DA_QK_DIM ** 0.5,
        0.5 * conv_w[l].astype(F32), S, tm)

    lam_init = 0.8 - 0.6 * math.exp(-0.3 * l)
    vec = lambda p: p[l].reshape(1, DA_QK_DIM).astype(F32)
    bound = 8.0 * jnp.max(jnp.abs(q_norm_w[l].astype(F32))) * jnp.max(jnp.abs(k_norm_w[l].astype(F32)))
    attn_args = ((bound * LOG2E).reshape(1), q, k, v, vec(lambda_q1), vec(lambda_k1), vec(lambda_q2),
                 vec(lambda_k2), da_out_norm_w[l].reshape(1, LANES).astype(F32))
    mix_a = lax.cond(bound <= MAX_FIXED_SHIFT,
                     lambda a: _attention(*a, B, S, tq, lam_init, online=False),
                     lambda a: _attention(*a, B, S, tq, lam_init, online=True), attn_args)

    mix_b = _gdn(gq, gk, gv, z, ab, pad_lanes(A_log[l]), pad_lanes(dt_bias[l]),
                 dn_out_norm_w[l].reshape(1, LANES).astype(F32), B, S, tb)

    wo = w_out[l].astype(BF16)
    return _mlp(x2, mix_a, mix_b, wo[:HEAD_COLS], wo[HEAD_COLS:], norm2_w[l].reshape(1, D).astype(F32),
                w_up[l].astype(BF16), w_down[l].astype(BF16), tm_mlp, ff_chunk)


def kernel(x, norm1_w, w_in, lambda_q1, lambda_k1, lambda_q2, lambda_k2, q_norm_w, k_norm_w, da_out_norm_w,
           conv_w, A_log, dt_bias, dn_out_norm_w, w_out, norm2_w, w_up, w_down):
    B, S, D = x.shape
    x2 = x.reshape(B * S, D)
    tm = min(512, S)
    tq = min(512, S)
    tb = min(1024, S)
    tm_mlp = min(512, S)
    ff_chunk = 1024
    assert S % tm == 0 and S % tq == 0 and S % tb == 0 and S % tm_mlp == 0, (S, tm, tq, tb, tm_mlp)
    assert tm % GDN_CHUNK == 0 and tb % GDN_CHUNK == 0 and tq % LANES == 0, (tm, tb, tq)
    assert w_in.shape[1:] == (D, 7 * HEAD_COLS + 2 * DN_HEADS) and w_up.shape[2] % ff_chunk == 0
    for l in range(w_in.shape[0]):
        x2 = _layer(x2, B, S, l, norm1_w, w_in, lambda_q1, lambda_k1, lambda_q2, lambda_k2, q_norm_w,
                    k_norm_w, da_out_norm_w, conv_w, A_log, dt_bias, dn_out_norm_w, w_out, norm2_w,
                    w_up, w_down, tm, tq, tb, tm_mlp, ff_chunk)
    return x2.reshape(B, S, D)
```

```python
import functools
import math

import jax
import jax.numpy as jnp
from jax import lax
from jax.experimental import pallas as pl
from jax.experimental.pallas import tpu as pltpu

F32 = jnp.float32
BF16 = jnp.bfloat16
EPS = 1e-6
LANES = 128
SUBLANES = 8
NEG_BIG = -1e30
LOG2E = math.log2(math.e)
MAX_FIXED_SHIFT = 40.0

DA_HEADS = 4
DA_QK_DIM = 64
DN_HEADS = 4
DN_DIM = 128
CONV_WIDTH = 4
HEAD_COLS = 512
GDN_CHUNK = 128
VMEM_LIMIT = 56 * 1024 * 1024


def _dot(a, b):
    return jnp.dot(a, b, preferred_element_type=F32)


def _dot_nt(a, b):
    return lax.dot_general(a, b, (((1,), (1,)), ((), ())), preferred_element_type=F32)


def _silu(x):
    h = 0.5 * x
    return h + h * jnp.tanh(h)


def _split3(x):
    x1 = x.astype(BF16)
    r1 = x - x1.astype(F32)
    x2 = r1.astype(BF16)
    x3 = (r1 - x2.astype(F32)).astype(BF16)
    return x1, x2, x3


def _inproj_body(x_ref, n1_ref, w_ref, qnw_ref, knw_ref, cw_ref,
                 q_out, k_out, v_out, gq_out, gk_out, gv_out, z_out, ab_out, tail_ref, wb_ref,
                 *, tiles_per_seq):
    tm = x_ref.shape[0]
    piece = 2 * LANES
    wa0, wdn0, wz0, wab0 = 0, 3 * HEAD_COLS, 6 * HEAD_COLS, 7 * HEAD_COLS
    n_in = w_ref.shape[1]

    @pl.when(pl.program_id(0) == 0)
    def _():
        for c0 in range(0, wab0, HEAD_COLS):
            wb_ref[:, c0:c0 + HEAD_COLS] = w_ref[:, c0:c0 + HEAD_COLS].astype(BF16)
        wb_ref[:, wab0:wab0 + LANES] = jnp.zeros((wb_ref.shape[0], LANES), BF16)
        wb_ref[:, wab0:n_in] = w_ref[:, wab0:n_in].astype(BF16)

    x = x_ref[...]
    ms = jnp.mean(x * x, axis=-1, keepdims=True)
    h = (x * lax.rsqrt(ms + EPS) * n1_ref[...]).astype(BF16)

    @pl.when(pl.program_id(0) % tiles_per_seq == 0)
    def _():
        tail_ref[...] = jnp.zeros_like(tail_ref)

    def deltanet_piece(p0):
        d = _dot(h, wb_ref[:, wdn0 + p0:wdn0 + p0 + piece])
        tail = tail_ref[:, p0:p0 + piece]
        tail_ref[:, p0:p0 + piece] = d[tm - SUBLANES:tm, :]
        for cc in range(0, piece, LANES):
            c0 = p0 + cc
            cb = c0 // LANES
            for r0 in range(0, tm, GDN_CHUNK):
                halo = tail[:, cc:cc + LANES] if r0 == 0 else d[r0 - SUBLANES:r0, cc:cc + LANES]
                a = jnp.concatenate([halo, d[r0:r0 + GDN_CHUNK, cc:cc + LANES]], axis=0)
                tap = lambda w: cw_ref[w:w + 1, c0:c0 + LANES]
                a1 = pltpu.roll(a, 1, axis=0)
                u = tap(1) * a + tap(0) * a1
                y = tap(3) * a[SUBLANES:] + tap(2) * a1[SUBLANES:] + pltpu.roll(u, 2, axis=0)[SUBLANES:]
                y = y + y * jnp.tanh(y)
                if cb < DN_HEADS:
                    y = y * (lax.rsqrt(jnp.sum(y * y, axis=-1, keepdims=True) + EPS) * (DN_DIM ** -0.5))
                elif cb < 2 * DN_HEADS:
                    y = y * lax.rsqrt(jnp.sum(y * y, axis=-1, keepdims=True) + EPS)
                out = (gq_out, gk_out, gv_out)[cb // DN_HEADS]
                c1 = (cb % DN_HEADS) * LANES
                out[r0:r0 + GDN_CHUNK, c1:c1 + LANES] = y.astype(BF16)

    lo = lax.broadcasted_iota(jnp.int32, (tm, LANES), 1) < DA_QK_DIM

    def qk_norm(t, w8):
        sq = t * t
        s_lo = jnp.sum(jnp.where(lo, sq, 0.0), axis=-1, keepdims=True)
        s_hi = jnp.sum(jnp.where(lo, 0.0, sq), axis=-1, keepdims=True)
        return t * lax.rsqrt(jnp.where(lo, s_lo, s_hi) + DA_QK_DIM * EPS) * w8

    def attention_piece(p0):
        t = _dot(h, wb_ref[:, wa0 + p0:wa0 + p0 + piece])
        for cc in range(0, piece, LANES):
            th = t[:, cc:cc + LANES]
            c1 = p0 % HEAD_COLS + cc
            if p0 < HEAD_COLS:
                q_out[:, c1:c1 + LANES] = qk_norm(th, qnw_ref[...]).astype(BF16)
            elif p0 < 2 * HEAD_COLS:
                k_out[:, c1:c1 + LANES] = qk_norm(th, knw_ref[...]).astype(BF16)
            else:
                v_out[:, c1:c1 + LANES] = th.astype(BF16)

    for p0 in range(0, 3 * HEAD_COLS, piece):
        deltanet_piece(p0)
        attention_piece(p0)
    for p0 in range(0, HEAD_COLS, piece):
        z_out[:, p0:p0 + piece] = _dot(h, wb_ref[:, wz0 + p0:wz0 + p0 + piece])
    ab_out[...] = _dot(h, wb_ref[:, wab0:wab0 + LANES])


def _inproj(x2, n1, w_in, l, qnw, knw, cw, S, tm):
    T, D = x2.shape
    W = cw.shape[1]
    n_in = w_in.shape[2]
    const = lambda i: (0, 0)
    row = lambda i: (i, 0)
    slab = pl.BlockSpec((tm, HEAD_COLS), row)
    slab_bf16 = jax.ShapeDtypeStruct((T, HEAD_COLS), BF16)
    return pl.pallas_call(
        functools.partial(_inproj_body, tiles_per_seq=S // tm),
        grid=(T // tm,),
        in_specs=[
            pl.BlockSpec((tm, D), row),
            pl.BlockSpec((1, D), const),
            pl.BlockSpec((None, D, n_in), lambda i: (l, 0, 0)),
            pl.BlockSpec((1, LANES), const),
            pl.BlockSpec((1, LANES), const),
            pl.BlockSpec((CONV_WIDTH, W), const),
        ],
        out_specs=[slab, slab, slab, slab, slab, slab, slab, pl.BlockSpec((tm, LANES), row)],
        out_shape=[slab_bf16, slab_bf16, slab_bf16, slab_bf16, slab_bf16, slab_bf16,
                   jax.ShapeDtypeStruct((T, HEAD_COLS), F32),
                   jax.ShapeDtypeStruct((T, LANES), F32)],
        scratch_shapes=[
            pltpu.VMEM((SUBLANES, W), F32),
            pltpu.VMEM((D, 7 * HEAD_COLS + LANES), BF16),
        ],
        compiler_params=pltpu.CompilerParams(
            dimension_semantics=("arbitrary",), vmem_limit_bytes=VMEM_LIMIT),
        name="inproj",
    )(x2, n1, w_in, qnw, knw, cw)


def _stack_maps(q_ref, qz_ref, tq):
    q = q_ref[...]
    lane = lax.broadcasted_iota(jnp.int32, (tq, LANES), 1)
    zero = jnp.zeros_like(q)
    qz_ref[0:tq, :] = jnp.where(lane < DA_QK_DIM, q, zero)
    qz_ref[tq:2 * tq, :] = jnp.where(lane < DA_QK_DIM, zero, q)


def _causal_mask(tq):
    rows = lax.broadcasted_iota(jnp.int32, (2 * tq, tq), 0)
    cols = lax.broadcasted_iota(jnp.int32, (2 * tq, tq), 1)
    return jnp.where(rows >= tq, rows - tq, rows) >= cols


def _attn_finish(acc, l, lq1_ref, lk1_ref, lq2_ref, lk2_ref, onw_ref, o_ref, tq, lam_init):
    lam = (jnp.exp(jnp.sum(lq1_ref[...] * lk1_ref[...], axis=-1, keepdims=True))
           - jnp.exp(jnp.sum(lq2_ref[...] * lk2_ref[...], axis=-1, keepdims=True)) + lam_init)
    o = acc[:tq] / l[:tq] - lam * (acc[tq:] / l[tq:])
    ms = jnp.mean(o * o, axis=-1, keepdims=True)
    o = o * lax.rsqrt(ms + EPS) * onw_ref[...] * (1.0 - lam_init)
    o_ref[...] = o.astype(BF16)


def _attn_shift_body(shift_ref, q_ref, k_ref, v_ref, lq1_ref, lk1_ref, lq2_ref, lk2_ref, onw_ref, o_ref,
                     qz_ref, p_ref, l_ref, acc_ref, *, tq, lam_init, unroll, heads):
    i = pl.program_id(2)
    shift = shift_ref[0]
    cols = lambda hh: slice(hh * LANES, (hh + 1) * LANES)
    rows = lambda j: pl.ds(pl.multiple_of(j * tq, tq), tq)

    def probs(hh, j, diagonal):
        s = _dot_nt(qz_ref[hh], k_ref[rows(j), cols(hh)]) - shift
        if diagonal:
            s = jnp.where(_causal_mask(tq), s, NEG_BIG)
        p = jnp.exp2(s)
        psum = p[:, 0:LANES]
        for c0 in range(LANES, tq, LANES):
            psum = psum + p[:, c0:c0 + LANES]
        return p.astype(BF16), psum

    for hh in range(heads):
        _stack_maps(q_ref.at[:, cols(hh)], qz_ref.at[hh], tq)
        p, psum = probs(hh, i, True)
        p_ref[hh] = p
        l_ref[hh] = psum
        acc_ref[hh] = jnp.zeros((2 * tq, LANES), F32)

    def pv_prev(hh, j):
        return _dot(p_ref[hh], v_ref[rows(jnp.where(j == 0, i, j - 1)), cols(hh)])

    def steps(j0, n):
        acc = [pv_prev(hh, j0) for hh in range(heads)]
        psum = [None] * heads
        for u in range(n):
            for hh in range(heads):
                p, ps = probs(hh, j0 + u, False)
                psum[hh] = ps if psum[hh] is None else psum[hh] + ps
                if u + 1 < n:
                    acc[hh] = acc[hh] + _dot(p, v_ref[rows(j0 + u), cols(hh)])
                else:
                    p_ref[hh] = p
        for hh in range(heads):
            acc_ref[hh] += acc[hh]
            l_ref[hh] += psum[hh]

    start = 0
    n = unroll
    while n >= 1:
        trips = (i - start) // n

        def group(t, _, start=start, n=n):
            steps(start + t * n, n)
            return 0

        lax.fori_loop(0, trips, group, 0)
        start = start + trips * n
        n //= 2
    for hh in range(heads):
        acc = acc_ref[hh] + pv_prev(hh, i)
        l = jnp.sum(l_ref[hh], axis=-1, keepdims=True)
        _attn_finish(acc, l, lq1_ref, lk1_ref, lq2_ref, lk2_ref, onw_ref, o_ref.at[:, cols(hh)], tq, lam_init)


def _attn_online_body(shift_ref, q_ref, k_ref, v_ref, lq1_ref, lk1_ref, lq2_ref, lk2_ref, onw_ref, o_ref,
                      qz_ref, *, tq, lam_init):
    del shift_ref
    i = pl.program_id(2)
    _stack_maps(q_ref, qz_ref, tq)

    def step(j, carry, diagonal):
        m, l, acc = carry
        r0 = pl.multiple_of(j * tq, tq)
        s = _dot_nt(qz_ref[...], k_ref[pl.ds(r0, tq), :])
        if diagonal:
            s = jnp.where(_causal_mask(tq), s, NEG_BIG)
        m_new = jnp.maximum(m, jnp.max(s, axis=-1, keepdims=True))
        p = jnp.exp2(s - m_new)
        alpha = jnp.exp2(m - m_new)
        l = alpha * l + jnp.sum(p, axis=-1, keepdims=True)
        acc = alpha * acc + _dot(p.astype(BF16), v_ref[pl.ds(r0, tq), :])
        return m_new, l, acc

    init = (jnp.full((2 * tq, 1), NEG_BIG, F32), jnp.zeros((2 * tq, 1), F32),
            jnp.zeros((2 * tq, LANES), F32))
    carry = lax.fori_loop(0, i, lambda j, c: step(j, c, False), init)
    _, l, acc = step(i, carry, True)
    _attn_finish(acc, l, lq1_ref, lk1_ref, lq2_ref, lk2_ref, onw_ref, o_ref, tq, lam_init)


def _attention(shift, q, k, v, lq1, lk1, lq2, lk2, onw, B, S, tq, lam_init, online):
    nq = S // tq
    vec = pl.BlockSpec((1, DA_QK_DIM), lambda b, h, i: (0, 0))
    if online:
        heads = 1
        body = functools.partial(_attn_online_body, tq=tq, lam_init=lam_init)
        scratch = [pltpu.VMEM((2 * tq, LANES), BF16)]
    else:
        heads = DA_HEADS
        body = functools.partial(_attn_shift_body, tq=tq, lam_init=lam_init, unroll=4, heads=heads)
        scratch = [pltpu.VMEM((heads, 2 * tq, LANES), BF16),
                   pltpu.VMEM((heads, 2 * tq, tq), BF16),
                   pltpu.VMEM((heads, 2 * tq, LANES), F32),
                   pltpu.VMEM((heads, 2 * tq, LANES), F32)]
    width = heads * LANES
    return pl.pallas_call(
        body,
        grid=(B, DA_HEADS // heads, nq),
        in_specs=[
            pl.BlockSpec(memory_space=pltpu.SMEM),
            pl.BlockSpec((tq, width), lambda b, h, i: (b * nq + i, h)),
            pl.BlockSpec((S, width), lambda b, h, i: (b, h)),
            pl.BlockSpec((S, width), lambda b, h, i: (b, h)),
            vec, vec, vec, vec,
            pl.BlockSpec((1, LANES), lambda b, h, i: (0, 0)),
        ],
        out_specs=pl.BlockSpec((tq, width), lambda b, h, i: (b * nq + i, h)),
        out_shape=jax.ShapeDtypeStruct((B * S, HEAD_COLS), BF16),
        scratch_shapes=scratch,
        compiler_params=pltpu.CompilerParams(
            dimension_semantics=("arbitrary", "arbitrary", "arbitrary"), vmem_limit_bytes=VMEM_LIMIT),
        name="diff_attention_online" if online else "diff_attention",
    )(shift, q, k, v, lq1, lk1, lq2, lk2, onw)


def _merge_masks(row, col, n):
    masks = []
    b = 2
    while b < n:
        masks.append(((row // b) == (col // b) + 1) & ((row // (2 * b)) == (col // (2 * b))))
        b *= 2
    return masks


def _gdn_body(q_ref, k_ref, v_ref, z_ref, ab_ref, alog_ref, dtb_ref, onw_ref, o_ref,
              state_ref, l_ref, t_ref, rhs_ref, ou_ref, ku_ref, qa_ref, qk_ref, kdt_ref, egl_ref,
              fmask_ref, bmask_ref, lmask_ref, *, tb, group):
    C = GDN_CHUNK
    nc = tb // C
    blk = pl.program_id(1)
    n_levels = lmask_ref.shape[0]

    @pl.when(blk == 0)
    def _():
        state_ref[...] = jnp.zeros_like(state_ref)
        row = lax.broadcasted_iota(jnp.int32, (C, C), 0)
        col = lax.broadcasted_iota(jnp.int32, (C, C), 1)
        fmask_ref[0] = (row >= col).astype(F32)
        fmask_ref[1] = (row > col).astype(F32)
        bmask_ref[0] = (row == col).astype(BF16)
        bmask_ref[1] = ((row == col + 1) & (row % 2 == 1)).astype(BF16)
        for lv, m in enumerate(_merge_masks(row, col, C)):
            lmask_ref[lv] = m.astype(BF16)

    ab = ab_ref[...]
    sp_in = ab + dtb_ref[...]
    softplus = jnp.maximum(sp_in, 0.0) + jnp.log(1.0 + jnp.exp(-jnp.abs(sp_in)))
    gtok = (-LOG2E) * jnp.exp(alog_ref[...]) * softplus
    gtok = jnp.where(lax.broadcasted_iota(jnp.int32, (tb, LANES), 1) < DN_HEADS, gtok, 0.0)
    beta_all = 1.0 / (1.0 + jnp.exp(-ab))

    tri = fmask_ref[0].astype(BF16)

    for c in range(nc):
        r0 = c * C
        g1, g2, g3 = _split3(gtok[r0:r0 + C])
        gc = _dot(tri, g1) + _dot(tri, g2) + _dot(tri, g3)
        gct = gc.T
        for h in range(DN_HEADS):
            un = c * DN_HEADS + h
            c0 = h * DN_DIM
            qh_bf = q_ref[r0:r0 + C, c0:c0 + DN_DIM]
            qh = qh_bf.astype(F32)
            kh_bf = k_ref[r0:r0 + C, c0:c0 + DN_DIM]
            kh = kh_bf.astype(F32)
            vh = v_ref[r0:r0 + C, c0:c0 + DN_DIM].astype(F32)
            gcol = gc[:, h:h + 1]
            grow = gct[h:h + 1, :]
            beta = beta_all[r0:r0 + C, DN_HEADS + h:DN_HEADS + h + 1]
            decay = jnp.exp2(jnp.minimum(gcol - grow, 0.0))
            kb = kh * beta
            a = _dot_nt(jnp.concatenate([kb.astype(BF16), qh_bf], axis=0), kh_bf)
            Lm = (a[:C] * decay * fmask_ref[1]).astype(BF16)
            l_ref[un] = Lm
            t_ref[un] = bmask_ref[0] - Lm * bmask_ref[1]
            qk_ref[un] = (a[C:] * decay * fmask_ref[0]).astype(BF16)
            eg = jnp.exp2(gcol)
            rhs_ref[un] = jnp.concatenate([vh * beta, kb * eg], axis=1).astype(BF16)
            qa_ref[un, 0:C, :] = (qh * eg).astype(BF16)
            g_last = grow[:, C - 1:C]
            kdt_ref[un] = (kh.T * jnp.exp2(g_last - grow)).astype(BF16)
            egl_ref[un] = jnp.broadcast_to(jnp.exp2(g_last), (SUBLANES, LANES))

    nu = nc * DN_HEADS
    for lv in range(n_levels):
        for g0 in range(0, nu, group):
            units = list(range(g0, min(g0 + group, nu)))
            ps = [_dot(l_ref[un] * lmask_ref[lv], t_ref[un]).astype(BF16) for un in units]
            for un, p in zip(units, ps):
                t_ref[un] = t_ref[un] - _dot(t_ref[un], p).astype(BF16)
    for g0 in range(0, nu, group):
        units = list(range(g0, min(g0 + group, nu)))
        uws = [_dot(t_ref[un], rhs_ref[un]).astype(BF16) for un in units]
        for un, uw in zip(units, uws):
            r = _dot(jnp.concatenate([qk_ref[un], kdt_ref[un]], axis=0), uw)
            ou_ref[un] = r[:C, :DN_DIM]
            qa_ref[un, 0:C, :] = (qa_ref[un, 0:C, :].astype(F32) - r[:C, DN_DIM:]).astype(BF16)
            qa_ref[un, C:C + DN_DIM, :] = (-r[C:, DN_DIM:]).astype(BF16)
            ku_ref[un] = r[C:, :DN_DIM]

    for c in range(nc):
        r0 = c * C
        for h in range(DN_HEADS):
            un = c * DN_HEADS + h
            c0 = h * DN_DIM
            state = state_ref[h]
            r = _dot(qa_ref[un], state.astype(BF16))
            o = ou_ref[un] + r[:C]
            state_ref[h] = state * egl_ref[un][0:1, :] + r[C:] + ku_ref[un]
            zh = z_ref[r0:r0 + C, c0:c0 + DN_DIM]
            o = o * lax.rsqrt(jnp.mean(o * o, axis=-1, keepdims=True) + EPS) * onw_ref[...]
            o = o * _silu(zh)
            o_ref[r0:r0 + C, c0:c0 + DN_DIM] = o.astype(BF16)


def _gdn(gq, gk, gv, z, ab, alog, dtb, onw, B, S, tb):
    nb = S // tb
    C = GDN_CHUNK
    nu = (tb // C) * DN_HEADS
    const = lambda b, i: (0, 0)
    row = lambda b, i: (b * nb + i, 0)
    return pl.pallas_call(
        functools.partial(_gdn_body, tb=tb, group=16),
        grid=(B, nb),
        in_specs=[
            pl.BlockSpec((tb, HEAD_COLS), row),
            pl.BlockSpec((tb, HEAD_COLS), row),
            pl.BlockSpec((tb, HEAD_COLS), row),
            pl.BlockSpec((tb, HEAD_COLS), row),
            pl.BlockSpec((tb, LANES), row),
            pl.BlockSpec((1, LANES), const),
            pl.BlockSpec((1, LANES), const),
            pl.BlockSpec((1, LANES), const),
        ],
        out_specs=pl.BlockSpec((tb, HEAD_COLS), row),
        out_shape=jax.ShapeDtypeStruct((B * S, HEAD_COLS), BF16),
        scratch_shapes=[
            pltpu.VMEM((DN_HEADS, DN_DIM, DN_DIM), F32),
            pltpu.VMEM((nu, C, C), BF16),
            pltpu.VMEM((nu, C, C), BF16),
            pltpu.VMEM((nu, C, 2 * DN_DIM), BF16),
            pltpu.VMEM((nu, C, DN_DIM), F32),
            pltpu.VMEM((nu, DN_DIM, DN_DIM), F32),
            pltpu.VMEM((nu, C + DN_DIM, DN_DIM), BF16),
            pltpu.VMEM((nu, C, C), BF16),
            pltpu.VMEM((nu, DN_DIM, C), BF16),
            pltpu.VMEM((nu, SUBLANES, LANES), F32),
            pltpu.VMEM((2, C, C), F32),
            pltpu.VMEM((2, C, C), BF16),
            pltpu.VMEM((C.bit_length() - 2, C, C), BF16),
        ],
        compiler_params=pltpu.CompilerParams(
            dimension_semantics=("arbitrary", "arbitrary"), vmem_limit_bytes=VMEM_LIMIT),
        name="gated_deltanet",
    )(gq, gk, gv, z, ab, alog, dtb, onw)


def _mlp_body(x_ref, ma_ref, mb_ref, woa_ref, wob_ref, n2_ref, wup_hbm, wdn_hbm, o_ref,
              wup_ref, wdn_ref, stage_ref, sem, *, ff_chunk):
    @pl.when(pl.program_id(0) == 0)
    def _():
        d_model, ff = wup_ref.shape
        chunks = ([(wup_hbm.at[:, pl.ds(c0, d_model)], wup_ref.at[:, pl.ds(c0, d_model)])
                   for c0 in range(0, ff, d_model)]
                  + [(wdn_hbm.at[pl.ds(c0, d_model), :], wdn_ref.at[pl.ds(c0, d_model), :])
                     for c0 in range(0, ff, d_model)])

        def copy(k):
            return pltpu.make_async_copy(chunks[k][0], stage_ref.at[k % 2], sem.at[k % 2])

        copy(0).start()
        for k in range(len(chunks)):
            if k + 1 < len(chunks):
                copy(k + 1).start()
            copy(k).wait()
            chunks[k][1][...] = stage_ref[k % 2].astype(BF16)

    x1 = x_ref[...] + _dot(ma_ref[...], woa_ref[...]) + _dot(mb_ref[...], wob_ref[...])
    ms = jnp.mean(x1 * x1, axis=-1, keepdims=True)
    h = (x1 * lax.rsqrt(ms + EPS) * n2_ref[...]).astype(BF16)
    mlp = None
    for c0 in range(0, wup_ref.shape[1], ff_chunk):
        up = jnp.maximum(_dot(h, wup_ref[:, c0:c0 + ff_chunk]), 0.0)
        down = _dot((up * up).astype(BF16), wdn_ref[c0:c0 + ff_chunk, :])
        mlp = down if mlp is None else mlp + down
    o_ref[...] = x1 + mlp


def _mlp(x2, mix_a, mix_b, woa, wob, n2, wup, wdn, tm, ff_chunk):
    T, D = x2.shape
    const = lambda i: (0, 0)
    row = lambda i: (i, 0)
    return pl.pallas_call(
        functools.partial(_mlp_body, ff_chunk=ff_chunk),
        grid=(T // tm,),
        in_specs=[
            pl.BlockSpec((tm, D), row),
            pl.BlockSpec((tm, HEAD_COLS), row),
            pl.BlockSpec((tm, HEAD_COLS), row),
            pl.BlockSpec(woa.shape, const),
            pl.BlockSpec(wob.shape, const),
            pl.BlockSpec((1, D), const),
            pl.BlockSpec(memory_space=pl.ANY),
            pl.BlockSpec(memory_space=pl.ANY),
        ],
        out_specs=pl.BlockSpec((tm, D), row),
        out_shape=jax.ShapeDtypeStruct((T, D), F32),
        scratch_shapes=[
            pltpu.VMEM(wup.shape, BF16),
            pltpu.VMEM(wdn.shape, BF16),
            pltpu.VMEM((2, D, D), F32),
            pltpu.SemaphoreType.DMA((2,)),
        ],
        compiler_params=pltpu.CompilerParams(
            dimension_semantics=("arbitrary",), vmem_limit_bytes=VMEM_LIMIT),
        name="outproj_mlp",
    )(x2, mix_a, mix_b, woa, wob, n2, wup, wdn)


def _layer(x2, B, S, l, norm1_w, w_in, lambda_q1, lambda_k1, lambda_q2, lambda_k2, q_norm_w, k_norm_w,
           da_out_norm_w, conv_w, A_log, dt_bias, dn_out_norm_w, w_out, norm2_w, w_up, w_down,
           tm, tq, tb, tm_mlp, ff_chunk):
    D = x2.shape[1]
    tile2 = lambda v: jnp.concatenate([v, v]).reshape(1, LANES).astype(F32)
    pad_lanes = lambda v: jnp.pad(v.astype(F32), (0, LANES - v.shape[0])).reshape(1, LANES)

    q, k, v, gq, gk, gv, z, ab = _inproj(
        x2, norm1_w[l].reshape(1, D).astype(F32), w_in.astype(F32), l,
        tile2(q_norm_w[l]) * LOG2E,
        tile2(k_norm_w[l]) * DA_QK_DIM ** 0.5,
        0.5 * conv_w[l].astype(F32), S, tm)

    lam_init = 0.8 - 0.6 * math.exp(-0.3 * l)
    vec = lambda p: p[l].reshape(1, DA_QK_DIM).astype(F32)
    bound = 8.0 * jnp.max(jnp.abs(q_norm_w[l].astype(F32))) * jnp.max(jnp.abs(k_norm_w[l].astype(F32)))
    attn_args = ((bound * LOG2E).reshape(1), q, k, v, vec(lambda_q1), vec(lambda_k1), vec(lambda_q2),
                 vec(lambda_k2), da_out_norm_w[l].reshape(1, LANES).astype(F32))
    mix_a = lax.cond(bound <= MAX_FIXED_SHIFT,
                     lambda a: _attention(*a, B, S, tq, lam_init, online=False),
                     lambda a: _attention(*a, B, S, tq, lam_init, online=True), attn_args)

    mix_b = _gdn(gq, gk, gv, z, ab, pad_lanes(A_log[l]), pad_lanes(dt_bias[l]),
                 dn_out_norm_w[l].reshape(1, LANES).astype(F32), B, S, tb)

    wo = w_out[l].astype(BF16)
    return _mlp(x2, mix_a, mix_b, wo[:HEAD_COLS], wo[HEAD_COLS:], norm2_w[l].reshape(1, D).astype(F32),
                w_up[l].astype(F32), w_down[l].astype(F32), tm_mlp, ff_chunk)


def kernel(x, norm1_w, w_in, lambda_q1, lambda_k1, lambda_q2, lambda_k2, q_norm_w, k_norm_w, da_out_norm_w,
           conv_w, A_log, dt_bias, dn_out_norm_w, w_out, norm2_w, w_up, w_down):
    B, S, D = x.shape
    x2 = x.reshape(B * S, D)
    tm = min(512, S)
    tq = min(512, S)
    tb = min(1024, S)
    tm_mlp = min(512, S)
    ff_chunk = 1024
    assert S % tm == 0 and S % tq == 0 and S % tb == 0 and S % tm_mlp == 0, (S, tm, tq, tb, tm_mlp)
    assert tm % GDN_CHUNK == 0 and tb % GDN_CHUNK == 0 and tq % LANES == 0, (tm, tb, tq)
    assert w_in.shape[1:] == (D, 7 * HEAD_COLS + 2 * DN_HEADS) and w_up.shape[2] % ff_chunk == 0
    for l in range(w_in.shape[0]):
        x2 = _layer(x2, B, S, l, norm1_w, w_in, lambda_q1, lambda_k1, lambda_q2, lambda_k2, q_norm_w,
                    k_norm_w, da_out_norm_w, conv_w, A_log, dt_bias, dn_out_norm_w, w_out, norm2_w,
                    w_up, w_down, tm, tq, tb, tm_mlp, ff_chunk)
    return x2.reshape(B, S, D)
```
